```python
import jax, jax.numpy as jnp
from jax import lax
import numpy as np

D_MODEL = 2048
BATCH = 4
SEQ = 8192
DEPTH = 4

F32 = jnp.float32
NORM_EPS = 1e-6
HALF = D_MODEL // 2
DN_DK = 128
DN_DV = 128
DN_HEADS = HALF // DN_DV
DN_QK = DN_HEADS * DN_DK
DN_V = DN_HEADS * DN_DV
DN_CONV = 4
DN_CHUNK = 64
LRU_WIDTH = HALF
LRU_BLOCKS = 8
LRU_BLK = LRU_WIDTH // LRU_BLOCKS
LRU_CONV = 4
LRU_C = 8.0
SWA_DIM = 128
SWA_HEADS = HALF // SWA_DIM
SWA_W = SWA_HEADS * SWA_DIM
SWA_BRANCHES = ((128, 1), (512, 4), (2048, 16))
SWA_BLOCK = 128
RET_DK = 128
RET_DV = 256
RET_HEADS = HALF // RET_DV
RET_QK = RET_HEADS * RET_DK
RET_V = RET_HEADS * RET_DV
RET_CHUNK = 64
GN_EPS = 1e-5
D_FF = 4 * D_MODEL
PLE_DIM = 256
N_EVEN = (DEPTH + 1) // 2
N_ODD = DEPTH // 2
EV_SPLITS = (2 * DN_QK + DN_V, DN_V, DN_HEADS, DN_HEADS, LRU_WIDTH, LRU_WIDTH)
EV_IN = sum(EV_SPLITS)
EV_OUT = DN_V + LRU_WIDTH
OD_SPLITS = (SWA_W, SWA_W, SWA_W, RET_QK, RET_QK, RET_V, RET_V)
OD_IN = sum(OD_SPLITS)
OD_OUT = SWA_W + RET_V

kernel_name = "hybrid_deltanet_rglru_dilated_retention_trunk"


def rms_norm(x, w, eps=NORM_EPS):
    xf = x.astype(F32)
    y = xf * lax.rsqrt(jnp.mean(xf * xf, axis=-1, keepdims=True) + eps)
    return (y * w.astype(F32)).astype(x.dtype)


def l2_normalize(x, eps=1e-6):
    return x * lax.rsqrt(jnp.sum(x * x, axis=-1, keepdims=True) + eps)


def head_group_norm(x, eps=GN_EPS):
    mu = jnp.mean(x, axis=-1, keepdims=True)
    xc = x - mu
    return xc * lax.rsqrt(jnp.mean(xc * xc, axis=-1, keepdims=True) + eps)


def split_cols(x, sizes):
    return jnp.split(x, np.cumsum(sizes)[:-1].tolist(), axis=-1)


def to_heads(x, n_heads):
    b, t, _ = x.shape
    return x.reshape(b, t, n_heads, -1).transpose(0, 2, 1, 3)


def merge_heads(x):
    b, h, t, d = x.shape
    return x.transpose(0, 2, 1, 3).reshape(b, t, h * d)


def to_chunks(x, c):
    b, h, t = x.shape[:3]
    return jnp.moveaxis(x.reshape(b, h, t // c, c, *x.shape[3:]), 2, 0)


def from_chunks(x):
    n, b, h, c, d = x.shape
    return jnp.moveaxis(x, 0, 2).reshape(b, h, n * c, d)


def causal_dwconv(x, w):
    width, t = w.shape[0], x.shape[1]
    xp = jnp.pad(x, ((0, 0), (width - 1, 0), (0, 0)))
    y = xp[:, 0:t] * w[0]
    for j in range(1, width):
        y = y + xp[:, j:j + t] * w[j]
    return y


def gated_delta_rule(q, k, v, g, beta):
    b, h, _, dk = q.shape
    dv = v.shape[-1]
    c = DN_CHUNK
    q, k, v = to_chunks(q, c), to_chunks(k, c), to_chunks(v, c)
    g, beta = to_chunks(g, c), to_chunks(beta, c)
    gc = jnp.cumsum(g, axis=-1)
    causal = jnp.tril(jnp.ones((c, c), bool))
    strict = jnp.tril(jnp.ones((c, c), bool), -1)
    decay = jnp.exp(jnp.where(causal, gc[..., :, None] - gc[..., None, :], -jnp.inf))
    kb = k * beta[..., None]
    m = jnp.where(strict, jnp.einsum('nbhid,nbhjd->nbhij', kb, k) * decay, 0.0)
    rhs = jnp.concatenate([v * beta[..., None], kb * jnp.exp(gc)[..., None]], axis=-1)
    sol = lax.linalg.triangular_solve(jnp.eye(c, dtype=F32) + m, rhs, left_side=True, lower=True, unit_diagonal=True)
    u, w = sol[..., :dv], sol[..., dv:]
    qk = jnp.einsum('nbhid,nbhjd->nbhij', q, k) * decay
    q_dec = q * jnp.exp(gc)[..., None]
    k_dec = k * jnp.exp(gc[..., -1:] - gc)[..., None]
    g_tot = jnp.exp(gc[..., -1])[..., None, None]

    def step(state, xs):
        u_n, w_n, qk_n, qd_n, kd_n, gt_n = xs
        v_new = u_n - jnp.einsum('bhcd,bhde->bhce', w_n, state)
        o = jnp.einsum('bhcd,bhde->bhce', qd_n, state) + jnp.einsum('bhij,bhje->bhie', qk_n, v_new)
        state = state * gt_n + jnp.einsum('bhcd,bhce->bhde', kd_n, v_new)
        return state, o

    _, o = lax.scan(step, jnp.zeros((b, h, dk, dv), F32), (u, w, qk, q_dec, k_dec, g_tot))
    return from_chunks(o)


def rg_lru_branch(xr, yr, conv_w, conv_b, wa, ba, wx, bx, lam):
    b, t, _ = xr.shape
    xc = (causal_dwconv(xr, conv_w) + conv_b).astype(F32)
    xb = xc.reshape(b, t, LRU_BLOCKS, LRU_BLK)
    r = jax.nn.sigmoid(jnp.einsum('btgi,gij->btgj', xb, wa.astype(F32)).reshape(b, t, -1) + ba.astype(F32))
    i = jax.nn.sigmoid(jnp.einsum('btgi,gij->btgj', xb, wx.astype(F32)).reshape(b, t, -1) + bx.astype(F32))
    log_a = -LRU_C * r * jax.nn.softplus(-lam.astype(F32))
    a = jnp.exp(log_a)
    u = jnp.sqrt(-jnp.expm1(2.0 * log_a)) * (i * xc)

    def combine(e1, e2):
        a1, b1 = e1
        a2, b2 = e2
        return a1 * a2, a2 * b1 + b2

    _, hs = lax.associative_scan(combine, (a, u), axis=1)
    return hs * jax.nn.gelu(yr.astype(F32))


def even_mixer(hn, w_in, w_out, dn_conv_w, dn_a_log, dn_dt_bias, dn_norm_w,
               lru_conv_w, lru_conv_b, lru_wa, lru_ba, lru_wx, lru_bx, lru_lambda):
    proj = hn @ w_in
    qkv, z, b_raw, a_raw, xr, yr = split_cols(proj, EV_SPLITS)
    qkv = jax.nn.silu(causal_dwconv(qkv, dn_conv_w)).astype(F32)
    q, k, v = split_cols(qkv, (DN_QK, DN_QK, DN_V))
    q = l2_normalize(to_heads(q, DN_HEADS)) * (DN_DK ** -0.5)
    k = l2_normalize(to_heads(k, DN_HEADS))
    v = to_heads(v, DN_HEADS)
    beta = jax.nn.sigmoid(b_raw.astype(F32)).transpose(0, 2, 1)
    g = (-jnp.exp(dn_a_log.astype(F32)) * jax.nn.softplus(a_raw.astype(F32) + dn_dt_bias.astype(F32))).transpose(0, 2, 1)
    o = gated_delta_rule(q, k, v, g, beta)
    o = rms_norm(o, dn_norm_w) * jax.nn.silu(to_heads(z.astype(F32), DN_HEADS))
    y_a = merge_heads(o)
    y_b = rg_lru_branch(xr, yr, lru_conv_w, lru_conv_b, lru_wa, lru_ba, lru_wx, lru_bx, lru_lambda)
    return jnp.concatenate([y_a, y_b], axis=-1).astype(hn.dtype) @ w_out


def dilated_branch(q, k, v, slopes, window, dilation):
    b, h, t, dh = q.shape
    d = dilation
    span = window // dilation
    n_len = t // d
    nb = -(-n_len // SWA_BLOCK)
    lp = nb * SWA_BLOCK

    def to_res(x):
        x = x.reshape(b, h, n_len, d, dh).transpose(0, 1, 3, 2, 4)
        x = jnp.pad(x, ((0, 0), (0, 0), (0, 0), (0, lp - n_len), (0, 0)))
        return x.reshape(b, h, d, nb, SWA_BLOCK, dh)

    def with_prev(x):
        prev = jnp.pad(x[:, :, :, :-1], ((0, 0), (0, 0), (0, 0), (1, 0), (0, 0), (0, 0)))
        return jnp.concatenate([prev, x], axis=4)

    def from_res(x):
        x = x.reshape(b, h, d, lp, *x.shape[5:])[:, :, :, :n_len]
        x = jnp.moveaxis(x, 2, 3)
        return x.reshape(b, h, t, *x.shape[4:])

    qr = to_res(q)
    kk, vv = with_prev(to_res(k)), with_prev(to_res(v))
    iq = jnp.arange(SWA_BLOCK)
    ik = jnp.arange(2 * SWA_BLOCK)
    rel = SWA_BLOCK + iq[:, None] - ik[None, :]
    blk = jnp.arange(nb)
    valid = (rel >= 0) & (rel <= span) & ((blk[:, None, None] > 0) | (ik >= SWA_BLOCK)[None, None, :])
    s = jnp.einsum('bhrnqd,bhrnkd->bhrnqk', qr, kk)
    s = s - slopes[:, None, None, None, None] * (rel * d).astype(F32)
    s = jnp.where(valid, s, -jnp.inf)
    mx = jnp.max(s, axis=-1, keepdims=True)
    pr = jnp.exp(s - mx)
    den = jnp.sum(pr, axis=-1, keepdims=True)
    o = jnp.einsum('bhrnqk,bhrnkd->bhrnqd', pr, vv) / den
    lse = (mx + jnp.log(den))[..., 0]
    return from_res(o), from_res(lse)


def retention(q, k, v, log_gamma):
    b, h, _, dk = q.shape
    dv = v.shape[-1]
    c = RET_CHUNK
    idx = jnp.arange(c, dtype=F32)
    rel = idx[:, None] - idx[None, :]
    dmask = jnp.where(rel >= 0, jnp.exp(jnp.maximum(rel, 0.0)[None] * log_gamma[:, None, None]), 0.0)
    qc, kc, vc = to_chunks(q, c), to_chunks(k, c), to_chunks(v, c)
    intra = jnp.einsum('nbhij,nbhje->nbhie', jnp.einsum('nbhid,nbhjd->nbhij', qc, kc) * dmask, vc)
    q_dec = qc * jnp.exp((idx + 1.0)[None, :] * log_gamma[:, None])[:, :, None]
    k_dec = kc * jnp.exp((c - 1.0 - idx)[None, :] * log_gamma[:, None])[:, :, None]
    chunk_decay = jnp.exp(c * log_gamma)[:, None, None]

    def step(state, xs):
        qd, kd, vn = xs
        o = jnp.einsum('bhcd,bhde->bhce', qd, state)
        state = state * chunk_decay + jnp.einsum('bhcd,bhce->bhde', kd, vn)
        return state, o

    _, inter = lax.scan(step, jnp.zeros((b, h, dk, dv), F32), (q_dec, k_dec, vc))
    return from_chunks(intra + inter)


def odd_mixer(hn, w_in, w_out):
    proj = hn @ w_in
    cq, ck, cv, rq, rk, rv, rg = split_cols(proj.astype(F32), OD_SPLITS)
    q = to_heads(cq, SWA_HEADS) * (SWA_DIM ** -0.5)
    k = to_heads(ck, SWA_HEADS)
    v = to_heads(cv, SWA_HEADS)
    slopes = jnp.exp2(-8.0 * jnp.arange(1, SWA_HEADS + 1, dtype=F32) / SWA_HEADS)
    outs, lses = [], []
    for window, dilation in SWA_BRANCHES:
        o_i, lse_i = dilated_branch(q, k, v, slopes, window, dilation)
        outs.append(o_i)
        lses.append(lse_i)
    wts = jax.nn.softmax(jnp.stack(lses), axis=0)
    y_c = merge_heads(jnp.sum(jnp.stack(outs) * wts[..., None], axis=0))
    log_gamma = jnp.log1p(-jnp.exp2(-5.0 - jnp.arange(RET_HEADS, dtype=F32)))
    o_r = retention(to_heads(rq, RET_HEADS), to_heads(rk, RET_HEADS) * (RET_DK ** -0.5), to_heads(rv, RET_HEADS), log_gamma)
    o_r = head_group_norm(o_r) * jax.nn.silu(to_heads(rg, RET_HEADS))
    y_d = merge_heads(o_r)
    return jnp.concatenate([y_c, y_d], axis=-1).astype(hn.dtype) @ w_out


def setup_inputs(seed: int = 0) -> dict:
    key = jax.random.key(seed)
    ks = iter(jax.random.split(key, 32))

    def normal(shape, fan_in):
        return jax.random.normal(next(ks), shape, F32) * (fan_in ** -0.5)

    def gain(shape):
        return 1.0 + 0.02 * jax.random.normal(next(ks), shape, F32)

    def small(shape):
        return 0.02 * jax.random.normal(next(ks), shape, F32)

    x = jax.random.normal(next(ks), (BATCH, SEQ, D_MODEL), F32)
    p = jax.random.normal(next(ks), (DEPTH, BATCH, SEQ, PLE_DIM), F32)
    ln_mix_w = gain((DEPTH, D_MODEL))
    ln_mlp_w = gain((DEPTH, D_MODEL))
    ln_ple_w = gain((DEPTH, D_MODEL))
    w_up = normal((DEPTH, D_MODEL, D_FF), D_MODEL)
    w_down = normal((DEPTH, D_FF, D_MODEL), D_FF)
    w_ple_proj = normal((DEPTH, PLE_DIM, D_MODEL), PLE_DIM)
    w_ple_gate = normal((DEPTH, D_MODEL, D_MODEL), D_MODEL)
    ln_final_w = gain((D_MODEL,))
    ev_w_in = normal((N_EVEN, D_MODEL, EV_IN), D_MODEL)
    ev_w_out = normal((N_EVEN, EV_OUT, D_MODEL), EV_OUT)
    dn_conv_w = normal((N_EVEN, DN_CONV, 2 * DN_QK + DN_V), DN_CONV)
    dn_a_log = jnp.log(jax.random.uniform(next(ks), (N_EVEN, DN_HEADS), F32, 1.0, 16.0))
    dt = jnp.exp(jax.random.uniform(next(ks), (N_EVEN, DN_HEADS), F32, float(np.log(1e-3)), float(np.log(1e-1))))
    dn_dt_bias = dt + jnp.log(-jnp.expm1(-dt))
    dn_norm_w = gain((N_EVEN, DN_DV))
    lru_conv_w = normal((N_EVEN, LRU_CONV, LRU_WIDTH), LRU_CONV)
    lru_conv_b = small((N_EVEN, LRU_WIDTH))
    lru_wa = normal((N_EVEN, LRU_BLOCKS, LRU_BLK, LRU_BLK), LRU_BLK)
    lru_ba = small((N_EVEN, LRU_WIDTH))
    lru_wx = normal((N_EVEN, LRU_BLOCKS, LRU_BLK, LRU_BLK), LRU_BLK)
    lru_bx = small((N_EVEN, LRU_WIDTH))
    a_pow_c = jax.random.uniform(next(ks), (N_EVEN, LRU_WIDTH), F32, 0.9, 0.999)
    log_a = jnp.log(a_pow_c) / LRU_C
    lru_lambda = log_a - jnp.log(-jnp.expm1(log_a))
    od_w_in = normal((N_ODD, D_MODEL, OD_IN), D_MODEL)
    od_w_out = normal((N_ODD, OD_OUT, D_MODEL), OD_OUT)
    return {"x": x, "p": p, "ln_mix_w": ln_mix_w, "ln_mlp_w": ln_mlp_w, "ln_ple_w": ln_ple_w,
            "w_up": w_up, "w_down": w_down, "w_ple_proj": w_ple_proj, "w_ple_gate": w_ple_gate,
            "ln_final_w": ln_final_w, "ev_w_in": ev_w_in, "ev_w_out": ev_w_out, "dn_conv_w": dn_conv_w,
            "dn_a_log": dn_a_log, "dn_dt_bias": dn_dt_bias, "dn_norm_w": dn_norm_w,
            "lru_conv_w": lru_conv_w, "lru_conv_b": lru_conv_b, "lru_wa": lru_wa, "lru_ba": lru_ba,
            "lru_wx": lru_wx, "lru_bx": lru_bx, "lru_lambda": lru_lambda,
            "od_w_in": od_w_in, "od_w_out": od_w_out}


def reference(x, p, ln_mix_w, ln_mlp_w, ln_ple_w, w_up, w_down, w_ple_proj, w_ple_gate, ln_final_w,
              ev_w_in, ev_w_out, dn_conv_w, dn_a_log, dn_dt_bias, dn_norm_w,
              lru_conv_w, lru_conv_b, lru_wa, lru_ba, lru_wx, lru_bx, lru_lambda,
              od_w_in, od_w_out):
    h = x
    for i in range(DEPTH):
        j = i // 2
        hn = rms_norm(h, ln_mix_w[i])
        if i % 2 == 0:
            mix = even_mixer(hn, ev_w_in[j], ev_w_out[j], dn_conv_w[j], dn_a_log[j], dn_dt_bias[j], dn_norm_w[j],
                             lru_conv_w[j], lru_conv_b[j], lru_wa[j], lru_ba[j], lru_wx[j], lru_bx[j], lru_lambda[j])
        else:
            mix = odd_mixer(hn, od_w_in[j], od_w_out[j])
        h = h + mix
        hn = rms_norm(h, ln_mlp_w[i])
        h = h + jnp.square(jax.nn.relu(hn @ w_up[i])) @ w_down[i]
        hn = rms_norm(h, ln_ple_w[i])
        h = h + jax.nn.sigmoid(hn @ w_ple_gate[i]) * (p[i] @ w_ple_proj[i])
    return rms_norm(h, ln_final_w)
```

```python
import functools

import jax
import jax.numpy as jnp
from jax import lax
from jax.experimental import pallas as pl
from jax.experimental.pallas import tpu as pltpu

F32 = jnp.float32
BF16 = jnp.bfloat16

V7X_LANES = 128
V7X_SUBLANES = 8
V7X_VMEM_LIMIT_BYTES = 56 * 1024 * 1024

NORM_EPS = 1e-6
GN_EPS = 1e-5
LRU_C = 8.0
DN_HEADS = 8
LRU_BLOCKS = 8
SWA_HEADS = 8
RET_HEADS = 4
RET_DV = 256
SWA_DILATIONS = (1, 4, 16)
SWA_SPAN = 128
SWA_BLOCK = 128
DN_CHUNK = 128
RET_CHUNK = 256
MASK_VALUE = -1e30


def _cparams(*sem):
    return pltpu.CompilerParams(dimension_semantics=sem, vmem_limit_bytes=V7X_VMEM_LIMIT_BYTES)


def _dot(a, b):
    return jnp.dot(a, b, preferred_element_type=F32)


def _dot_nt(a, b):
    return lax.dot_general(a, b, (((1,), (1,)), ((), ())), preferred_element_type=F32)


def _dot_tn(a, b):
    return lax.dot_general(a, b, (((0,), (0,)), ((), ())), preferred_element_type=F32)


def _split_bf16(x):
    hi = x.astype(BF16)
    lo = (x - hi.astype(F32)).astype(BF16)
    return hi, lo


def _dot_f32(a, b):
    ah, al = _split_bf16(a)
    bh, bl = _split_bf16(b)
    return _dot(ah, bh) + (_dot(ah, bl) + _dot(al, bh))


def _rms(x, w):
    return x * lax.rsqrt(jnp.mean(x * x, axis=-1, keepdims=True) + NORM_EPS) * w


def _silu(x):
    return x * jax.nn.sigmoid(x)


def _expm1(x):
    u = jnp.exp(x)
    lg = jnp.where(u == 1.0, 1.0, jnp.log(u))
    return jnp.where(u == 1.0, x, (u - 1.0) * x / lg)


def _tile(n, cap, quantum):
    if n <= cap:
        return n
    best = None
    for c in range(quantum, cap + 1, quantum):
        if n % c == 0:
            best = c
    assert best is not None, (n, cap, quantum)
    return best


def _norm_mm_kernel(x_ref, g_ref, w_ref, o_ref, hn_ref, *, act):
    @pl.when(pl.program_id(1) == 0)
    def _():
        hn_ref[...] = _rms(x_ref[...], g_ref[...]).astype(BF16)

    a = _dot(hn_ref[...], w_ref[...])
    if act == "relu2":
        a = jnp.square(jnp.maximum(a, 0.0))
    o_ref[...] = a.astype(o_ref.dtype)


def _norm_matmul(x, g, w, *, act, out_dtype, tm_cap=1024, tn_cap=1280):
    m, k = x.shape
    n = w.shape[1]
    tm = _tile(m, tm_cap, 256)
    tn = _tile(n, tn_cap, 256)
    return pl.pallas_call(
        functools.partial(_norm_mm_kernel, act=act),
        grid=(m // tm, n // tn),
        in_specs=[
            pl.BlockSpec((tm, k), lambda i, j: (i, 0)),
            pl.BlockSpec((1, k), lambda i, j: (0, 0)),
            pl.BlockSpec((k, tn), lambda i, j: (0, j)),
        ],
        out_specs=pl.BlockSpec((tm, tn), lambda i, j: (i, j)),
        out_shape=jax.ShapeDtypeStruct((m, n), out_dtype),
        scratch_shapes=[pltpu.VMEM((tm, k), BF16)],
        compiler_params=_cparams("parallel", "arbitrary"),
        name="norm_matmul",
    )(x, g.reshape(1, k), w)


def _mm_res_kernel(*refs, n_x):
    xs, ws = refs[:n_x], refs[n_x:2 * n_x]
    res_ref, o_ref = refs[2 * n_x], refs[2 * n_x + 1]
    acc = res_ref[...]
    for x_ref, w_ref in zip(xs, ws):
        acc = acc + _dot(x_ref[...], w_ref[...])
    o_ref[...] = acc


def _matmul_residual(xs, w, res, *, tm_cap=512, tn_cap=512):
    m, n = res.shape
    tm = _tile(m, tm_cap, 256)
    tn = _tile(n, tn_cap, 256)
    in_specs, row = [], 0
    for x in xs:
        in_specs.append(pl.BlockSpec((tm, x.shape[1]), lambda i, j: (i, 0)))
    w_specs = []
    for x in xs:
        kk = x.shape[1]
        assert row % kk == 0
        w_specs.append(pl.BlockSpec((kk, tn), functools.partial(lambda i, j, rb: (rb, j), rb=row // kk)))
        row += kk
    assert row == w.shape[0]
    return pl.pallas_call(
        functools.partial(_mm_res_kernel, n_x=len(xs)),
        grid=(m // tm, n // tn),
        in_specs=in_specs + w_specs + [pl.BlockSpec((tm, tn), lambda i, j: (i, j))],
        out_specs=pl.BlockSpec((tm, tn), lambda i, j: (i, j)),
        out_shape=jax.ShapeDtypeStruct((m, n), F32),
        compiler_params=_cparams("parallel", "parallel"),
        name="matmul_residual",
    )(*xs, *([w] * len(xs)), res)


def _ple_kernel(x_ref, g_ref, wg_ref, p_ref, wp_ref, gf_ref, o_ref, *, final_norm):
    d = x_ref.shape[1]
    hn = _rms(x_ref[...], g_ref[...]).astype(BF16)
    pb = p_ref[...].astype(BF16)
    nh = d // 2 if d % (2 * V7X_LANES) == 0 else d
    for c0 in range(0, d, nh):
        gate = jax.nn.sigmoid(_dot(hn, wg_ref[:, c0:c0 + nh]))
        pp = _dot(pb, wp_ref[:, c0:c0 + nh])
        o_ref[:, c0:c0 + nh] = x_ref[:, c0:c0 + nh] + gate * pp
    if final_norm:
        o_ref[...] = _rms(o_ref[...], gf_ref[...])


def _ple(x, g, wg, p, wp, gf, *, final_norm, tm_cap=512):
    m, d = x.shape
    pd = p.shape[1]
    tm = _tile(m, tm_cap, 256)
    return pl.pallas_call(
        functools.partial(_ple_kernel, final_norm=final_norm),
        grid=(m // tm,),
        in_specs=[
            pl.BlockSpec((tm, d), lambda i: (i, 0)),
            pl.BlockSpec((1, d), lambda i: (0, 0)),
            pl.BlockSpec((d, d), lambda i: (0, 0)),
            pl.BlockSpec((tm, pd), lambda i: (i, 0)),
            pl.BlockSpec((pd, d), lambda i: (0, 0)),
            pl.BlockSpec((1, d), lambda i: (0, 0)),
        ],
        out_specs=pl.BlockSpec((tm, d), lambda i: (i, 0)),
        out_shape=jax.ShapeDtypeStruct((m, d), F32),
        compiler_params=_cparams("parallel"),
        name="ple",
    )(x, g.reshape(1, d), wg, p, wp, gf.reshape(1, d))


def _causal_conv(x_ref, halo_ref, w_ref, first):
    x = x_ref[0]
    halo = jnp.where(first, 0.0, halo_ref[0])
    xp = jnp.concatenate([halo, x], axis=0)
    w = w_ref[...]
    y = pltpu.roll(xp, 3, axis=0)[V7X_SUBLANES:] * w[0:1]
    y = y + pltpu.roll(xp, 2, axis=0)[V7X_SUBLANES:] * w[1:2]
    y = y + pltpu.roll(xp, 1, axis=0)[V7X_SUBLANES:] * w[2:3]
    return y + x * w[3:4]


def _halo_map(col, rows_per_tile):
    nb = rows_per_tile // V7X_SUBLANES
    return lambda b, h, t: (b, jnp.maximum(t * nb - 1, 0), col(h))


def _unit_lower_inverse(m_strict, row, col):
    c = m_strict.shape[0]
    eye = (row == col).astype(F32)
    diag16 = (row >> 4) == (col >> 4)
    n = jnp.where(diag16, m_strict, 0.0)
    inv = eye - n
    q = _dot_f32(n, n)
    inv = inv + _dot_f32(inv, q)
    q = _dot_f32(q, q)
    inv = inv + _dot_f32(inv, q)
    q = _dot_f32(q, q)
    inv = inv + _dot_f32(inv, q)
    shift = 4
    while (1 << shift) < c:
        off = ((row >> (shift + 1)) == (col >> (shift + 1))) & ((row >> shift) != (col >> shift))
        cpart = jnp.where(off, m_strict, 0.0)
        inv = inv - _dot_f32(_dot_f32(inv, cpart), inv)
        shift += 1
    return inv


def _delta_kernel(q_ref, qh_ref, k_ref, kh_ref, v_ref, vh_ref, z_ref, ba_ref,
                  wq_ref, wk_ref, wv_ref, alog_ref, dtb_ref, nw_ref, o_ref, s_ref):
    h = pl.program_id(1)
    first = pl.program_id(2) == 0
    tt = q_ref.shape[1]
    c = DN_CHUNK

    @pl.when(first)
    def _():
        s_ref[...] = jnp.zeros_like(s_ref)

    def l2n(x):
        return x * lax.rsqrt(jnp.sum(x * x, axis=-1, keepdims=True) + 1e-6)

    q = l2n(_silu(_causal_conv(q_ref, qh_ref, wq_ref, first))) * (V7X_LANES ** -0.5)
    k = l2n(_silu(_causal_conv(k_ref, kh_ref, wk_ref, first)))
    v = _silu(_causal_conv(v_ref, vh_ref, wv_ref, first))
    z = z_ref[0]

    ba = ba_ref[0]
    lane = lax.broadcasted_iota(jnp.int32, ba.shape, 1)
    beta = jnp.sum(jnp.where(lane == h, jax.nn.sigmoid(ba), 0.0), axis=-1, keepdims=True)
    g_all = -jnp.exp(alog_ref[...]) * jax.nn.softplus(ba + dtb_ref[...])
    g = jnp.sum(jnp.where(lane == h + DN_HEADS, g_all, 0.0), axis=-1, keepdims=True)

    row = lax.broadcasted_iota(jnp.int32, (c, c), 0)
    col = lax.broadcasted_iota(jnp.int32, (c, c), 1)
    causal = row >= col
    strict = row > col
    ltri = causal.astype(BF16)

    state = s_ref[...]
    for ci in range(tt // c):
        sl = slice(ci * c, (ci + 1) * c)
        qc, kc, vc, bc = q[sl], k[sl], v[sl], beta[sl]
        g_hi, g_lo = _split_bf16(jnp.broadcast_to(g[sl], (c, c)))
        g_lo2 = (jnp.broadcast_to(g[sl], (c, c)) - g_hi.astype(F32) - g_lo.astype(F32)).astype(BF16)
        gc = _dot(ltri, g_hi) + (_dot(ltri, g_lo) + _dot(ltri, g_lo2))
        diff = gc - gc.T
        decay = jnp.where(causal, jnp.exp(jnp.where(causal, diff, 0.0)), 0.0)
        kb = kc * bc
        m_strict = jnp.where(strict, _dot_nt(kb.astype(BF16), kc.astype(BF16)) * decay, 0.0)
        inv = _unit_lower_inverse(m_strict, row, col)
        egc = jnp.exp(gc)
        u = _dot_f32(inv, vc * bc)
        w = _dot_f32(inv, kb * egc)
        qk = _dot_nt(qc.astype(BF16), kc.astype(BF16)) * decay
        q_dec = qc * egc
        last = gc[c - 1:c, :]
        k_dec = kc * jnp.exp(last - gc)
        g_tot = jnp.exp(last)
        v_new = u - _dot(w.astype(BF16), state.astype(BF16))
        o = _dot(q_dec.astype(BF16), state.astype(BF16)) + _dot(qk.astype(BF16), v_new.astype(BF16))
        state = state * g_tot + _dot_tn(k_dec.astype(BF16), v_new.astype(BF16))
        o_ref[0, sl, :] = (_rms(o, nw_ref[...]) * _silu(z[sl])).astype(o_ref.dtype)
    s_ref[...] = state


def _delta_mixer(proj, conv_w, a_log, dt_bias, norm_w, *, tt_cap=512):
    b, t, _ = proj.shape
    tt = _tile(t, tt_cap, DN_CHUNK)
    hh = DN_HEADS
    blk = lambda off: pl.BlockSpec((1, tt, V7X_LANES), lambda bi, h, ti, off=off: (bi, ti, off + h))
    halo = lambda off: pl.BlockSpec((1, V7X_SUBLANES, V7X_LANES), _halo_map(lambda h, off=off: off + h, tt))
    cw = lambda off: pl.BlockSpec((4, V7X_LANES), lambda bi, h, ti, off=off: (0, off + h))
    row_spec = pl.BlockSpec((1, V7X_LANES), lambda bi, h, ti: (0, 0))
    pad = jnp.zeros((V7X_LANES - 2 * hh,), F32)
    alog_row = jnp.concatenate([jnp.zeros((hh,), F32), a_log, pad]).reshape(1, V7X_LANES)
    dtb_row = jnp.concatenate([jnp.zeros((hh,), F32), dt_bias, pad]).reshape(1, V7X_LANES)
    ba_col = 6 * hh
    return pl.pallas_call(
        _delta_kernel,
        grid=(b, hh, t // tt),
        in_specs=[
            blk(0), halo(0), blk(hh), halo(hh), blk(2 * hh), halo(2 * hh), blk(3 * hh),
            pl.BlockSpec((1, tt, V7X_LANES), lambda bi, h, ti: (bi, ti, ba_col)),
            cw(0), cw(hh), cw(2 * hh), row_spec, row_spec, row_spec,
        ],
        out_specs=pl.BlockSpec((1, tt, V7X_LANES), lambda bi, h, ti: (bi, ti, h)),
        out_shape=jax.ShapeDtypeStruct((b, t, hh * V7X_LANES), BF16),
        scratch_shapes=[pltpu.VMEM((V7X_LANES, V7X_LANES), F32)],
        compiler_params=_cparams("parallel", "parallel", "arbitrary"),
        name="delta_mixer",
    )(proj, proj, proj, proj, proj, proj, proj, proj, conv_w, conv_w, conv_w,
      alog_row, dtb_row, norm_w.reshape(1, V7X_LANES))


def _lru_kernel(x_ref, xh_ref, y_ref, cw_ref, cb_ref, wa_ref, ba_ref, wx_ref, bx_ref, lam_ref,
                o_ref, h_ref):
    first = pl.program_id(2) == 0
    tt = x_ref.shape[1]

    @pl.when(first)
    def _():
        h_ref[...] = jnp.zeros_like(h_ref)

    xc = _causal_conv(x_ref, xh_ref, cw_ref, first) + cb_ref[...]
    xb = xc.astype(BF16)
    r = jax.nn.sigmoid(_dot(xb, wa_ref[0]) + ba_ref[...])
    i = jax.nn.sigmoid(_dot(xb, wx_ref[0]) + bx_ref[...])
    log_a = -LRU_C * r * jax.nn.softplus(-lam_ref[...])
    a = jnp.exp(log_a)
    u = jnp.sqrt(-_expm1(2.0 * log_a)) * (i * xc)

    rowi = lax.broadcasted_iota(jnp.int32, (tt, V7X_LANES), 0)
    s = 1
    while s < tt:
        if s < V7X_SUBLANES:
            a_sh = jnp.where(rowi >= s, pltpu.roll(a, s, axis=0), 1.0)
            u_sh = jnp.where(rowi >= s, pltpu.roll(u, s, axis=0), 0.0)
        else:
            a_sh = jnp.concatenate([jnp.ones((s, V7X_LANES), F32), a[:tt - s]], axis=0)
            u_sh = jnp.concatenate([jnp.zeros((s, V7X_LANES), F32), u[:tt - s]], axis=0)
        u = a * u_sh + u
        a = a * a_sh
        s *= 2
    hs = u + a * h_ref[...]
    h_ref[...] = hs[tt - 1:tt]
    o_ref[0] = (hs * jax.nn.gelu(y_ref[0])).astype(o_ref.dtype)


def _lru_mixer(proj, conv_w, conv_b, wa, ba, wx, bx, lam, *, x_col, y_col, tt_cap=512):
    b, t, _ = proj.shape
    tt = _tile(t, tt_cap, 16)
    gg = LRU_BLOCKS
    vec = lambda a: a.reshape(1, gg * V7X_LANES)
    vspec = pl.BlockSpec((1, V7X_LANES), lambda bi, g, ti: (0, g))
    wspec = pl.BlockSpec((1, V7X_LANES, V7X_LANES), lambda bi, g, ti: (g, 0, 0))
    return pl.pallas_call(
        _lru_kernel,
        grid=(b, gg, t // tt),
        in_specs=[
            pl.BlockSpec((1, tt, V7X_LANES), lambda bi, g, ti: (bi, ti, x_col + g)),
            pl.BlockSpec((1, V7X_SUBLANES, V7X_LANES), _halo_map(lambda g: x_col + g, tt)),
            pl.BlockSpec((1, tt, V7X_LANES), lambda bi, g, ti: (bi, ti, y_col + g)),
            pl.BlockSpec((4, V7X_LANES), lambda bi, g, ti: (0, g)),
            vspec, wspec, vspec, wspec, vspec, vspec,
        ],
        out_specs=pl.BlockSpec((1, tt, V7X_LANES), lambda bi, g, ti: (bi, ti, g)),
        out_shape=jax.ShapeDtypeStruct((b, t, gg * V7X_LANES), BF16),
        scratch_shapes=[pltpu.VMEM((1, V7X_LANES), F32)],
        compiler_params=_cparams("parallel", "parallel", "arbitrary"),
        name="lru_mixer",
    )(proj, proj, proj, conv_w, vec(conv_b), wa.astype(BF16), vec(ba), wx.astype(BF16), vec(bx), vec(lam))


def _dilated_kernel(q_ref, kc_ref, kp_ref, vc_ref, vp_ref, o_ref, kk_ref, vv_ref, m_ref, l_ref, acc_ref):
    h = pl.program_id(1)
    has_prev = pl.program_id(2) > 0
    tq = q_ref.shape[1]
    bq = SWA_BLOCK
    kk_ref[0:tq, :] = kp_ref[0]
    kk_ref[tq:2 * tq, :] = kc_ref[0]
    vv_ref[0:tq, :] = vp_ref[0]
    vv_ref[tq:2 * tq, :] = vc_ref[0]

    slope = jnp.exp2(-(jnp.full((1, 1), h, jnp.int32).astype(F32) + 1.0) * (8.0 / SWA_HEADS))
    iq = lax.broadcasted_iota(jnp.int32, (bq, 2 * bq), 0)
    ik = lax.broadcasted_iota(jnp.int32, (bq, 2 * bq), 1)
    rel = bq + iq - ik
    in_window = (rel >= 0) & (rel <= SWA_SPAN)
    relf = rel.astype(F32)
    scale = V7X_LANES ** -0.5

    for bi, d in enumerate(SWA_DILATIONS):
        nblk = tq // (bq * d)
        assert nblk >= 1 and nblk & (nblk - 1) == 0
        bias = (slope * float(d)) * relf

        def rows(ref, base, d=d):
            if d == 1:
                return ref[pl.ds(base, bq), :]
            return ref[pl.ds(base, bq, stride=d), :]

        def body(idx, carry, d=d, nblk=nblk, bias=bias, bi=bi):
            r = idx >> (nblk.bit_length() - 1)
            n = idx & (nblk - 1)
            base = n * (bq * d) + r
            qb = (rows(q_ref.at[0], base) * scale).astype(BF16)
            kcat = jnp.concatenate([rows(kk_ref, tq + base - bq * d), rows(kk_ref, tq + base)], axis=0)
            vcat = jnp.concatenate([rows(vv_ref, tq + base - bq * d), rows(vv_ref, tq + base)], axis=0)
            s = _dot_nt(qb, kcat.astype(BF16)) - bias
            valid = in_window & ((ik >= bq) | (n > 0) | has_prev)
            s = jnp.where(valid, s, MASK_VALUE)
            m_b = jnp.max(s, axis=-1, keepdims=True)
            p = jnp.exp(s - m_b)
            l_b = jnp.sum(p, axis=-1, keepdims=True)
            acc_b = _dot(p.astype(BF16), vcat.astype(BF16))
            m_b = jnp.broadcast_to(m_b, (bq, V7X_LANES))
            l_b = jnp.broadcast_to(l_b, (bq, V7X_LANES))
            if bi > 0:
                m_o, l_o, acc_o = rows(m_ref, base), rows(l_ref, base), rows(acc_ref, base)
                m_n = jnp.maximum(m_o, m_b)
                alpha = jnp.exp(m_o - m_n)
                beta = jnp.exp(m_b - m_n)
                m_b = m_n
                l_b = alpha * l_o + beta * l_b
                acc_b = alpha * acc_o + beta * acc_b
            if d == 1:
                m_ref[pl.ds(base, bq), :] = m_b
                l_ref[pl.ds(base, bq), :] = l_b
                acc_ref[pl.ds(base, bq), :] = acc_b
            else:
                m_ref[pl.ds(base, bq, stride=d), :] = m_b
                l_ref[pl.ds(base, bq, stride=d), :] = l_b
                acc_ref[pl.ds(base, bq, stride=d), :] = acc_b
            return carry

        lax.fori_loop(0, d * nblk, body, 0)

    o_ref[0] = (acc_ref[...] / l_ref[...]).astype(o_ref.dtype)


def _dilated_mixer(proj, *, q_col, k_col, v_col, tq=2048):
    b, t, _ = proj.shape
    assert t % tq == 0 and tq % (SWA_BLOCK * max(SWA_DILATIONS)) == 0
    hh = SWA_HEADS
    cur = lambda off: pl.BlockSpec((1, tq, V7X_LANES), lambda bi, h, ti, off=off: (bi, ti, off + h))
    prev = lambda off: pl.BlockSpec((1, tq, V7X_LANES),
                                    lambda bi, h, ti, off=off: (bi, jnp.maximum(ti - 1, 0), off + h))
    return pl.pallas_call(
        _dilated_kernel,
        grid=(b, hh, t // tq),
        in_specs=[cur(q_col), cur(k_col), prev(k_col), cur(v_col), prev(v_col)],
        out_specs=pl.BlockSpec((1, tq, V7X_LANES), lambda bi, h, ti: (bi, ti, h)),
        out_shape=jax.ShapeDtypeStruct((b, t, hh * V7X_LANES), BF16),
        scratch_shapes=[pltpu.VMEM((2 * tq, V7X_LANES), F32), pltpu.VMEM((2 * tq, V7X_LANES), F32),
                        pltpu.VMEM((tq, V7X_LANES), F32), pltpu.VMEM((tq, V7X_LANES), F32),
                        pltpu.VMEM((tq, V7X_LANES), F32)],
        compiler_params=_cparams("parallel", "parallel", "parallel"),
        name="dilated_mixer",
    )(proj, proj, proj, proj, proj)


def _retention_kernel(q_ref, k_ref, v_ref, g_ref, o_ref, s_ref):
    h = pl.program_id(1)
    c = q_ref.shape[1]

    @pl.when(pl.program_id(2) == 0)
    def _():
        s_ref[...] = jnp.zeros_like(s_ref)

    hf = jnp.full((1, 1), h, jnp.int32).astype(F32)
    log_gamma = jnp.log1p(-jnp.exp2(-5.0 - hf))
    q = q_ref[0]
    k = k_ref[0] * (V7X_LANES ** -0.5)
    v = v_ref[0].astype(BF16)
    row = lax.broadcasted_iota(jnp.int32, (c, c), 0)
    col = lax.broadcasted_iota(jnp.int32, (c, c), 1)
    rel = (row - col).astype(F32)
    dmask = jnp.where(rel >= 0, jnp.exp(jnp.maximum(rel, 0.0) * log_gamma), 0.0)
    intra = _dot((_dot_nt(q.astype(BF16), k.astype(BF16)) * dmask).astype(BF16), v)
    idx = lax.broadcasted_iota(jnp.int32, (c, 1), 0).astype(F32)
    q_dec = q * jnp.exp((idx + 1.0) * log_gamma)
    k_dec = k * jnp.exp((c - 1.0 - idx) * log_gamma)
    state = s_ref[...]
    inter = _dot(q_dec.astype(BF16), state.astype(BF16))
    s_ref[...] = state * jnp.exp(float(c) * log_gamma) + _dot_tn(k_dec.astype(BF16), v)
    o = intra + inter
    mu = jnp.mean(o, axis=-1, keepdims=True)
    oc = o - mu
    o = oc * lax.rsqrt(jnp.mean(oc * oc, axis=-1, keepdims=True) + GN_EPS)
    o_ref[0] = (o * _silu(g_ref[0])).astype(o_ref.dtype)


def _retention_mixer(proj, *, q_col, k_col, v_col, g_col):
    b, t, _ = proj.shape
    c = _tile(t, RET_CHUNK, 16)
    hh = RET_HEADS
    wide = RET_DV // V7X_LANES
    assert v_col % wide == 0 and g_col % wide == 0
    nar = lambda off: pl.BlockSpec((1, c, V7X_LANES), lambda bi, h, ti, off=off: (bi, ti, off + h))
    wid = lambda off: pl.BlockSpec((1, c, RET_DV), lambda bi, h, ti, off=off: (bi, ti, off // wide + h))
    return pl.pallas_call(
        _retention_kernel,
        grid=(b, hh, t // c),
        in_specs=[nar(q_col), nar(k_col), wid(v_col), wid(g_col)],
        out_specs=pl.BlockSpec((1, c, RET_DV), lambda bi, h, ti: (bi, ti, h)),
        out_shape=jax.ShapeDtypeStruct((b, t, hh * RET_DV), BF16),
        scratch_shapes=[pltpu.VMEM((V7X_LANES, RET_DV), F32)],
        compiler_params=_cparams("parallel", "parallel", "arbitrary"),
        name="retention_mixer",
    )(proj, proj, proj, proj)


def _even_w_in_layout(w):
    d = w.shape[0]
    qkvz = 4 * DN_HEADS * V7X_LANES
    nba = 2 * DN_HEADS
    pad = jnp.zeros((d, 2 * V7X_LANES - nba), w.dtype)
    return jnp.concatenate([w[:, :qkvz], w[:, qkvz + nba:], w[:, qkvz:qkvz + nba], pad], axis=1)


def kernel(x, p, ln_mix_w, ln_mlp_w, ln_ple_w, w_up, w_down, w_ple_proj, w_ple_gate, ln_final_w,
           ev_w_in, ev_w_out, dn_conv_w, dn_a_log, dn_dt_bias, dn_norm_w,
           lru_conv_w, lru_conv_b, lru_wa, lru_ba, lru_wx, lru_bx, lru_lambda,
           od_w_in, od_w_out):
    b, t, d = x.shape
    depth = ln_mix_w.shape[0]
    m = b * t
    h = x.reshape(m, d)
    for i in range(depth):
        j = i // 2
        if i % 2 == 0:
            w_in = _even_w_in_layout(ev_w_in[j]).astype(BF16)
            proj = _norm_matmul(h, ln_mix_w[i], w_in, act=None, out_dtype=F32)
            proj = proj.reshape(b, t, -1)
            y_a = _delta_mixer(proj, dn_conv_w[j], dn_a_log[j], dn_dt_bias[j], dn_norm_w[j])
            y_b = _lru_mixer(proj, lru_conv_w[j], lru_conv_b[j], lru_wa[j], lru_ba[j], lru_wx[j], lru_bx[j],
                             lru_lambda[j], x_col=4 * DN_HEADS, y_col=4 * DN_HEADS + LRU_BLOCKS)
            w_out = ev_w_out[j].astype(BF16)
        else:
            proj = _norm_matmul(h, ln_mix_w[i], od_w_in[j].astype(BF16), act=None, out_dtype=F32)
            proj = proj.reshape(b, t, -1)
            y_a = _dilated_mixer(proj, q_col=0, k_col=SWA_HEADS, v_col=2 * SWA_HEADS)
            y_b = _retention_mixer(proj, q_col=3 * SWA_HEADS, k_col=3 * SWA_HEADS + RET_HEADS,
                                   v_col=3 * SWA_HEADS + 2 * RET_HEADS,
                                   g_col=3 * SWA_HEADS + 2 * RET_HEADS + RET_HEADS * RET_DV // V7X_LANES)
            w_out = od_w_out[j].astype(BF16)
        h = _matmul_residual([y_a.reshape(m, -1), y_b.reshape(m, -1)], w_out, h)
        up = _norm_matmul(h, ln_mlp_w[i], w_up[i].astype(BF16), act="relu2", out_dtype=BF16)
        h = _matmul_residual([up], w_down[i].astype(BF16), h)
        h = _ple(h, ln_ple_w[i], w_ple_gate[i].astype(BF16), p[i].reshape(m, -1), w_ple_proj[i].astype(BF16),
                 ln_final_w, final_norm=(i == depth - 1))
    return h.reshape(b, t, d)
```

```python
import functools

import jax
import jax.numpy as jnp
from jax import lax
from jax.experimental import pallas as pl
from jax.experimental.pallas import tpu as pltpu

F32 = jnp.float32
BF16 = jnp.bfloat16

V7X_LANES = 128
V7X_SUBLANES = 8
V7X_VMEM_LIMIT_BYTES = 56 * 1024 * 1024

NORM_EPS = 1e-6
GN_EPS = 1e-5
LRU_C = 8.0
DN_HEADS = 8
LRU_BLOCKS = 8
SWA_HEADS = 8
RET_HEADS = 4
RET_DV = 256
SWA_DILATIONS = (1, 4, 16)
SWA_SPAN = 128
SWA_BLOCK = 128
SWA_UNROLL = 8
DN_CHUNK = 128
DN_HEADS_PER_STEP = 2
RET_CHUNK = 256
MASK_VALUE = -1e30


def _cparams(*sem):
    return pltpu.CompilerParams(dimension_semantics=sem, vmem_limit_bytes=V7X_VMEM_LIMIT_BYTES)


def _dot(a, b):
    return jnp.dot(a, b, preferred_element_type=F32)


def _dot_nt(a, b):
    return lax.dot_general(a, b, (((1,), (1,)), ((), ())), preferred_element_type=F32)


def _dot_tn(a, b):
    return lax.dot_general(a, b, (((0,), (0,)), ((), ())), preferred_element_type=F32)


def _split_bf16(x):
    hi = x.astype(BF16)
    lo = (x - hi.astype(F32)).astype(BF16)
    return hi, lo


def _dot_f32(a, b):
    ah, al = _split_bf16(a)
    bh, bl = _split_bf16(b)
    n = b.shape[1]
    wide = _dot(ah, jnp.concatenate([bh, bl], axis=1))
    return wide[:, :n] + (wide[:, n:] + _dot(al, bh))


def _rms(x, w):
    return x * lax.rsqrt(jnp.mean(x * x, axis=-1, keepdims=True) + NORM_EPS) * w


def _silu(x):
    return x * jax.nn.sigmoid(x)


def _expm1(x):
    u = jnp.exp(x)
    lg = jnp.where(u == 1.0, 1.0, jnp.log(u))
    return jnp.where(u == 1.0, x, (u - 1.0) * x / lg)


def _tile(n, cap, quantum):
    if n <= cap:
        return n
    best = None
    for c in range(quantum, cap + 1, quantum):
        if n % c == 0:
            best = c
    assert best is not None, (n, cap, quantum)
    return best


def _norm_mm_kernel(x_ref, g_ref, w_ref, o_ref, hn_ref, *, act):
    @pl.when(pl.program_id(1) == 0)
    def _():
        hn_ref[...] = _rms(x_ref[...], g_ref[...]).astype(BF16)

    a = _dot(hn_ref[...], w_ref[...])
    if act == "relu2":
        a = jnp.square(jnp.maximum(a, 0.0))
    o_ref[...] = a.astype(o_ref.dtype)


def _norm_matmul(x, g, w, *, act, out_dtype, tm_cap=1024, tn_cap=1280):
    m, k = x.shape
    n = w.shape[1]
    tm = _tile(m, tm_cap, 256)
    tn = _tile(n, tn_cap, 256)
    return pl.pallas_call(
        functools.partial(_norm_mm_kernel, act=act),
        grid=(m // tm, n // tn),
        in_specs=[
            pl.BlockSpec((tm, k), lambda i, j: (i, 0)),
            pl.BlockSpec((1, k), lambda i, j: (0, 0)),
            pl.BlockSpec((k, tn), lambda i, j: (0, j)),
        ],
        out_specs=pl.BlockSpec((tm, tn), lambda i, j: (i, j)),
        out_shape=jax.ShapeDtypeStruct((m, n), out_dtype),
        scratch_shapes=[pltpu.VMEM((tm, k), BF16)],
        compiler_params=_cparams("parallel", "arbitrary"),
        name="norm_matmul",
    )(x, g.reshape(1, k), w)


def _mm_res_kernel(*refs, n_x):
    xs, ws = refs[:n_x], refs[n_x:2 * n_x]
    res_ref, o_ref = refs[2 * n_x], refs[2 * n_x + 1]
    acc = res_ref[...]
    for x_ref, w_ref in zip(xs, ws):
        acc = acc + _dot(x_ref[...], w_ref[...])
    o_ref[...] = acc


def _matmul_residual(xs, w, res, *, tm_cap=512, tn_cap=512):
    m, n = res.shape
    tm = _tile(m, tm_cap, 256)
    tn = _tile(n, tn_cap, 256)
    in_specs, row = [], 0
    for x in xs:
        in_specs.append(pl.BlockSpec((tm, x.shape[1]), lambda i, j: (i, 0)))
    w_specs = []
    for x in xs:
        kk = x.shape[1]
        assert row % kk == 0
        w_specs.append(pl.BlockSpec((kk, tn), functools.partial(lambda i, j, rb: (rb, j), rb=row // kk)))
        row += kk
    assert row == w.shape[0]
    return pl.pallas_call(
        functools.partial(_mm_res_kernel, n_x=len(xs)),
        grid=(m // tm, n // tn),
        in_specs=in_specs + w_specs + [pl.BlockSpec((tm, tn), lambda i, j: (i, j))],
        out_specs=pl.BlockSpec((tm, tn), lambda i, j: (i, j)),
        out_shape=jax.ShapeDtypeStruct((m, n), F32),
        compiler_params=_cparams("parallel", "parallel"),
        name="matmul_residual",
    )(*xs, *([w] * len(xs)), res)


def _ple_kernel(x_ref, g_ref, wg_ref, p_ref, wp_ref, gf_ref, o_ref, *, final_norm):
    d = x_ref.shape[1]
    hn = _rms(x_ref[...], g_ref[...]).astype(BF16)
    pb = p_ref[...].astype(BF16)
    nh = d // 2 if d % (2 * V7X_LANES) == 0 else d
    for c0 in range(0, d, nh):
        gate = jax.nn.sigmoid(_dot(hn, wg_ref[:, c0:c0 + nh]))
        pp = _dot(pb, wp_ref[:, c0:c0 + nh])
        o_ref[:, c0:c0 + nh] = x_ref[:, c0:c0 + nh] + gate * pp
    if final_norm:
        o_ref[...] = _rms(o_ref[...], gf_ref[...])


def _ple(x, g, wg, p, wp, gf, *, final_norm, tm_cap=512):
    m, d = x.shape
    pd = p.shape[1]
    tm = _tile(m, tm_cap, 256)
    return pl.pallas_call(
        functools.partial(_ple_kernel, final_norm=final_norm),
        grid=(m // tm,),
        in_specs=[
            pl.BlockSpec((tm, d), lambda i: (i, 0)),
            pl.BlockSpec((1, d), lambda i: (0, 0)),
            pl.BlockSpec((d, d), lambda i: (0, 0)),
            pl.BlockSpec((tm, pd), lambda i: (i, 0)),
            pl.BlockSpec((pd, d), lambda i: (0, 0)),
            pl.BlockSpec((1, d), lambda i: (0, 0)),
        ],
        out_specs=pl.BlockSpec((tm, d), lambda i: (i, 0)),
        out_shape=jax.ShapeDtypeStruct((m, d), F32),
        compiler_params=_cparams("parallel"),
        name="ple",
    )(x, g.reshape(1, d), wg, p, wp, gf.reshape(1, d))


def _causal_conv(x_ref, halo_ref, w_ref, first):
    x = x_ref[0]
    halo = jnp.where(first, 0.0, halo_ref[0])
    xp = jnp.concatenate([halo, x], axis=0)
    w = w_ref[...]
    y = pltpu.roll(xp, 3, axis=0)[V7X_SUBLANES:] * w[0:1]
    y = y + pltpu.roll(xp, 2, axis=0)[V7X_SUBLANES:] * w[1:2]
    y = y + pltpu.roll(xp, 1, axis=0)[V7X_SUBLANES:] * w[2:3]
    return y + x * w[3:4]


def _halo_map(col, rows_per_tile):
    nb = rows_per_tile // V7X_SUBLANES
    return lambda b, h, t: (b, jnp.maximum(t * nb - 1, 0), col(h))


def _unit_lower_inverses(ms, row, col):
    c = ms[0].shape[0]
    eye = (row == col).astype(F32)
    diag16 = (row >> 4) == (col >> 4)
    ns = [jnp.where(diag16, m, 0.0) for m in ms]
    invs = [eye - n for n in ns]
    qs = [_dot_f32(n, n) for n in ns]
    invs = [inv + _dot_f32(inv, q) for inv, q in zip(invs, qs)]
    qs = [_dot_f32(q, q) for q in qs]
    invs = [inv + _dot_f32(inv, q) for inv, q in zip(invs, qs)]
    qs = [_dot_f32(q, q) for q in qs]
    invs = [inv + _dot_f32(inv, q) for inv, q in zip(invs, qs)]
    shift = 4
    while (1 << shift) < c:
        off = ((row >> (shift + 1)) == (col >> (shift + 1))) & ((row >> shift) != (col >> shift))
        tmps = [_dot_f32(inv, jnp.where(off, m, 0.0)) for inv, m in zip(invs, ms)]
        invs = [inv - _dot_f32(tmp, inv) for inv, tmp in zip(invs, tmps)]
        shift += 1
    return invs


def _delta_kernel(q_ref, qh_ref, k_ref, kh_ref, v_ref, vh_ref, z_ref, ba_ref,
                  wq_ref, wk_ref, wv_ref, alog_ref, dtb_ref, nw_ref, o_ref, s_ref):
    hp = pl.program_id(1)
    first = pl.program_id(2) == 0
    tt = q_ref.shape[1]
    c = DN_CHUNK
    dh = V7X_LANES
    nheads = q_ref.shape[2] // dh

    @pl.when(first)
    def _():
        s_ref[...] = jnp.zeros_like(s_ref)

    def l2n(x):
        return x * lax.rsqrt(jnp.sum(x * x, axis=-1, keepdims=True) + 1e-6)

    q_all = _silu(_causal_conv(q_ref, qh_ref, wq_ref, first))
    k_all = _silu(_causal_conv(k_ref, kh_ref, wk_ref, first))
    v_all = _silu(_causal_conv(v_ref, vh_ref, wv_ref, first))
    ba = ba_ref[0]
    lane = lax.broadcasted_iota(jnp.int32, ba.shape, 1)
    sig_ba = jax.nn.sigmoid(ba)
    g_all = -jnp.exp(alog_ref[...]) * jax.nn.softplus(ba + dtb_ref[...])

    row = lax.broadcasted_iota(jnp.int32, (c, c), 0)
    col = lax.broadcasted_iota(jnp.int32, (c, c), 1)
    causal = row >= col
    strict = row > col
    ltri = causal.astype(BF16)

    qs, ks, vs, bs, gs = [], [], [], [], []
    for hi in range(nheads):
        hs = slice(hi * dh, (hi + 1) * dh)
        head = nheads * hp + hi
        qh = l2n(q_all[:, hs]) * (dh ** -0.5)
        kh = l2n(k_all[:, hs])
        vh = v_all[:, hs]
        beta = jnp.sum(jnp.where(lane == head, sig_ba, 0.0), axis=-1, keepdims=True)
        g = jnp.sum(jnp.where(lane == head + DN_HEADS, g_all, 0.0), axis=-1, keepdims=True)
        for ci in range(tt // c):
            sl = slice(ci * c, (ci + 1) * c)
            qs.append(qh[sl]); ks.append(kh[sl]); vs.append(vh[sl]); bs.append(beta[sl]); gs.append(g[sl])

    def cumsum_rows(gcol):
        gb = jnp.broadcast_to(gcol, (c, c))
        g_hi, g_lo = _split_bf16(gb)
        g_lo2 = (gb - g_hi.astype(F32) - g_lo.astype(F32)).astype(BF16)
        return _dot(ltri, g_hi) + (_dot(ltri, g_lo) + _dot(ltri, g_lo2))

    gcs = [cumsum_rows(g) for g in gs]
    decays = [jnp.where(causal, jnp.exp(jnp.where(causal, gc - gc.T, 0.0)), 0.0) for gc in gcs]
    kbs = [kc * bc for kc, bc in zip(ks, bs)]
    k16 = [kc.astype(BF16) for kc in ks]
    ms = [jnp.where(strict, _dot_nt(kb.astype(BF16), kc) * dec, 0.0) for kb, kc, dec in zip(kbs, k16, decays)]
    invs = _unit_lower_inverses(ms, row, col)
    egcs = [jnp.exp(gc) for gc in gcs]
    us = [_dot_f32(inv, vc * bc) for inv, vc, bc in zip(invs, vs, bs)]
    ws = [_dot_f32(inv, kb * egc).astype(BF16) for inv, kb, egc in zip(invs, kbs, egcs)]
    qks = [(_dot_nt(qc.astype(BF16), kc) * dec).astype(BF16) for qc, kc, dec in zip(qs, k16, decays)]
    q_decs = [(qc * egc).astype(BF16) for qc, egc in zip(qs, egcs)]
    lasts = [gc[c - 1:c, :] for gc in gcs]
    k_decs = [(kc * jnp.exp(last - gc)).astype(BF16) for kc, last, gc in zip(ks, lasts, gcs)]
    g_tots = [jnp.exp(last) for last in lasts]

    nch = tt // c
    states = [s_ref[hi] for hi in range(nheads)]
    for ci in range(nch):
        sl = slice(ci * c, (ci + 1) * c)
        for hi in range(nheads):
            i = hi * nch + ci
            hs = slice(hi * dh, (hi + 1) * dh)
            sb = states[hi].astype(BF16)
            v_new = (us[i] - _dot(ws[i], sb)).astype(BF16)
            o = _dot(q_decs[i], sb) + _dot(qks[i], v_new)
            states[hi] = states[hi] * g_tots[i] + _dot_tn(k_decs[i], v_new)
            o_ref[0, sl, hs] = (_rms(o, nw_ref[...]) * _silu(z_ref[0, sl, hs])).astype(o_ref.dtype)
    for hi in range(nheads):
        s_ref[hi] = states[hi]


def _delta_mixer(proj, conv_w, a_log, dt_bias, norm_w, *, tt_cap=512):
    b, t, _ = proj.shape
    tt = _tile(t, tt_cap, DN_CHUNK)
    hh = DN_HEADS
    hps = DN_HEADS_PER_STEP
    wd = hps * V7X_LANES
    npair = hh // hps
    blk = lambda off: pl.BlockSpec((1, tt, wd), lambda bi, h, ti, off=off: (bi, ti, off + h))
    halo = lambda off: pl.BlockSpec((1, V7X_SUBLANES, wd), _halo_map(lambda h, off=off: off + h, tt))
    cw = lambda off: pl.BlockSpec((4, wd), lambda bi, h, ti, off=off: (0, off + h))
    row_spec = pl.BlockSpec((1, V7X_LANES), lambda bi, h, ti: (0, 0))
    pad = jnp.zeros((V7X_LANES - 2 * hh,), F32)
    alog_row = jnp.concatenate([jnp.zeros((hh,), F32), a_log, pad]).reshape(1, V7X_LANES)
    dtb_row = jnp.concatenate([jnp.zeros((hh,), F32), dt_bias, pad]).reshape(1, V7X_LANES)
    ba_col = 6 * hh
    return pl.pallas_call(
        _delta_kernel,
        grid=(b, npair, t // tt),
        in_specs=[
            blk(0), halo(0), blk(npair), halo(npair), blk(2 * npair), halo(2 * npair), blk(3 * npair),
            pl.BlockSpec((1, tt, V7X_LANES), lambda bi, h, ti: (bi, ti, ba_col)),
            cw(0), cw(npair), cw(2 * npair), row_spec, row_spec, row_spec,
        ],
        out_specs=pl.BlockSpec((1, tt, wd), lambda bi, h, ti: (bi, ti, h)),
        out_shape=jax.ShapeDtypeStruct((b, t, hh * V7X_LANES), BF16),
        scratch_shapes=[pltpu.VMEM((hps, V7X_LANES, V7X_LANES), F32)],
        compiler_params=_cparams("parallel", "parallel", "arbitrary"),
        name="delta_mixer",
    )(proj, proj, proj, proj, proj, proj, proj, proj, conv_w, conv_w, conv_w,
      alog_row, dtb_row, norm_w.reshape(1, V7X_LANES))


def _lru_kernel(x_ref, xh_ref, y_ref, cw_ref, cb_ref, wa_ref, ba_ref, wx_ref, bx_ref, lam_ref,
                o_ref, h_ref):
    first = pl.program_id(2) == 0
    tt = x_ref.shape[1]

    @pl.when(first)
    def _():
        h_ref[...] = jnp.zeros_like(h_ref)

    xc = _causal_conv(x_ref, xh_ref, cw_ref, first) + cb_ref[...]
    xb = xc.astype(BF16)
    r = jax.nn.sigmoid(_dot(xb, wa_ref[0]) + ba_ref[...])
    i = jax.nn.sigmoid(_dot(xb, wx_ref[0]) + bx_ref[...])
    log_a = -LRU_C * r * jax.nn.softplus(-lam_ref[...])
    a = jnp.exp(log_a)
    u = jnp.sqrt(-_expm1(2.0 * log_a)) * (i * xc)

    rowi = lax.broadcasted_iota(jnp.int32, (tt, V7X_LANES), 0)
    s = 1
    while s < tt:
        if s < V7X_SUBLANES:
            a_sh = jnp.where(rowi >= s, pltpu.roll(a, s, axis=0), 1.0)
            u_sh = jnp.where(rowi >= s, pltpu.roll(u, s, axis=0), 0.0)
        else:
            a_sh = jnp.concatenate([jnp.ones((s, V7X_LANES), F32), a[:tt - s]], axis=0)
            u_sh = jnp.concatenate([jnp.zeros((s, V7X_LANES), F32), u[:tt - s]], axis=0)
        u = a * u_sh + u
        a = a * a_sh
        s *= 2
    hs = u + a * h_ref[...]
    h_ref[...] = hs[tt - 1:tt]
    o_ref[0] = (hs * jax.nn.gelu(y_ref[0])).astype(o_ref.dtype)


def _lru_mixer(proj, conv_w, conv_b, wa, ba, wx, bx, lam, *, x_col, y_col, tt_cap=512):
    b, t, _ = proj.shape
    tt = _tile(t, tt_cap, 16)
    gg = LRU_BLOCKS
    vec = lambda a: a.reshape(1, gg * V7X_LANES)
    vspec = pl.BlockSpec((1, V7X_LANES), lambda bi, g, ti: (0, g))
    wspec = pl.BlockSpec((1, V7X_LANES, V7X_LANES), lambda bi, g, ti: (g, 0, 0))
    return pl.pallas_call(
        _lru_kernel,
        grid=(b, gg, t // tt),
        in_specs=[
            pl.BlockSpec((1, tt, V7X_LANES), lambda bi, g, ti: (bi, ti, x_col + g)),
            pl.BlockSpec((1, V7X_SUBLANES, V7X_LANES), _halo_map(lambda g: x_col + g, tt)),
            pl.BlockSpec((1, tt, V7X_LANES), lambda bi, g, ti: (bi, ti, y_col + g)),
            pl.BlockSpec((4, V7X_LANES), lambda bi, g, ti: (0, g)),
            vspec, wspec, vspec, wspec, vspec, vspec,
        ],
        out_specs=pl.BlockSpec((1, tt, V7X_LANES), lambda bi, g, ti: (bi, ti, g)),
        out_shape=jax.ShapeDtypeStruct((b, t, gg * V7X_LANES), BF16),
        scratch_shapes=[pltpu.VMEM((1, V7X_LANES), F32)],
        compiler_params=_cparams("parallel", "parallel", "arbitrary"),
        name="lru_mixer",
    )(proj, proj, proj, conv_w, vec(conv_b), wa.astype(BF16), vec(ba), wx.astype(BF16), vec(bx), vec(lam))


def _dilated_kernel(q_ref, kc_ref, kp_ref, vc_ref, vp_ref, o_ref, kk_ref, vv_ref, m_ref, l_ref, acc_ref):
    h = pl.program_id(1)
    has_prev = pl.program_id(2) > 0
    tq = q_ref.shape[1]
    bq = SWA_BLOCK
    kk_ref[0:tq, :] = kp_ref[0]
    kk_ref[tq:2 * tq, :] = kc_ref[0]
    vv_ref[0:tq, :] = vp_ref[0]
    vv_ref[tq:2 * tq, :] = vc_ref[0]

    slope = jnp.exp2(-(jnp.full((1, 1), h, jnp.int32).astype(F32) + 1.0) * (8.0 / SWA_HEADS))
    iq = lax.broadcasted_iota(jnp.int32, (bq, 2 * bq), 0)
    ik = lax.broadcasted_iota(jnp.int32, (bq, 2 * bq), 1)
    rel = bq + iq - ik
    in_window = (rel >= 0) & (rel <= SWA_SPAN)
    relf = rel.astype(F32)
    scale = V7X_LANES ** -0.5

    for bi, d in enumerate(SWA_DILATIONS):
        nblk = tq // (bq * d)
        assert nblk >= 1 and nblk & (nblk - 1) == 0
        bias = (slope * float(d)) * relf

        def rows(ref, base, d=d):
            if d == 1:
                return ref[pl.ds(base, bq), :]
            return ref[pl.ds(base, bq, stride=d), :]

        def body(idx, carry, d=d, nblk=nblk, bias=bias, bi=bi):
            r = idx >> (nblk.bit_length() - 1)
            n = idx & (nblk - 1)
            base = n * (bq * d) + r
            qb = (rows(q_ref.at[0], base) * scale).astype(BF16)
            kcat = jnp.concatenate([rows(kk_ref, tq + base - bq * d), rows(kk_ref, tq + base)], axis=0)
            vcat = jnp.concatenate([rows(vv_ref, tq + base - bq * d), rows(vv_ref, tq + base)], axis=0)
            s = _dot_nt(qb, kcat.astype(BF16)) - bias
            valid = in_window & ((ik >= bq) | (n > 0) | has_prev)
            s = jnp.where(valid, s, MASK_VALUE)
            m_b = jnp.max(s, axis=-1, keepdims=True)
            p = jnp.exp(s - m_b)
            l_b = jnp.sum(p, axis=-1, keepdims=True)
            acc_b = _dot(p.astype(BF16), vcat.astype(BF16))
            m_b = jnp.broadcast_to(m_b, (bq, V7X_LANES))
            l_b = jnp.broadcast_to(l_b, (bq, V7X_LANES))
            if bi > 0:
                m_o, l_o, acc_o = rows(m_ref, base), rows(l_ref, base), rows(acc_ref, base)
                m_n = jnp.maximum(m_o, m_b)
                alpha = jnp.exp(m_o - m_n)
                beta = jnp.exp(m_b - m_n)
                m_b = m_n
                l_b = alpha * l_o + beta * l_b
                acc_b = alpha * acc_o + beta * acc_b
            if d == 1:
                m_ref[pl.ds(base, bq), :] = m_b
                l_ref[pl.ds(base, bq), :] = l_b
                acc_ref[pl.ds(base, bq), :] = acc_b
            else:
                m_ref[pl.ds(base, bq, stride=d), :] = m_b
                l_ref[pl.ds(base, bq, stride=d), :] = l_b
                acc_ref[pl.ds(base, bq, stride=d), :] = acc_b
            return carry

        lax.fori_loop(0, d * nblk, body, 0, unroll=SWA_UNROLL)

    o_ref[0] = (acc_ref[...] / l_ref[...]).astype(o_ref.dtype)


def _dilated_mixer(proj, *, q_col, k_col, v_col, tq=2048):
    b, t, _ = proj.shape
    assert t % tq == 0 and tq % (SWA_BLOCK * max(SWA_DILATIONS)) == 0
    hh = SWA_HEADS
    cur = lambda off: pl.BlockSpec((1, tq, V7X_LANES), lambda bi, h, ti, off=off: (bi, ti, off + h))
    prev = lambda off: pl.BlockSpec((1, tq, V7X_LANES),
                                    lambda bi, h, ti, off=off: (bi, jnp.maximum(ti - 1, 0), off + h))
    return pl.pallas_call(
        _dilated_kernel,
        grid=(b, hh, t // tq),
        in_specs=[cur(q_col), cur(k_col), prev(k_col), cur(v_col), prev(v_col)],
        out_specs=pl.BlockSpec((1, tq, V7X_LANES), lambda bi, h, ti: (bi, ti, h)),
        out_shape=jax.ShapeDtypeStruct((b, t, hh * V7X_LANES), BF16),
        scratch_shapes=[pltpu.VMEM((2 * tq, V7X_LANES), F32), pltpu.VMEM((2 * tq, V7X_LANES), F32),
                        pltpu.VMEM((tq, V7X_LANES), F32), pltpu.VMEM((tq, V7X_LANES), F32),
                        pltpu.VMEM((tq, V7X_LANES), F32)],
        compiler_params=_cparams("parallel", "parallel", "parallel"),
        name="dilated_mixer",
    )(proj, proj, proj, proj, proj)


def _retention_kernel(q_ref, k_ref, v_ref, g_ref, o_ref, s_ref):
    h = pl.program_id(1)
    c = q_ref.shape[1]

    @pl.when(pl.program_id(2) == 0)
    def _():
        s_ref[...] = jnp.zeros_like(s_ref)

    hf = jnp.full((1, 1), h, jnp.int32).astype(F32)
    log_gamma = jnp.log1p(-jnp.exp2(-5.0 - hf))
    q = q_ref[0]
    k = k_ref[0] * (V7X_LANES ** -0.5)
    v = v_ref[0].astype(BF16)
    row = lax.broadcasted_iota(jnp.int32, (c, c), 0)
    col = lax.broadcasted_iota(jnp.int32, (c, c), 1)
    rel = (row - col).astype(F32)
    dmask = jnp.where(rel >= 0, jnp.exp(jnp.maximum(rel, 0.0) * log_gamma), 0.0)
    intra = _dot((_dot_nt(q.astype(BF16), k.astype(BF16)) * dmask).astype(BF16), v)
    idx = lax.broadcasted_iota(jnp.int32, (c, 1), 0).astype(F32)
    q_dec = q * jnp.exp((idx + 1.0) * log_gamma)
    k_dec = k * jnp.exp((c - 1.0 - idx) * log_gamma)
    state = s_ref[...]
    inter = _dot(q_dec.astype(BF16), state.astype(BF16))
    s_ref[...] = state * jnp.exp(float(c) * log_gamma) + _dot_tn(k_dec.astype(BF16), v)
    o = intra + inter
    mu = jnp.mean(o, axis=-1, keepdims=True)
    oc = o - mu
    o = oc * lax.rsqrt(jnp.mean(oc * oc, axis=-1, keepdims=True) + GN_EPS)
    o_ref[0] = (o * _silu(g_ref[0])).astype(o_ref.dtype)


def _retention_mixer(proj, *, q_col, k_col, v_col, g_col):
    b, t, _ = proj.shape
    c = _tile(t, RET_CHUNK, 16)
    hh = RET_HEADS
    wide = RET_DV // V7X_LANES
    assert v_col % wide == 0 and g_col % wide == 0
    nar = lambda off: pl.BlockSpec((1, c, V7X_LANES), lambda bi, h, ti, off=off: (bi, ti, off + h))
    wid = lambda off: pl.BlockSpec((1, c, RET_DV), lambda bi, h, ti, off=off: (bi, ti, off // wide + h))
    return pl.pallas_call(
        _retention_kernel,
        grid=(b, hh, t // c),
        in_specs=[nar(q_col), nar(k_col), wid(v_col), wid(g_col)],
        out_specs=pl.BlockSpec((1, c, RET_DV), lambda bi, h, ti: (bi, ti, h)),
        out_shape=jax.ShapeDtypeStruct((b, t, hh * RET_DV), BF16),
        scratch_shapes=[pltpu.VMEM((V7X_LANES, RET_DV), F32)],
        compiler_params=_cparams("parallel", "parallel", "arbitrary"),
        name="retention_mixer",
    )(proj, proj, proj, proj)


def _even_w_in_layout(w):
    d = w.shape[0]
    qkvz = 4 * DN_HEADS * V7X_LANES
    nba = 2 * DN_HEADS
    pad = jnp.zeros((d, 2 * V7X_LANES - nba), w.dtype)
    return jnp.concatenate([w[:, :qkvz], w[:, qkvz + nba:], w[:, qkvz:qkvz + nba], pad], axis=1)


def kernel(x, p, ln_mix_w, ln_mlp_w, ln_ple_w, w_up, w_down, w_ple_proj, w_ple_gate, ln_final_w,
           ev_w_in, ev_w_out, dn_conv_w, dn_a_log, dn_dt_bias, dn_norm_w,
           lru_conv_w, lru_conv_b, lru_wa, lru_ba, lru_wx, lru_bx, lru_lambda,
           od_w_in, od_w_out):
    b, t, d = x.shape
    depth = ln_mix_w.shape[0]
    m = b * t
    h = x.reshape(m, d)
    for i in range(depth):
        j = i // 2
        if i % 2 == 0:
            w_in = _even_w_in_layout(ev_w_in[j]).astype(BF16)
            proj = _norm_matmul(h, ln_mix_w[i], w_in, act=None, out_dtype=F32)
            proj = proj.reshape(b, t, -1)
            y_a = _delta_mixer(proj, dn_conv_w[j], dn_a_log[j], dn_dt_bias[j], dn_norm_w[j])
            y_b = _lru_mixer(proj, lru_conv_w[j], lru_conv_b[j], lru_wa[j], lru_ba[j], lru_wx[j], lru_bx[j],
                             lru_lambda[j], x_col=4 * DN_HEADS, y_col=4 * DN_HEADS + LRU_BLOCKS)
            w_out = ev_w_out[j].astype(BF16)
        else:
            proj = _norm_matmul(h, ln_mix_w[i], od_w_in[j].astype(BF16), act=None, out_dtype=F32)
            proj = proj.reshape(b, t, -1)
            y_a = _dilated_mixer(proj, q_col=0, k_col=SWA_HEADS, v_col=2 * SWA_HEADS)
            y_b = _retention_mixer(proj, q_col=3 * SWA_HEADS, k_col=3 * SWA_HEADS + RET_HEADS,
                                   v_col=3 * SWA_HEADS + 2 * RET_HEADS,
                                   g_col=3 * SWA_HEADS + 2 * RET_HEADS + RET_HEADS * RET_DV // V7X_LANES)
            w_out = od_w_out[j].astype(BF16)
        h = _matmul_residual([y_a.reshape(m, -1), y_b.reshape(m, -1)], w_out, h)
        up = _norm_matmul(h, ln_mlp_w[i], w_up[i].astype(BF16), act="relu2", out_dtype=BF16)
        h = _matmul_residual([up], w_down[i].astype(BF16), h)
        h = _ple(h, ln_ple_w[i], w_ple_gate[i].astype(BF16), p[i].reshape(m, -1), w_ple_proj[i].astype(BF16),
                 ln_final_w, final_norm=(i == depth - 1))
    return h.reshape(b, t, d)
```

```python
import functools

import jax
import jax.numpy as jnp
from jax import lax
from jax.experimental import pallas as pl
from jax.experimental.pallas import tpu as pltpu

F32 = jnp.float32
BF16 = jnp.bfloat16

V7X_LANES = 128
V7X_SUBLANES = 8
V7X_VMEM_LIMIT_BYTES = 56 * 1024 * 1024

NORM_EPS = 1e-6
GN_EPS = 1e-5
LRU_C = 8.0
DN_HEADS = 8
LRU_BLOCKS = 8
SWA_HEADS = 8
RET_HEADS = 4
RET_DV = 256
SWA_DILATIONS = (1, 4, 16)
SWA_SPAN = 128
SWA_BLOCK = 128
SWA_UNROLL = 8
DN_CHUNK = 128
DN_HEADS_PER_STEP = 4
RET_CHUNK = 256
MASK_VALUE = -1e30


def _cparams(*sem):
    return pltpu.CompilerParams(dimension_semantics=sem, vmem_limit_bytes=V7X_VMEM_LIMIT_BYTES)


def _dot(a, b):
    return jnp.dot(a, b, preferred_element_type=F32)


def _dot_nt(a, b):
    return lax.dot_general(a, b, (((1,), (1,)), ((), ())), preferred_element_type=F32)


def _dot_tn(a, b):
    return lax.dot_general(a, b, (((0,), (0,)), ((), ())), preferred_element_type=F32)


def _split_bf16(x):
    hi = x.astype(BF16)
    lo = (x - hi.astype(F32)).astype(BF16)
    return hi, lo


def _dot_f32(a, b):
    ah, al = a if isinstance(a, tuple) else _split_bf16(a)
    bh, bl = b if isinstance(b, tuple) else _split_bf16(b)
    n = bh.shape[1]
    wide = _dot(ah, jnp.concatenate([bh, bl], axis=1))
    return wide[:, :n] + (wide[:, n:] + _dot(al, bh))


def _rms(x, w):
    return x * lax.rsqrt(jnp.mean(x * x, axis=-1, keepdims=True) + NORM_EPS) * w


def _silu(x):
    return x * jax.nn.sigmoid(x)


def _expm1(x):
    u = jnp.exp(x)
    lg = jnp.where(u == 1.0, 1.0, jnp.log(u))
    return jnp.where(u == 1.0, x, (u - 1.0) * x / lg)


def _tile(n, cap, quantum):
    if n <= cap:
        return n
    best = None
    for c in range(quantum, cap + 1, quantum):
        if n % c == 0:
            best = c
    assert best is not None, (n, cap, quantum)
    return best


CAST_BLOCK_BYTES = 8 * 1024 * 1024


def _cast_kernel(x_ref, o_ref):
    o_ref[...] = x_ref[...].astype(o_ref.dtype)


def _to_bf16(w):
    l, r, c = w.shape
    tr = _tile(r, max(16, CAST_BLOCK_BYTES // (4 * c) // 16 * 16), 16)
    return pl.pallas_call(
        _cast_kernel,
        grid=(l, r // tr),
        in_specs=[pl.BlockSpec((1, tr, c), lambda i, j: (i, j, 0))],
        out_specs=pl.BlockSpec((1, tr, c), lambda i, j: (i, j, 0)),
        out_shape=jax.ShapeDtypeStruct(w.shape, BF16),
        compiler_params=_cparams("parallel", "parallel"),
        name="to_bf16",
    )(w)


def _norm_mm_kernel(x_ref, g_ref, w_ref, o_ref, hn_ref, *, act):
    @pl.when(pl.program_id(1) == 0)
    def _():
        hn_ref[...] = _rms(x_ref[...], g_ref[...]).astype(BF16)

    a = _dot(hn_ref[...], w_ref[...])
    if act == "relu2":
        a = jnp.square(jnp.maximum(a, 0.0))
    o_ref[...] = a.astype(o_ref.dtype)


def _norm_matmul(x, g, w, *, act, out_dtype, tm_cap=1024, tn_cap=1280):
    m, k = x.shape
    n = w.shape[1]
    tm = _tile(m, tm_cap, 256)
    tn = _tile(n, tn_cap, 256)
    return pl.pallas_call(
        functools.partial(_norm_mm_kernel, act=act),
        grid=(m // tm, n // tn),
        in_specs=[
            pl.BlockSpec((tm, k), lambda i, j: (i, 0)),
            pl.BlockSpec((1, k), lambda i, j: (0, 0)),
            pl.BlockSpec((k, tn), lambda i, j: (0, j)),
        ],
        out_specs=pl.BlockSpec((tm, tn), lambda i, j: (i, j)),
        out_shape=jax.ShapeDtypeStruct((m, n), out_dtype),
        scratch_shapes=[pltpu.VMEM((tm, k), BF16)],
        compiler_params=_cparams("parallel", "arbitrary"),
        name="norm_matmul",
    )(x, g.reshape(1, k), w)


def _mm_res_kernel(*refs, n_x):
    xs, ws = refs[:n_x], refs[n_x:2 * n_x]
    res_ref, o_ref = refs[2 * n_x], refs[2 * n_x + 1]
    acc = res_ref[...]
    for x_ref, w_ref in zip(xs, ws):
        acc = acc + _dot(x_ref[...], w_ref[...])
    o_ref[...] = acc


def _matmul_residual(xs, w, res, *, tm_cap=512, tn_cap=512):
    m, n = res.shape
    tm = _tile(m, tm_cap, 256)
    tn = _tile(n, tn_cap, 256)
    in_specs, row = [], 0
    for x in xs:
        in_specs.append(pl.BlockSpec((tm, x.shape[1]), lambda i, j: (i, 0)))
    w_specs = []
    for x in xs:
        kk = x.shape[1]
        assert row % kk == 0
        w_specs.append(pl.BlockSpec((kk, tn), functools.partial(lambda i, j, rb: (rb, j), rb=row // kk)))
        row += kk
    assert row == w.shape[0]
    return pl.pallas_call(
        functools.partial(_mm_res_kernel, n_x=len(xs)),
        grid=(m // tm, n // tn),
        in_specs=in_specs + w_specs + [pl.BlockSpec((tm, tn), lambda i, j: (i, j))],
        out_specs=pl.BlockSpec((tm, tn), lambda i, j: (i, j)),
        out_shape=jax.ShapeDtypeStruct((m, n), F32),
        compiler_params=_cparams("parallel", "parallel"),
        name="matmul_residual",
    )(*xs, *([w] * len(xs)), res)


def _ple_kernel(x_ref, g_ref, wg_ref, p_ref, wp_ref, gf_ref, o_ref, *, final_norm):
    d = x_ref.shape[1]
    hn = _rms(x_ref[...], g_ref[...]).astype(BF16)
    pb = p_ref[...].astype(BF16)
    nh = d // 2 if d % (2 * V7X_LANES) == 0 else d
    for c0 in range(0, d, nh):
        gate = jax.nn.sigmoid(_dot(hn, wg_ref[:, c0:c0 + nh]))
        pp = _dot(pb, wp_ref[:, c0:c0 + nh])
        o_ref[:, c0:c0 + nh] = x_ref[:, c0:c0 + nh] + gate * pp
    if final_norm:
        o_ref[...] = _rms(o_ref[...], gf_ref[...])


def _ple(x, g, wg, p, wp, gf, *, final_norm, tm_cap=512):
    m, d = x.shape
    pd = p.shape[1]
    tm = _tile(m, tm_cap, 256)
    return pl.pallas_call(
        functools.partial(_ple_kernel, final_norm=final_norm),
        grid=(m // tm,),
        in_specs=[
            pl.BlockSpec((tm, d), lambda i: (i, 0)),
            pl.BlockSpec((1, d), lambda i: (0, 0)),
            pl.BlockSpec((d, d), lambda i: (0, 0)),
            pl.BlockSpec((tm, pd), lambda i: (i, 0)),
            pl.BlockSpec((pd, d), lambda i: (0, 0)),
            pl.BlockSpec((1, d), lambda i: (0, 0)),
        ],
        out_specs=pl.BlockSpec((tm, d), lambda i: (i, 0)),
        out_shape=jax.ShapeDtypeStruct((m, d), F32),
        compiler_params=_cparams("parallel"),
        name="ple",
    )(x, g.reshape(1, d), wg, p, wp, gf.reshape(1, d))


def _causal_conv(x_ref, halo_ref, w_ref, first):
    x = x_ref[0]
    halo = jnp.where(first, 0.0, halo_ref[0])
    xp = jnp.concatenate([halo, x], axis=0)
    w = w_ref[...]
    y = pltpu.roll(xp, 3, axis=0)[V7X_SUBLANES:] * w[0:1]
    y = y + pltpu.roll(xp, 2, axis=0)[V7X_SUBLANES:] * w[1:2]
    y = y + pltpu.roll(xp, 1, axis=0)[V7X_SUBLANES:] * w[2:3]
    return y + x * w[3:4]


def _halo_map(col, rows_per_tile):
    nb = rows_per_tile // V7X_SUBLANES
    return lambda b, h, t: (b, jnp.maximum(t * nb - 1, 0), col(h))


def _unit_lower_inverses(ms, row, col):
    c = ms[0].shape[0]
    eye = (row == col).astype(F32)
    diag16 = (row >> 4) == (col >> 4)
    b16 = lambda xs: [x.astype(BF16) for x in xs]
    ns = [jnp.where(diag16, m, 0.0) for m in ms]
    invs = [eye - n for n in ns]
    qs = b16(ns)
    for step in range(3):
        qs = b16([_dot(q, q) for q in qs])
        invs = [inv + _dot(inv.astype(BF16), q) for inv, q in zip(invs, qs)]
    shift = 4
    while (1 << shift) < c:
        off = ((row >> (shift + 1)) == (col >> (shift + 1))) & ((row >> shift) != (col >> shift))
        inv16 = b16(invs)
        tmps = [_dot(inv, jnp.where(off, m, 0.0).astype(BF16)) for inv, m in zip(inv16, ms)]
        invs = [inv - _dot(tmp.astype(BF16), i16) for inv, tmp, i16 in zip(invs, tmps, inv16)]
        shift += 1
    xs = [_split_bf16(inv) for inv in invs]
    res = [eye - inv - _dot_f32(m, x) for inv, m, x in zip(invs, ms, xs)]
    return [inv + _dot_f32(x, r) for inv, x, r in zip(invs, xs, res)]


def _delta_kernel(q_ref, qh_ref, k_ref, kh_ref, v_ref, vh_ref, z_ref, ba_ref,
                  wq_ref, wk_ref, wv_ref, alog_ref, dtb_ref, nw_ref, o_ref, s_ref):
    hp = pl.program_id(1)
    first = pl.program_id(2) == 0
    tt = q_ref.shape[1]
    c = DN_CHUNK
    dh = V7X_LANES
    nheads = q_ref.shape[2] // dh

    @pl.when(first)
    def _():
        s_ref[...] = jnp.zeros_like(s_ref)

    def l2n(x):
        return x * lax.rsqrt(jnp.sum(x * x, axis=-1, keepdims=True) + 1e-6)

    q_all = _silu(_causal_conv(q_ref, qh_ref, wq_ref, first))
    k_all = _silu(_causal_conv(k_ref, kh_ref, wk_ref, first))
    v_all = _silu(_causal_conv(v_ref, vh_ref, wv_ref, first))
    ba = ba_ref[0]
    lane = lax.broadcasted_iota(jnp.int32, ba.shape, 1)
    sig_ba = jax.nn.sigmoid(ba)
    g_all = -jnp.exp(alog_ref[...]) * jax.nn.softplus(ba + dtb_ref[...])

    row = lax.broadcasted_iota(jnp.int32, (c, c), 0)
    col = lax.broadcasted_iota(jnp.int32, (c, c), 1)
    causal = row >= col
    strict = row > col
    ltri = causal.astype(BF16)

    nch = tt // c

    def cumsum_block(gblk):
        g_hi, g_lo = _split_bf16(gblk)
        g_lo2 = (gblk - g_hi.astype(F32) - g_lo.astype(F32)).astype(BF16)
        return _dot(ltri, g_hi) + (_dot(ltri, g_lo) + _dot(ltri, g_lo2))

    gc_blocks = [cumsum_block(g_all[ci * c:(ci + 1) * c]) for ci in range(nch)]
    gc_blocks_t = [gcb.T for gcb in gc_blocks]

    qs, ks, vs, bs, gcs, gc_rows = [], [], [], [], [], []
    for hi in range(nheads):
        hs = slice(hi * dh, (hi + 1) * dh)
        head = nheads * hp + hi
        qh = l2n(q_all[:, hs]) * (dh ** -0.5)
        kh = l2n(k_all[:, hs])
        vh = v_all[:, hs]
        beta = jnp.sum(jnp.where(lane == head, sig_ba, 0.0), axis=-1, keepdims=True)
        for ci in range(nch):
            sl = slice(ci * c, (ci + 1) * c)
            qs.append(qh[sl]); ks.append(kh[sl]); vs.append(vh[sl]); bs.append(beta[sl])
            gcol = jnp.sum(jnp.where(col == head + DN_HEADS, gc_blocks[ci], 0.0), axis=-1, keepdims=True)
            grow = jnp.sum(jnp.where(row == head + DN_HEADS, gc_blocks_t[ci], 0.0), axis=0, keepdims=True)
            gcs.append(jnp.broadcast_to(gcol, (c, c)))
            gc_rows.append(jnp.broadcast_to(grow, (c, c)))

    decays = [jnp.where(causal, jnp.exp(jnp.where(causal, gc - gr, 0.0)), 0.0) for gc, gr in zip(gcs, gc_rows)]
    kbs = [kc * bc for kc, bc in zip(ks, bs)]
    k16 = [kc.astype(BF16) for kc in ks]
    ms = [jnp.where(strict, _dot_nt(kb.astype(BF16), kc) * dec, 0.0) for kb, kc, dec in zip(kbs, k16, decays)]
    invs = _unit_lower_inverses(ms, row, col)
    egcs = [jnp.exp(gc) for gc in gcs]
    sols = [_dot_f32(inv, jnp.concatenate([vc * bc, kb * egc], axis=1))
            for inv, vc, bc, kb, egc in zip(invs, vs, bs, kbs, egcs)]
    us = [sol[:, :dh] for sol in sols]
    ws = [sol[:, dh:].astype(BF16) for sol in sols]
    qks = [(_dot_nt(qc.astype(BF16), kc) * dec).astype(BF16) for qc, kc, dec in zip(qs, k16, decays)]
    q_decs = [(qc * egc).astype(BF16) for qc, egc in zip(qs, egcs)]
    lasts = [gc[c - 1:c, :] for gc in gcs]
    k_decs = [(kc * jnp.exp(last - gc)).astype(BF16) for kc, last, gc in zip(ks, lasts, gcs)]
    g_tots = [jnp.exp(last) for last in lasts]

    nch = tt // c
    states = [s_ref[hi] for hi in range(nheads)]
    for ci in range(nch):
        sl = slice(ci * c, (ci + 1) * c)
        for hi in range(nheads):
            i = hi * nch + ci
            hs = slice(hi * dh, (hi + 1) * dh)
            sb = states[hi].astype(BF16)
            v_new = (us[i] - _dot(ws[i], sb)).astype(BF16)
            o = _dot(q_decs[i], sb) + _dot(qks[i], v_new)
            states[hi] = states[hi] * g_tots[i] + _dot_tn(k_decs[i], v_new)
            o_ref[0, sl, hs] = (_rms(o, nw_ref[...]) * _silu(z_ref[0, sl, hs])).astype(o_ref.dtype)
    for hi in range(nheads):
        s_ref[hi] = states[hi]


def _delta_mixer(proj, conv_w, a_log, dt_bias, norm_w, *, tt_cap=512):
    b, t, _ = proj.shape
    tt = _tile(t, tt_cap, DN_CHUNK)
    hh = DN_HEADS
    hps = DN_HEADS_PER_STEP
    wd = hps * V7X_LANES
    npair = hh // hps
    blk = lambda off: pl.BlockSpec((1, tt, wd), lambda bi, h, ti, off=off: (bi, ti, off + h))
    halo = lambda off: pl.BlockSpec((1, V7X_SUBLANES, wd), _halo_map(lambda h, off=off: off + h, tt))
    cw = lambda off: pl.BlockSpec((4, wd), lambda bi, h, ti, off=off: (0, off + h))
    row_spec = pl.BlockSpec((1, V7X_LANES), lambda bi, h, ti: (0, 0))
    pad = jnp.zeros((V7X_LANES - 2 * hh,), F32)
    alog_row = jnp.concatenate([jnp.zeros((hh,), F32), a_log, pad]).reshape(1, V7X_LANES)
    dtb_row = jnp.concatenate([jnp.zeros((hh,), F32), dt_bias, pad]).reshape(1, V7X_LANES)
    ba_col = 6 * hh
    return pl.pallas_call(
        _delta_kernel,
        grid=(b, npair, t // tt),
        in_specs=[
            blk(0), halo(0), blk(npair), halo(npair), blk(2 * npair), halo(2 * npair), blk(3 * npair),
            pl.BlockSpec((1, tt, V7X_LANES), lambda bi, h, ti: (bi, ti, ba_col)),
            cw(0), cw(npair), cw(2 * npair), row_spec, row_spec, row_spec,
        ],
        out_specs=pl.BlockSpec((1, tt, wd), lambda bi, h, ti: (bi, ti, h)),
        out_shape=jax.ShapeDtypeStruct((b, t, hh * V7X_LANES), BF16),
        scratch_shapes=[pltpu.VMEM((hps, V7X_LANES, V7X_LANES), F32)],
        compiler_params=_cparams("parallel", "parallel", "arbitrary"),
        name="delta_mixer",
    )(proj, proj, proj, proj, proj, proj, proj, proj, conv_w, conv_w, conv_w,
      alog_row, dtb_row, norm_w.reshape(1, V7X_LANES))


def _lru_kernel(x_ref, xh_ref, y_ref, cw_ref, cb_ref, wa_ref, ba_ref, wx_ref, bx_ref, lam_ref,
                o_ref, h_ref):
    first = pl.program_id(2) == 0
    tt = x_ref.shape[1]

    @pl.when(first)
    def _():
        h_ref[...] = jnp.zeros_like(h_ref)

    xc = _causal_conv(x_ref, xh_ref, cw_ref, first) + cb_ref[...]
    xb = xc.astype(BF16)
    r = jax.nn.sigmoid(_dot(xb, wa_ref[0]) + ba_ref[...])
    i = jax.nn.sigmoid(_dot(xb, wx_ref[0]) + bx_ref[...])
    log_a = -LRU_C * r * jax.nn.softplus(-lam_ref[...])
    a = jnp.exp(log_a)
    u = jnp.sqrt(-_expm1(2.0 * log_a)) * (i * xc)

    rowi = lax.broadcasted_iota(jnp.int32, (tt, V7X_LANES), 0)
    s = 1
    while s < tt:
        if s < V7X_SUBLANES:
            a_sh = jnp.where(rowi >= s, pltpu.roll(a, s, axis=0), 1.0)
            u_sh = jnp.where(rowi >= s, pltpu.roll(u, s, axis=0), 0.0)
        else:
            a_sh = jnp.concatenate([jnp.ones((s, V7X_LANES), F32), a[:tt - s]], axis=0)
            u_sh = jnp.concatenate([jnp.zeros((s, V7X_LANES), F32), u[:tt - s]], axis=0)
        u = a * u_sh + u
        a = a * a_sh
        s *= 2
    hs = u + a * h_ref[...]
    h_ref[...] = hs[tt - 1:tt]
    o_ref[0] = (hs * jax.nn.gelu(y_ref[0])).astype(o_ref.dtype)


def _lru_mixer(proj, conv_w, conv_b, wa, ba, wx, bx, lam, *, x_col, y_col, tt_cap=512):
    b, t, _ = proj.shape
    tt = _tile(t, tt_cap, 16)
    gg = LRU_BLOCKS
    vec = lambda a: a.reshape(1, gg * V7X_LANES)
    vspec = pl.BlockSpec((1, V7X_LANES), lambda bi, g, ti: (0, g))
    wspec = pl.BlockSpec((1, V7X_LANES, V7X_LANES), lambda bi, g, ti: (g, 0, 0))
    return pl.pallas_call(
        _lru_kernel,
        grid=(b, gg, t // tt),
        in_specs=[
            pl.BlockSpec((1, tt, V7X_LANES), lambda bi, g, ti: (bi, ti, x_col + g)),
            pl.BlockSpec((1, V7X_SUBLANES, V7X_LANES), _halo_map(lambda g: x_col + g, tt)),
            pl.BlockSpec((1, tt, V7X_LANES), lambda bi, g, ti: (bi, ti, y_col + g)),
            pl.BlockSpec((4, V7X_LANES), lambda bi, g, ti: (0, g)),
            vspec, wspec, vspec, wspec, vspec, vspec,
        ],
        out_specs=pl.BlockSpec((1, tt, V7X_LANES), lambda bi, g, ti: (bi, ti, g)),
        out_shape=jax.ShapeDtypeStruct((b, t, gg * V7X_LANES), BF16),
        scratch_shapes=[pltpu.VMEM((1, V7X_LANES), F32)],
        compiler_params=_cparams("parallel", "parallel", "arbitrary"),
        name="lru_mixer",
    )(proj, proj, proj, conv_w, vec(conv_b), wa.astype(BF16), vec(ba), wx.astype(BF16), vec(bx), vec(lam))


def _dilated_kernel(q_ref, kc_ref, kp_ref, vc_ref, vp_ref, o_ref, kk_ref, vv_ref, m_ref, l_ref, acc_ref):
    h = pl.program_id(1)
    has_prev = pl.program_id(2) > 0
    tq = q_ref.shape[1]
    bq = SWA_BLOCK
    kk_ref[0:tq, :] = kp_ref[0]
    kk_ref[tq:2 * tq, :] = kc_ref[0]
    vv_ref[0:tq, :] = vp_ref[0]
    vv_ref[tq:2 * tq, :] = vc_ref[0]

    slope = jnp.exp2(-(jnp.full((1, 1), h, jnp.int32).astype(F32) + 1.0) * (8.0 / SWA_HEADS))
    iq = lax.broadcasted_iota(jnp.int32, (bq, 2 * bq), 0)
    ik = lax.broadcasted_iota(jnp.int32, (bq, 2 * bq), 1)
    rel = bq + iq - ik
    in_window = (rel >= 0) & (rel <= SWA_SPAN)
    relf = rel.astype(F32)
    scale = V7X_LANES ** -0.5
    ones_blk = jnp.ones((2 * bq, V7X_LANES), BF16)

    for bi, d in enumerate(SWA_DILATIONS):
        nblk = tq // (bq * d)
        assert nblk >= 1 and nblk & (nblk - 1) == 0
        bias = (slope * float(d)) * relf

        def rows(ref, base, d=d):
            if d == 1:
                return ref[pl.ds(base, bq), :]
            return ref[pl.ds(base, bq, stride=d), :]

        def body(idx, carry, d=d, nblk=nblk, bias=bias, bi=bi):
            r = idx >> (nblk.bit_length() - 1)
            n = idx & (nblk - 1)
            base = n * (bq * d) + r
            qb = (rows(q_ref.at[0], base) * scale).astype(BF16)
            kcat = jnp.concatenate([rows(kk_ref, tq + base - bq * d), rows(kk_ref, tq + base)], axis=0)
            vcat = jnp.concatenate([rows(vv_ref, tq + base - bq * d), rows(vv_ref, tq + base)], axis=0)
            s = _dot_nt(qb, kcat.astype(BF16)) - bias
            valid = in_window & ((ik >= bq) | (n > 0) | has_prev)
            s = jnp.where(valid, s, MASK_VALUE)
            m_b = jnp.max(s, axis=-1, keepdims=True)
            p = jnp.exp(s - m_b).astype(BF16)
            pv = _dot(p, jnp.concatenate([vcat.astype(BF16), ones_blk], axis=1))
            acc_b = pv[:, :V7X_LANES]
            l_b = pv[:, V7X_LANES:]
            m_b = jnp.broadcast_to(m_b, (bq, V7X_LANES))
            if bi > 0:
                m_o, l_o, acc_o = rows(m_ref, base), rows(l_ref, base), rows(acc_ref, base)
                m_n = jnp.maximum(m_o, m_b)
                alpha = jnp.exp(m_o - m_n)
                beta = jnp.exp(m_b - m_n)
                m_b = m_n
                l_b = alpha * l_o + beta * l_b
                acc_b = alpha * acc_o + beta * acc_b
            if d == 1:
                m_ref[pl.ds(base, bq), :] = m_b
                l_ref[pl.ds(base, bq), :] = l_b
                acc_ref[pl.ds(base, bq), :] = acc_b
            else:
                m_ref[pl.ds(base, bq, stride=d), :] = m_b
                l_ref[pl.ds(base, bq, stride=d), :] = l_b
                acc_ref[pl.ds(base, bq, stride=d), :] = acc_b
            return carry

        lax.fori_loop(0, d * nblk, body, 0, unroll=SWA_UNROLL)

    o_ref[0] = (acc_ref[...] / l_ref[...]).astype(o_ref.dtype)


def _dilated_mixer(proj, *, q_col, k_col, v_col, tq=2048):
    b, t, _ = proj.shape
    assert t % tq == 0 and tq % (SWA_BLOCK * max(SWA_DILATIONS)) == 0
    hh = SWA_HEADS
    cur = lambda off: pl.BlockSpec((1, tq, V7X_LANES), lambda bi, h, ti, off=off: (bi, ti, off + h))
    prev = lambda off: pl.BlockSpec((1, tq, V7X_LANES),
                                    lambda bi, h, ti, off=off: (bi, jnp.maximum(ti - 1, 0), off + h))
    return pl.pallas_call(
        _dilated_kernel,
        grid=(b, hh, t // tq),
        in_specs=[cur(q_col), cur(k_col), prev(k_col), cur(v_col), prev(v_col)],
        out_specs=pl.BlockSpec((1, tq, V7X_LANES), lambda bi, h, ti: (bi, ti, h)),
        out_shape=jax.ShapeDtypeStruct((b, t, hh * V7X_LANES), BF16),
        scratch_shapes=[pltpu.VMEM((2 * tq, V7X_LANES), F32), pltpu.VMEM((2 * tq, V7X_LANES), F32),
                        pltpu.VMEM((tq, V7X_LANES), F32), pltpu.VMEM((tq, V7X_LANES), F32),
                        pltpu.VMEM((tq, V7X_LANES), F32)],
        compiler_params=_cparams("parallel", "parallel", "parallel"),
        name="dilated_mixer",
    )(proj, proj, proj, proj, proj)


def _retention_kernel(q_ref, k_ref, v_ref, g_ref, o_ref, s_ref):
    h = pl.program_id(1)
    c = q_ref.shape[1]

    @pl.when(pl.program_id(2) == 0)
    def _():
        s_ref[...] = jnp.zeros_like(s_ref)

    hf = jnp.full((1, 1), h, jnp.int32).astype(F32)
    log_gamma = jnp.log1p(-jnp.exp2(-5.0 - hf))
    q = q_ref[0]
    k = k_ref[0] * (V7X_LANES ** -0.5)
    v = v_ref[0].astype(BF16)
    row = lax.broadcasted_iota(jnp.int32, (c, c), 0)
    col = lax.broadcasted_iota(jnp.int32, (c, c), 1)
    rel = (row - col).astype(F32)
    dmask = jnp.where(rel >= 0, jnp.exp(jnp.maximum(rel, 0.0) * log_gamma), 0.0)
    intra = _dot((_dot_nt(q.astype(BF16), k.astype(BF16)) * dmask).astype(BF16), v)
    idx = lax.broadcasted_iota(jnp.int32, (c, 1), 0).astype(F32)
    q_dec = q * jnp.exp((idx + 1.0) * log_gamma)
    k_dec = k * jnp.exp((c - 1.0 - idx) * log_gamma)
    state = s_ref[...]
    inter = _dot(q_dec.astype(BF16), state.astype(BF16))
    s_ref[...] = state * jnp.exp(float(c) * log_gamma) + _dot_tn(k_dec.astype(BF16), v)
    o = intra + inter
    mu = jnp.mean(o, axis=-1, keepdims=True)
    oc = o - mu
    o = oc * lax.rsqrt(jnp.mean(oc * oc, axis=-1, keepdims=True) + GN_EPS)
    o_ref[0] = (o * _silu(g_ref[0])).astype(o_ref.dtype)


def _retention_mixer(proj, *, q_col, k_col, v_col, g_col):
    b, t, _ = proj.shape
    c = _tile(t, RET_CHUNK, 16)
    hh = RET_HEADS
    wide = RET_DV // V7X_LANES
    assert v_col % wide == 0 and g_col % wide == 0
    nar = lambda off: pl.BlockSpec((1, c, V7X_LANES), lambda bi, h, ti, off=off: (bi, ti, off + h))
    wid = lambda off: pl.BlockSpec((1, c, RET_DV), lambda bi, h, ti, off=off: (bi, ti, off // wide + h))
    return pl.pallas_call(
        _retention_kernel,
        grid=(b, hh, t // c),
        in_specs=[nar(q_col), nar(k_col), wid(v_col), wid(g_col)],
        out_specs=pl.BlockSpec((1, c, RET_DV), lambda bi, h, ti: (bi, ti, h)),
        out_shape=jax.ShapeDtypeStruct((b, t, hh * RET_DV), BF16),
        scratch_shapes=[pltpu.VMEM((V7X_LANES, RET_DV), F32)],
        compiler_params=_cparams("parallel", "parallel", "arbitrary"),
        name="retention_mixer",
    )(proj, proj, proj, proj)


def _even_w_in_layout(w):
    d = w.shape[0]
    qkvz = 4 * DN_HEADS * V7X_LANES
    nba = 2 * DN_HEADS
    pad = jnp.zeros((d, 2 * V7X_LANES - nba), w.dtype)
    return jnp.concatenate([w[:, :qkvz], w[:, qkvz + nba:], w[:, qkvz:qkvz + nba], pad], axis=1)


def kernel(x, p, ln_mix_w, ln_mlp_w, ln_ple_w, w_up, w_down, w_ple_proj, w_ple_gate, ln_final_w,
           ev_w_in, ev_w_out, dn_conv_w, dn_a_log, dn_dt_bias, dn_norm_w,
           lru_conv_w, lru_conv_b, lru_wa, lru_ba, lru_wx, lru_bx, lru_lambda,
           od_w_in, od_w_out):
    b, t, d = x.shape
    depth = ln_mix_w.shape[0]
    m = b * t
    h = x.reshape(m, d)
    w_up, w_down, w_ple_proj, w_ple_gate, ev_w_in, ev_w_out, od_w_in, od_w_out = (
        _to_bf16(w) for w in (w_up, w_down, w_ple_proj, w_ple_gate, ev_w_in, ev_w_out, od_w_in, od_w_out))
    for i in range(depth):
        j = i // 2
        if i % 2 == 0:
            w_in = _even_w_in_layout(ev_w_in[j])
            proj = _norm_matmul(h, ln_mix_w[i], w_in, act=None, out_dtype=F32)
            proj = proj.reshape(b, t, -1)
            y_a = _delta_mixer(proj, dn_conv_w[j], dn_a_log[j], dn_dt_bias[j], dn_norm_w[j])
            y_b = _lru_mixer(proj, lru_conv_w[j], lru_conv_b[j], lru_wa[j], lru_ba[j], lru_wx[j], lru_bx[j],
                             lru_lambda[j], x_col=4 * DN_HEADS, y_col=4 * DN_HEADS + LRU_BLOCKS)
            w_out = ev_w_out[j]
        else:
            proj = _norm_matmul(h, ln_mix_w[i], od_w_in[j], act=None, out_dtype=F32)
            proj = proj.reshape(b, t, -1)
            y_a = _dilated_mixer(proj, q_col=0, k_col=SWA_HEADS, v_col=2 * SWA_HEADS)
            y_b = _retention_mixer(proj, q_col=3 * SWA_HEADS, k_col=3 * SWA_HEADS + RET_HEADS,
                                   v_col=3 * SWA_HEADS + 2 * RET_HEADS,
                                   g_col=3 * SWA_HEADS + 2 * RET_HEADS + RET_HEADS * RET_DV // V7X_LANES)
            w_out = od_w_out[j]
        h = _matmul_residual([y_a.reshape(m, -1), y_b.reshape(m, -1)], w_out, h, tn_cap=d)
        up = _norm_matmul(h, ln_mlp_w[i], w_up[i], act="relu2", out_dtype=BF16)
        h = _matmul_residual([up], w_down[i], h)
        h = _ple(h, ln_ple_w[i], w_ple_gate[i], p[i].reshape(m, -1), w_ple_proj[i],
                 ln_final_w, final_norm=(i == depth - 1))
    return h.reshape(b, t, d)
```

```python
import functools

import jax
import jax.numpy as jnp
from jax import lax
from jax.experimental import pallas as pl
from jax.experimental.pallas import tpu as pltpu

F32 = jnp.float32
BF16 = jnp.bfloat16

V7X_LANES = 128
V7X_SUBLANES = 8
V7X_VMEM_LIMIT_BYTES = 56 * 1024 * 1024

NORM_EPS = 1e-6
GN_EPS = 1e-5
LRU_C = 8.0
DN_HEADS = 8
LRU_BLOCKS = 8
SWA_HEADS = 8
RET_HEADS = 4
RET_DV = 256
SWA_DILATIONS = (1, 4, 16)
SWA_SPAN = 128
SWA_BLOCK = 128
SWA_UNROLL = 8
DN_CHUNK = 128
DN_HEADS_PER_STEP = 4
RET_CHUNK = 256
RET_TILE = 1024
MASK_VALUE = -1e30
MLP_UP_TN = 2048
MLP_DOWN_TN = 256


def _cparams(*sem):
    return pltpu.CompilerParams(dimension_semantics=sem, vmem_limit_bytes=V7X_VMEM_LIMIT_BYTES)


def _dot(a, b):
    return jnp.dot(a, b, preferred_element_type=F32)


def _dot_nt(a, b):
    return lax.dot_general(a, b, (((1,), (1,)), ((), ())), preferred_element_type=F32)


def _dot_tn(a, b):
    return lax.dot_general(a, b, (((0,), (0,)), ((), ())), preferred_element_type=F32)


def _split_bf16(x):
    hi = x.astype(BF16)
    lo = (x - hi.astype(F32)).astype(BF16)
    return hi, lo


def _dot_f32(a, b):
    ah, al = a if isinstance(a, tuple) else _split_bf16(a)
    bh, bl = b if isinstance(b, tuple) else _split_bf16(b)
    n = bh.shape[1]
    wide = _dot(ah, jnp.concatenate([bh, bl], axis=1))
    return wide[:, :n] + (wide[:, n:] + _dot(al, bh))


def _rms(x, w):
    return x * lax.rsqrt(jnp.mean(x * x, axis=-1, keepdims=True) + NORM_EPS) * w


def _silu(x):
    return x * jax.nn.sigmoid(x)


def _expm1(x):
    u = jnp.exp(x)
    lg = jnp.where(u == 1.0, 1.0, jnp.log(u))
    return jnp.where(u == 1.0, x, (u - 1.0) * x / lg)


def _tile(n, cap, quantum):
    if n <= cap:
        return n
    best = None
    for c in range(quantum, cap + 1, quantum):
        if n % c == 0:
            best = c
    assert best is not None, (n, cap, quantum)
    return best


CAST_BLOCK_BYTES = 8 * 1024 * 1024


def _cast_kernel(x_ref, o_ref):
    o_ref[...] = x_ref[...].astype(o_ref.dtype)


def _to_bf16(w, tn=None):
    l, r, c = w.shape
    tn = c if tn is None else tn
    tr = _tile(r, max(16, CAST_BLOCK_BYTES // (4 * tn) // 16 * 16), 16)
    out = pl.pallas_call(
        _cast_kernel,
        grid=(l, c // tn, r // tr),
        in_specs=[pl.BlockSpec((None, tr, tn), lambda i, j, k: (i, k, j))],
        out_specs=pl.BlockSpec((None, None, tr, tn), lambda i, j, k: (i, j, k, 0)),
        out_shape=jax.ShapeDtypeStruct((l, c // tn, r, tn), BF16),
        compiler_params=_cparams("parallel", "parallel", "parallel"),
        name="to_bf16",
    )(w)
    return out if tn != c else out.reshape(l, r, c)


def _w_spec(w, layer, kk, tn, row_block=0):
    if w.ndim == 2:
        return pl.BlockSpec((kk, tn), lambda i, j: (row_block, j))
    if w.ndim == 3:
        return pl.BlockSpec((None, kk, tn), lambda i, j: (layer, row_block, j))
    assert w.shape[3] == tn, (w.shape, tn)
    return pl.BlockSpec((None, None, kk, tn), lambda i, j: (layer, j, row_block, 0))


def _w_cols(w):
    return w.shape[-1] if w.ndim < 4 else w.shape[1] * w.shape[3]


def _norm_mm_kernel(x_ref, g_ref, w_ref, o_ref, hn_ref, *, act):
    @pl.when(pl.program_id(1) == 0)
    def _():
        hn_ref[...] = _rms(x_ref[...], g_ref[...]).astype(BF16)

    a = _dot(hn_ref[...], w_ref[...])
    if act == "relu2":
        a = jnp.square(jnp.maximum(a, 0.0))
    o_ref[...] = a.astype(o_ref.dtype)


def _norm_matmul(x, g, w, *, act, out_dtype, layer=0, tm_cap=1024, tn_cap=1280):
    m, k = x.shape
    n = _w_cols(w)
    tm = _tile(m, tm_cap, 256)
    tn = w.shape[3] if w.ndim == 4 else _tile(n, tn_cap, 256)
    return pl.pallas_call(
        functools.partial(_norm_mm_kernel, act=act),
        grid=(m // tm, n // tn),
        in_specs=[
            pl.BlockSpec((tm, k), lambda i, j: (i, 0)),
            pl.BlockSpec((1, k), lambda i, j: (0, 0)),
            _w_spec(w, layer, k, tn),
        ],
        out_specs=pl.BlockSpec((tm, tn), lambda i, j: (i, j)),
        out_shape=jax.ShapeDtypeStruct((m, n), out_dtype),
        scratch_shapes=[pltpu.VMEM((tm, k), BF16)],
        compiler_params=_cparams("parallel", "arbitrary"),
        name="norm_matmul",
    )(x, g.reshape(1, k), w)


def _mm_res_kernel(*refs, n_x):
    xs, ws = refs[:n_x], refs[n_x:2 * n_x]
    res_ref, o_ref = refs[2 * n_x], refs[2 * n_x + 1]
    acc = res_ref[...]
    for x_ref, w_ref in zip(xs, ws):
        acc = acc + _dot(x_ref[...], w_ref[...])
    o_ref[...] = acc


def _matmul_residual(xs, w, res, *, layer=0, tm_cap=512, tn_cap=512):
    m, n = res.shape
    assert n == _w_cols(w)
    tm = _tile(m, tm_cap, 256)
    tn = w.shape[3] if w.ndim == 4 else _tile(n, tn_cap, 256)
    in_specs, row = [], 0
    for x in xs:
        in_specs.append(pl.BlockSpec((tm, x.shape[1]), lambda i, j: (i, 0)))
    w_specs = []
    for x in xs:
        kk = x.shape[1]
        assert row % kk == 0
        w_specs.append(_w_spec(w, layer, kk, tn, row_block=row // kk))
        row += kk
    assert row == w.shape[-2]
    return pl.pallas_call(
        functools.partial(_mm_res_kernel, n_x=len(xs)),
        grid=(m // tm, n // tn),
        in_specs=in_specs + w_specs + [pl.BlockSpec((tm, tn), lambda i, j: (i, j))],
        out_specs=pl.BlockSpec((tm, tn), lambda i, j: (i, j)),
        out_shape=jax.ShapeDtypeStruct((m, n), F32),
        compiler_params=_cparams("parallel", "parallel"),
        name="matmul_residual",
    )(*xs, *([w] * len(xs)), res)


def _ple_kernel(x_ref, g_ref, wg_ref, p_ref, wp_ref, gf_ref, o_ref, *, final_norm):
    d = x_ref.shape[1]
    hn = _rms(x_ref[...], g_ref[...]).astype(BF16)
    pb = p_ref[...].astype(BF16)
    nh = d // 2 if d % (2 * V7X_LANES) == 0 else d
    for c0 in range(0, d, nh):
        gate = jax.nn.sigmoid(_dot(hn, wg_ref[:, c0:c0 + nh]))
        pp = _dot(pb, wp_ref[:, c0:c0 + nh])
        o_ref[:, c0:c0 + nh] = x_ref[:, c0:c0 + nh] + gate * pp
    if final_norm:
        o_ref[...] = _rms(o_ref[...], gf_ref[...])


def _ple(x, g, wg, p, wp, gf, *, layer, final_norm, tm_cap=512):
    m, d = x.shape
    pd = p.shape[2]
    tm = _tile(m, tm_cap, 256)
    return pl.pallas_call(
        functools.partial(_ple_kernel, final_norm=final_norm),
        grid=(m // tm,),
        in_specs=[
            pl.BlockSpec((tm, d), lambda i: (i, 0)),
            pl.BlockSpec((1, d), lambda i: (0, 0)),
            pl.BlockSpec((None, d, d), lambda i: (layer, 0, 0)),
            pl.BlockSpec((None, tm, pd), lambda i: (layer, i, 0)),
            pl.BlockSpec((None, pd, d), lambda i: (layer, 0, 0)),
            pl.BlockSpec((1, d), lambda i: (0, 0)),
        ],
        out_specs=pl.BlockSpec((tm, d), lambda i: (i, 0)),
        out_shape=jax.ShapeDtypeStruct((m, d), F32),
        compiler_params=_cparams("parallel"),
        name="ple",
    )(x, g.reshape(1, d), wg, p, wp, gf.reshape(1, d))


def _causal_conv(x_ref, halo_ref, w_ref, first):
    x = x_ref[0]
    halo = jnp.where(first, 0.0, halo_ref[0])
    xp = jnp.concatenate([halo, x], axis=0)
    w = w_ref[...]
    y = pltpu.roll(xp, 3, axis=0)[V7X_SUBLANES:] * w[0:1]
    y = y + pltpu.roll(xp, 2, axis=0)[V7X_SUBLANES:] * w[1:2]
    y = y + pltpu.roll(xp, 1, axis=0)[V7X_SUBLANES:] * w[2:3]
    return y + x * w[3:4]


def _halo_map(col, rows_per_tile):
    nb = rows_per_tile // V7X_SUBLANES
    return lambda b, h, t: (b, jnp.maximum(t * nb - 1, 0), col(h))


def _unit_lower_inverses(ms, row, col):
    c = ms[0].shape[0]
    eye = (row == col).astype(F32)
    diag16 = (row >> 4) == (col >> 4)
    b16 = lambda xs: [x.astype(BF16) for x in xs]
    ns = [jnp.where(diag16, m, 0.0) for m in ms]
    invs = [eye - n for n in ns]
    qs = b16(ns)
    for step in range(3):
        qs = b16([_dot(q, q) for q in qs])
        invs = [inv + _dot(inv.astype(BF16), q) for inv, q in zip(invs, qs)]
    shift = 4
    while (1 << shift) < c:
        off = ((row >> (shift + 1)) == (col >> (shift + 1))) & ((row >> shift) != (col >> shift))
        inv16 = b16(invs)
        tmps = [_dot(inv, jnp.where(off, m, 0.0).astype(BF16)) for inv, m in zip(inv16, ms)]
        invs = [inv - _dot(tmp.astype(BF16), i16) for inv, tmp, i16 in zip(invs, tmps, inv16)]
        shift += 1
    xs = [_split_bf16(inv) for inv in invs]
    res = [eye - inv - _dot_f32(m, x) for inv, m, x in zip(invs, ms, xs)]
    return [inv + _dot_f32(x, r) for inv, x, r in zip(invs, xs, res)]


def _delta_kernel(q_ref, qh_ref, k_ref, kh_ref, v_ref, vh_ref, z_ref, ba_ref,
                  wq_ref, wk_ref, wv_ref, alog_ref, dtb_ref, nw_ref, o_ref, s_ref):
    hp = pl.program_id(1)
    first = pl.program_id(2) == 0
    tt = q_ref.shape[1]
    c = DN_CHUNK
    dh = V7X_LANES
    nheads = q_ref.shape[2] // dh

    @pl.when(first)
    def _():
        s_ref[...] = jnp.zeros_like(s_ref)

    def l2n(x):
        return x * lax.rsqrt(jnp.sum(x * x, axis=-1, keepdims=True) + 1e-6)

    q_all = _silu(_causal_conv(q_ref, qh_ref, wq_ref, first))
    k_all = _silu(_causal_conv(k_ref, kh_ref, wk_ref, first))
    v_all = _silu(_causal_conv(v_ref, vh_ref, wv_ref, first))
    ba = ba_ref[0]
    lane = lax.broadcasted_iota(jnp.int32, ba.shape, 1)
    sig_ba = jax.nn.sigmoid(ba)
    g_all = -jnp.exp(alog_ref[...]) * jax.nn.softplus(ba + dtb_ref[...])

    row = lax.broadcasted_iota(jnp.int32, (c, c), 0)
    col = lax.broadcasted_iota(jnp.int32, (c, c), 1)
    causal = row >= col
    strict = row > col
    ltri = causal.astype(BF16)

    nch = tt // c

    def cumsum_block(gblk):
        g_hi, g_lo = _split_bf16(gblk)
        g_lo2 = (gblk - g_hi.astype(F32) - g_lo.astype(F32)).astype(BF16)
        return _dot(ltri, g_hi) + (_dot(ltri, g_lo) + _dot(ltri, g_lo2))

    gc_blocks = [cumsum_block(g_all[ci * c:(ci + 1) * c]) for ci in range(nch)]
    gc_blocks_t = [gcb.T for gcb in gc_blocks]

    qs, ks, vs, bs, gcs, gc_rows = [], [], [], [], [], []
    for hi in range(nheads):
        hs = slice(hi * dh, (hi + 1) * dh)
        head = nheads * hp + hi
        qh = l2n(q_all[:, hs]) * (dh ** -0.5)
        kh = l2n(k_all[:, hs])
        vh = v_all[:, hs]
        beta = jnp.sum(jnp.where(lane == head, sig_ba, 0.0), axis=-1, keepdims=True)
        for ci in range(nch):
            sl = slice(ci * c, (ci + 1) * c)
            qs.append(qh[sl]); ks.append(kh[sl]); vs.append(vh[sl]); bs.append(beta[sl])
            gcol = jnp.sum(jnp.where(col == head + DN_HEADS, gc_blocks[ci], 0.0), axis=-1, keepdims=True)
            grow = jnp.sum(jnp.where(row == head + DN_HEADS, gc_blocks_t[ci], 0.0), axis=0, keepdims=True)
            gcs.append(jnp.broadcast_to(gcol, (c, c)))
            gc_rows.append(jnp.broadcast_to(grow, (c, c)))

    decays = [jnp.where(causal, jnp.exp(jnp.where(causal, gc - gr, 0.0)), 0.0) for gc, gr in zip(gcs, gc_rows)]
    kbs = [kc * bc for kc, bc in zip(ks, bs)]
    k16 = [kc.astype(BF16) for kc in ks]
    ms = [jnp.where(strict, _dot_nt(kb.astype(BF16), kc) * dec, 0.0) for kb, kc, dec in zip(kbs, k16, decays)]
    invs = _unit_lower_inverses(ms, row, col)
    egcs = [jnp.exp(gc) for gc in gcs]
    sols = [_dot_f32(inv, jnp.concatenate([vc * bc, kb * egc], axis=1))
            for inv, vc, bc, kb, egc in zip(invs, vs, bs, kbs, egcs)]
    us = [sol[:, :dh] for sol in sols]
    ws = [sol[:, dh:].astype(BF16) for sol in sols]
    qks = [(_dot_nt(qc.astype(BF16), kc) * dec).astype(BF16) for qc, kc, dec in zip(qs, k16, decays)]
    q_decs = [(qc * egc).astype(BF16) for qc, egc in zip(qs, egcs)]
    lasts = [gc[c - 1:c, :] for gc in gcs]
    k_decs = [(kc * jnp.exp(last - gc)).astype(BF16) for kc, last, gc in zip(ks, lasts, gcs)]
    g_tots = [jnp.exp(last) for last in lasts]

    nch = tt // c
    states = [s_ref[hi] for hi in range(nheads)]
    for ci in range(nch):
        sl = slice(ci * c, (ci + 1) * c)
        for hi in range(nheads):
            i = hi * nch + ci
            hs = slice(hi * dh, (hi + 1) * dh)
            sb = states[hi].astype(BF16)
            v_new = (us[i] - _dot(ws[i], sb)).astype(BF16)
            o = _dot(q_decs[i], sb) + _dot(qks[i], v_new)
            states[hi] = states[hi] * g_tots[i] + _dot_tn(k_decs[i], v_new)
            o_ref[0, sl, hs] = (_rms(o, nw_ref[...]) * _silu(z_ref[0, sl, hs])).astype(o_ref.dtype)
    for hi in range(nheads):
        s_ref[hi] = states[hi]


def _delta_mixer(proj, conv_w, a_log, dt_bias, norm_w, *, tt_cap=512):
    b, t, _ = proj.shape
    tt = _tile(t, tt_cap, DN_CHUNK)
    hh = DN_HEADS
    hps = DN_HEADS_PER_STEP
    wd = hps * V7X_LANES
    npair = hh // hps
    blk = lambda off: pl.BlockSpec((1, tt, wd), lambda bi, h, ti, off=off: (bi, ti, off + h))
    halo = lambda off: pl.BlockSpec((1, V7X_SUBLANES, wd), _halo_map(lambda h, off=off: off + h, tt))
    cw = lambda off: pl.BlockSpec((4, wd), lambda bi, h, ti, off=off: (0, off + h))
    row_spec = pl.BlockSpec((1, V7X_LANES), lambda bi, h, ti: (0, 0))
    pad = jnp.zeros((V7X_LANES - 2 * hh,), F32)
    alog_row = jnp.concatenate([jnp.zeros((hh,), F32), a_log, pad]).reshape(1, V7X_LANES)
    dtb_row = jnp.concatenate([jnp.zeros((hh,), F32), dt_bias, pad]).reshape(1, V7X_LANES)
    ba_col = 6 * hh
    return pl.pallas_call(
        _delta_kernel,
        grid=(b, npair, t // tt),
        in_specs=[
            blk(0), halo(0), blk(npair), halo(npair), blk(2 * npair), halo(2 * npair), blk(3 * npair),
            pl.BlockSpec((1, tt, V7X_LANES), lambda bi, h, ti: (bi, ti, ba_col)),
            cw(0), cw(npair), cw(2 * npair), row_spec, row_spec, row_spec,
        ],
        out_specs=pl.BlockSpec((1, tt, wd), lambda bi, h, ti: (bi, ti, h)),
        out_shape=jax.ShapeDtypeStruct((b, t, hh * V7X_LANES), BF16),
        scratch_shapes=[pltpu.VMEM((hps, V7X_LANES, V7X_LANES), F32)],
        compiler_params=_cparams("parallel", "parallel", "arbitrary"),
        name="delta_mixer",
    )(proj, proj, proj, proj, proj, proj, proj, proj, conv_w, conv_w, conv_w,
      alog_row, dtb_row, norm_w.reshape(1, V7X_LANES))


def _lru_kernel(x_ref, xh_ref, y_ref, cw_ref, cb_ref, wa_ref, ba_ref, wx_ref, bx_ref, lam_ref,
                o_ref, h_ref):
    first = pl.program_id(2) == 0
    tt = x_ref.shape[1]

    @pl.when(first)
    def _():
        h_ref[...] = jnp.zeros_like(h_ref)

    xc = _causal_conv(x_ref, xh_ref, cw_ref, first) + cb_ref[...]
    xb = xc.astype(BF16)
    r = jax.nn.sigmoid(_dot(xb, wa_ref[0]) + ba_ref[...])
    i = jax.nn.sigmoid(_dot(xb, wx_ref[0]) + bx_ref[...])
    log_a = -LRU_C * r * jax.nn.softplus(-lam_ref[...])
    a = jnp.exp(log_a)
    u = jnp.sqrt(-_expm1(2.0 * log_a)) * (i * xc)

    rowi = lax.broadcasted_iota(jnp.int32, (tt, V7X_LANES), 0)
    s = 1
    while s < tt:
        if s < V7X_SUBLANES:
            a_sh = jnp.where(rowi >= s, pltpu.roll(a, s, axis=0), 1.0)
            u_sh = jnp.where(rowi >= s, pltpu.roll(u, s, axis=0), 0.0)
        else:
            a_sh = jnp.concatenate([jnp.ones((s, V7X_LANES), F32), a[:tt - s]], axis=0)
            u_sh = jnp.concatenate([jnp.zeros((s, V7X_LANES), F32), u[:tt - s]], axis=0)
        u = a * u_sh + u
        a = a * a_sh
        s *= 2
    hs = u + a * h_ref[...]
    h_ref[...] = hs[tt - 1:tt]
    o_ref[0] = (hs * jax.nn.gelu(y_ref[0])).astype(o_ref.dtype)


def _lru_mixer(proj, conv_w, conv_b, wa, ba, wx, bx, lam, *, x_col, y_col, tt_cap=512):
    b, t, _ = proj.shape
    tt = _tile(t, tt_cap, 16)
    gg = LRU_BLOCKS
    vec = lambda a: a.reshape(1, gg * V7X_LANES)
    vspec = pl.BlockSpec((1, V7X_LANES), lambda bi, g, ti: (0, g))
    wspec = pl.BlockSpec((1, V7X_LANES, V7X_LANES), lambda bi, g, ti: (g, 0, 0))
    return pl.pallas_call(
        _lru_kernel,
        grid=(b, gg, t // tt),
        in_specs=[
            pl.BlockSpec((1, tt, V7X_LANES), lambda bi, g, ti: (bi, ti, x_col + g)),
            pl.BlockSpec((1, V7X_SUBLANES, V7X_LANES), _halo_map(lambda g: x_col + g, tt)),
            pl.BlockSpec((1, tt, V7X_LANES), lambda bi, g, ti: (bi, ti, y_col + g)),
            pl.BlockSpec((4, V7X_LANES), lambda bi, g, ti: (0, g)),
            vspec, wspec, vspec, wspec, vspec, vspec,
        ],
        out_specs=pl.BlockSpec((1, tt, V7X_LANES), lambda bi, g, ti: (bi, ti, g)),
        out_shape=jax.ShapeDtypeStruct((b, t, gg * V7X_LANES), BF16),
        scratch_shapes=[pltpu.VMEM((1, V7X_LANES), F32)],
        compiler_params=_cparams("parallel", "parallel", "arbitrary"),
        name="lru_mixer",
    )(proj, proj, proj, conv_w, vec(conv_b), wa.astype(BF16), vec(ba), wx.astype(BF16), vec(bx), vec(lam))


def _dilated_kernel(q_ref, kc_ref, kp_ref, vc_ref, vp_ref, o_ref, kk_ref, vv_ref, m_ref, l_ref, acc_ref):
    h = pl.program_id(1)
    has_prev = pl.program_id(2) > 0
    tq = q_ref.shape[1]
    bq = SWA_BLOCK
    kk_ref[0:tq, :] = kp_ref[0]
    kk_ref[tq:2 * tq, :] = kc_ref[0]
    vv_ref[0:tq, :] = vp_ref[0]
    vv_ref[tq:2 * tq, :] = vc_ref[0]

    slope = jnp.exp2(-(jnp.full((1, 1), h, jnp.int32).astype(F32) + 1.0) * (8.0 / SWA_HEADS))
    iq = lax.broadcasted_iota(jnp.int32, (bq, 2 * bq), 0)
    ik = lax.broadcasted_iota(jnp.int32, (bq, 2 * bq), 1)
    rel = bq + iq - ik
    in_window = (rel >= 0) & (rel <= SWA_SPAN)
    relf = rel.astype(F32)
    scale = V7X_LANES ** -0.5
    ones_blk = jnp.ones((2 * bq, V7X_LANES), BF16)

    for bi, d in enumerate(SWA_DILATIONS):
        nblk = tq // (bq * d)
        assert nblk >= 1 and nblk & (nblk - 1) == 0
        bias = (slope * float(d)) * relf

        def rows(ref, base, d=d):
            if d == 1:
                return ref[pl.ds(base, bq), :]
            return ref[pl.ds(base, bq, stride=d), :]

        def body(idx, carry, d=d, nblk=nblk, bias=bias, bi=bi):
            r = idx >> (nblk.bit_length() - 1)
            n = idx & (nblk - 1)
            base = n * (bq * d) + r
            qb = (rows(q_ref.at[0], base) * scale).astype(BF16)
            kcat = jnp.concatenate([rows(kk_ref, tq + base - bq * d), rows(kk_ref, tq + base)], axis=0)
            vcat = jnp.concatenate([rows(vv_ref, tq + base - bq * d), rows(vv_ref, tq + base)], axis=0)
            s = _dot_nt(qb, kcat.astype(BF16)) - bias
            valid = in_window & ((ik >= bq) | (n > 0) | has_prev)
            s = jnp.where(valid, s, MASK_VALUE)
            m_b = jnp.max(s, axis=-1, keepdims=True)
            p = jnp.exp(s - m_b).astype(BF16)
            pv = _dot(p, jnp.concatenate([vcat.astype(BF16), ones_blk], axis=1))
            acc_b = pv[:, :V7X_LANES]
            l_b = pv[:, V7X_LANES:]
            m_b = jnp.broadcast_to(m_b, (bq, V7X_LANES))
            if bi > 0:
                m_o, l_o, acc_o = rows(m_ref, base), rows(l_ref, base), rows(acc_ref, base)
                m_n = jnp.maximum(m_o, m_b)
                alpha = jnp.exp(m_o - m_n)
                beta = jnp.exp(m_b - m_n)
                m_b = m_n
                l_b = alpha * l_o + beta * l_b
                acc_b = alpha * acc_o + beta * acc_b
            if d == 1:
                m_ref[pl.ds(base, bq), :] = m_b
                l_ref[pl.ds(base, bq), :] = l_b
                acc_ref[pl.ds(base, bq), :] = acc_b
            else:
                m_ref[pl.ds(base, bq, stride=d), :] = m_b
                l_ref[pl.ds(base, bq, stride=d), :] = l_b
                acc_ref[pl.ds(base, bq, stride=d), :] = acc_b
            return carry

        lax.fori_loop(0, d * nblk, body, 0, unroll=SWA_UNROLL)

    o_ref[0] = (acc_ref[...] / l_ref[...]).astype(o_ref.dtype)


def _dilated_mixer(proj, *, q_col, k_col, v_col, tq=2048):
    b, t, _ = proj.shape
    assert t % tq == 0 and tq % (SWA_BLOCK * max(SWA_DILATIONS)) == 0
    hh = SWA_HEADS
    cur = lambda off: pl.BlockSpec((1, tq, V7X_LANES), lambda bi, h, ti, off=off: (bi, ti, off + h))
    prev = lambda off: pl.BlockSpec((1, tq, V7X_LANES),
                                    lambda bi, h, ti, off=off: (bi, jnp.maximum(ti - 1, 0), off + h))
    return pl.pallas_call(
        _dilated_kernel,
        grid=(b, hh, t // tq),
        in_specs=[cur(q_col), cur(k_col), prev(k_col), cur(v_col), prev(v_col)],
        out_specs=pl.BlockSpec((1, tq, V7X_LANES), lambda bi, h, ti: (bi, ti, h)),
        out_shape=jax.ShapeDtypeStruct((b, t, hh * V7X_LANES), BF16),
        scratch_shapes=[pltpu.VMEM((2 * tq, V7X_LANES), F32), pltpu.VMEM((2 * tq, V7X_LANES), F32),
                        pltpu.VMEM((tq, V7X_LANES), F32), pltpu.VMEM((tq, V7X_LANES), F32),
                        pltpu.VMEM((tq, V7X_LANES), F32)],
        compiler_params=_cparams("parallel", "parallel", "parallel"),
        name="dilated_mixer",
    )(proj, proj, proj, proj, proj)


def _retention_kernel(q_ref, k_ref, v_ref, g_ref, o_ref, s_ref):
    h = pl.program_id(1)
    tt = q_ref.shape[1]
    c = min(RET_CHUNK, tt)
    nch = tt // c

    @pl.when(pl.program_id(2) == 0)
    def _():
        s_ref[...] = jnp.zeros_like(s_ref)

    hf = jnp.full((1, 1), h, jnp.int32).astype(F32)
    log_gamma = jnp.log1p(-jnp.exp2(-5.0 - hf))
    row = lax.broadcasted_iota(jnp.int32, (c, c), 0)
    col = lax.broadcasted_iota(jnp.int32, (c, c), 1)
    rel = (row - col).astype(F32)
    dmask = jnp.where(rel >= 0, jnp.exp(jnp.maximum(rel, 0.0) * log_gamma), 0.0)
    idx = lax.broadcasted_iota(jnp.int32, (c, 1), 0).astype(F32)
    q_scale = jnp.exp((idx + 1.0) * log_gamma)
    k_scale = jnp.exp((c - 1.0 - idx) * log_gamma)
    chunk_decay = jnp.exp(float(c) * log_gamma)

    sls = [slice(ci * c, (ci + 1) * c) for ci in range(nch)]
    qs = [q_ref[0, sl, :] for sl in sls]
    ks = [k_ref[0, sl, :] * (V7X_LANES ** -0.5) for sl in sls]
    vs = [v_ref[0, sl, :].astype(BF16) for sl in sls]
    intras = [_dot((_dot_nt(q.astype(BF16), k.astype(BF16)) * dmask).astype(BF16), v)
              for q, k, v in zip(qs, ks, vs)]
    kvs = [_dot_tn((k * k_scale).astype(BF16), v) for k, v in zip(ks, vs)]
    states = [s_ref[...]]
    for kv in kvs:
        states.append(states[-1] * chunk_decay + kv)
    s_ref[...] = states[-1]
    for sl, q, intra, state in zip(sls, qs, intras, states):
        o = intra + _dot((q * q_scale).astype(BF16), state.astype(BF16))
        mu = jnp.mean(o, axis=-1, keepdims=True)
        oc = o - mu
        o = oc * lax.rsqrt(jnp.mean(oc * oc, axis=-1, keepdims=True) + GN_EPS)
        o_ref[0, sl, :] = (o * _silu(g_ref[0, sl, :])).astype(o_ref.dtype)


def _retention_mixer(proj, *, q_col, k_col, v_col, g_col):
    b, t, _ = proj.shape
    c = _tile(t, RET_TILE, RET_CHUNK)
    hh = RET_HEADS
    wide = RET_DV // V7X_LANES
    assert v_col % wide == 0 and g_col % wide == 0
    nar = lambda off: pl.BlockSpec((1, c, V7X_LANES), lambda bi, h, ti, off=off: (bi, ti, off + h))
    wid = lambda off: pl.BlockSpec((1, c, RET_DV), lambda bi, h, ti, off=off: (bi, ti, off // wide + h))
    return pl.pallas_call(
        _retention_kernel,
        grid=(b, hh, t // c),
        in_specs=[nar(q_col), nar(k_col), wid(v_col), wid(g_col)],
        out_specs=pl.BlockSpec((1, c, RET_DV), lambda bi, h, ti: (bi, ti, h)),
        out_shape=jax.ShapeDtypeStruct((b, t, hh * RET_DV), BF16),
        scratch_shapes=[pltpu.VMEM((V7X_LANES, RET_DV), F32)],
        compiler_params=_cparams("parallel", "parallel", "arbitrary"),
        name="retention_mixer",
    )(proj, proj, proj, proj)


def _even_w_in_layout(w):
    d = w.shape[0]
    qkvz = 4 * DN_HEADS * V7X_LANES
    nba = 2 * DN_HEADS
    pad = jnp.zeros((d, 2 * V7X_LANES - nba), w.dtype)
    return jnp.concatenate([w[:, :qkvz], w[:, qkvz + nba:], w[:, qkvz:qkvz + nba], pad], axis=1)


def kernel(x, p, ln_mix_w, ln_mlp_w, ln_ple_w, w_up, w_down, w_ple_proj, w_ple_gate, ln_final_w,
           ev_w_in, ev_w_out, dn_conv_w, dn_a_log, dn_dt_bias, dn_norm_w,
           lru_conv_w, lru_conv_b, lru_wa, lru_ba, lru_wx, lru_bx, lru_lambda,
           od_w_in, od_w_out):
    b, t, d = x.shape
    depth = ln_mix_w.shape[0]
    m = b * t
    h = x.reshape(m, d)
    w_ple_proj, w_ple_gate, ev_w_in, ev_w_out, od_w_in, od_w_out = (
        _to_bf16(w) for w in (w_ple_proj, w_ple_gate, ev_w_in, ev_w_out, od_w_in, od_w_out))
    w_up = _to_bf16(w_up, tn=MLP_UP_TN)
    w_down = _to_bf16(w_down, tn=MLP_DOWN_TN)
    p = p.reshape(depth, m, -1)
    for i in range(depth):
        j = i // 2
        if i % 2 == 0:
            w_in = _even_w_in_layout(ev_w_in[j])
            proj = _norm_matmul(h, ln_mix_w[i], w_in, act=None, out_dtype=F32)
            proj = proj.reshape(b, t, -1)
            y_a = _delta_mixer(proj, dn_conv_w[j], dn_a_log[j], dn_dt_bias[j], dn_norm_w[j])
            y_b = _lru_mixer(proj, lru_conv_w[j], lru_conv_b[j], lru_wa[j], lru_ba[j], lru_wx[j], lru_bx[j],
                             lru_lambda[j], x_col=4 * DN_HEADS, y_col=4 * DN_HEADS + LRU_BLOCKS)
            w_out = ev_w_out
        else:
            proj = _norm_matmul(h, ln_mix_w[i], od_w_in, layer=j, act=None, out_dtype=F32)
            proj = proj.reshape(b, t, -1)
            y_a = _dilated_mixer(proj, q_col=0, k_col=SWA_HEADS, v_col=2 * SWA_HEADS)
            y_b = _retention_mixer(proj, q_col=3 * SWA_HEADS, k_col=3 * SWA_HEADS + RET_HEADS,
                                   v_col=3 * SWA_HEADS + 2 * RET_HEADS,
                                   g_col=3 * SWA_HEADS + 2 * RET_HEADS + RET_HEADS * RET_DV // V7X_LANES)
            w_out = od_w_out
        h = _matmul_residual([y_a.reshape(m, -1), y_b.reshape(m, -1)], w_out, h, layer=j, tn_cap=d)
        up = _norm_matmul(h, ln_mlp_w[i], w_up, layer=i, act="relu2", out_dtype=BF16)
        h = _matmul_residual([up], w_down, h, layer=i, tm_cap=1024)
        h = _ple(h, ln_ple_w[i], w_ple_gate, p, w_ple_proj, ln_final_w, layer=i, final_norm=(i == depth - 1))
    return h.reshape(b, t, d)
```

```python
import functools

import jax
import jax.numpy as jnp
from jax import lax
from jax.experimental import pallas as pl
from jax.experimental.pallas import tpu as pltpu

F32 = jnp.float32
BF16 = jnp.bfloat16

V7X_LANES = 128
V7X_SUBLANES = 8
V7X_VMEM_LIMIT_BYTES = 56 * 1024 * 1024

NORM_EPS = 1e-6
GN_EPS = 1e-5
LRU_C = 8.0
DN_HEADS = 8
LRU_BLOCKS = 8
SWA_HEADS = 8
RET_HEADS = 4
RET_DV = 256
SWA_DILATIONS = (1, 4, 16)
SWA_SPAN = 128
SWA_BLOCK = 128
SWA_UNROLL_D1 = 5
SWA_UNROLL = 8
DN_CHUNK = 128
DN_HEADS_PER_STEP = 4
RET_CHUNK = 256
RET_TILE = 1024
MASK_VALUE = -1e30
MLP_UP_TN = 2048


def _cparams(*sem):
    return pltpu.CompilerParams(dimension_semantics=sem, vmem_limit_bytes=V7X_VMEM_LIMIT_BYTES)


def _dot(a, b):
    return jnp.dot(a, b, preferred_element_type=F32)


def _dot_nt(a, b):
    return lax.dot_general(a, b, (((1,), (1,)), ((), ())), preferred_element_type=F32)


def _dot_tn(a, b):
    return lax.dot_general(a, b, (((0,), (0,)), ((), ())), preferred_element_type=F32)


def _split_bf16(x):
    hi = x.astype(BF16)
    lo = (x - hi.astype(F32)).astype(BF16)
    return hi, lo


def _dot_f32(a, b):
    ah, al = a if isinstance(a, tuple) else _split_bf16(a)
    bh, bl = b if isinstance(b, tuple) else _split_bf16(b)
    n = bh.shape[1]
    wide = _dot(ah, jnp.concatenate([bh, bl], axis=1))
    return wide[:, :n] + (wide[:, n:] + _dot(al, bh))


def _rms(x, w):
    return x * lax.rsqrt(jnp.mean(x * x, axis=-1, keepdims=True) + NORM_EPS) * w


def _silu(x):
    return x * jax.nn.sigmoid(x)


def _tile(n, cap, quantum):
    if n <= cap:
        return n
    best = None
    for c in range(quantum, cap + 1, quantum):
        if n % c == 0:
            best = c
    assert best is not None, (n, cap, quantum)
    return best


CAST_BLOCK_BYTES = 8 * 1024 * 1024


def _cast_kernel(x_ref, o_ref):
    o_ref[...] = x_ref[...].astype(o_ref.dtype)


def _to_bf16(w, tn=None):
    l, r, c = w.shape
    tn = c if tn is None else tn
    tr = _tile(r, max(16, CAST_BLOCK_BYTES // (4 * tn) // 16 * 16), 16)
    out = pl.pallas_call(
        _cast_kernel,
        grid=(l, c // tn, r // tr),
        in_specs=[pl.BlockSpec((None, tr, tn), lambda i, j, k: (i, k, j))],
        out_specs=pl.BlockSpec((None, None, tr, tn), lambda i, j, k: (i, j, k, 0)),
        out_shape=jax.ShapeDtypeStruct((l, c // tn, r, tn), BF16),
        compiler_params=_cparams("parallel", "parallel", "parallel"),
        name="to_bf16",
    )(w)
    return out if tn != c else out.reshape(l, r, c)


def _w_spec(w, layer, kk, tn, row_block=0):
    if w.ndim == 2:
        return pl.BlockSpec((kk, tn), lambda i, j: (row_block, j))
    if w.ndim == 3:
        return pl.BlockSpec((None, kk, tn), lambda i, j: (layer, row_block, j))
    assert w.shape[3] == tn, (w.shape, tn)
    return pl.BlockSpec((None, None, kk, tn), lambda i, j: (layer, j, row_block, 0))


def _w_cols(w):
    return w.shape[-1] if w.ndim < 4 else w.shape[1] * w.shape[3]


def _norm_mm_kernel(x_ref, g_ref, w_ref, o_ref, hn_ref, *, act):
    @pl.when(pl.program_id(1) == 0)
    def _():
        hn_ref[...] = _rms(x_ref[...], g_ref[...]).astype(BF16)

    a = _dot(hn_ref[...], w_ref[...])
    if act == "relu2":
        a = jnp.square(jnp.maximum(a, 0.0))
    o_ref[...] = a.astype(o_ref.dtype)


def _norm_matmul(x, g, w, *, act, out_dtype, layer=0, tm_cap=1024, tn_cap=1536):
    m, k = x.shape
    n = _w_cols(w)
    tm = _tile(m, tm_cap, 256)
    tn = w.shape[3] if w.ndim == 4 else _tile(n, tn_cap, 256)
    return pl.pallas_call(
        functools.partial(_norm_mm_kernel, act=act),
        grid=(m // tm, n // tn),
        in_specs=[
            pl.BlockSpec((tm, k), lambda i, j: (i, 0)),
            pl.BlockSpec((1, k), lambda i, j: (0, 0)),
            _w_spec(w, layer, k, tn),
        ],
        out_specs=pl.BlockSpec((tm, tn), lambda i, j: (i, j)),
        out_shape=jax.ShapeDtypeStruct((m, n), out_dtype),
        scratch_shapes=[pltpu.VMEM((tm, k), BF16)],
        compiler_params=_cparams("parallel", "arbitrary"),
        name="norm_matmul",
    )(x, g.reshape(1, k), w)


def _mm_res_kernel(*refs, n_x):
    xs, ws = refs[:n_x], refs[n_x:2 * n_x]
    res_ref, o_ref = refs[2 * n_x], refs[2 * n_x + 1]
    acc = res_ref[...]
    for x_ref, w_ref in zip(xs, ws):
        acc = acc + _dot(x_ref[...], w_ref[...])
    o_ref[...] = acc


def _matmul_residual(xs, w, res, *, layer=0, tm_cap=512, tn_cap=512):
    m, n = res.shape
    assert n == _w_cols(w)
    tm = _tile(m, tm_cap, 256)
    tn = w.shape[3] if w.ndim == 4 else _tile(n, tn_cap, 256)
    in_specs, row = [], 0
    for x in xs:
        in_specs.append(pl.BlockSpec((tm, x.shape[1]), lambda i, j: (i, 0)))
    w_specs = []
    for x in xs:
        kk = x.shape[1]
        assert row % kk == 0
        w_specs.append(_w_spec(w, layer, kk, tn, row_block=row // kk))
        row += kk
    assert row == w.shape[-2]
    return pl.pallas_call(
        functools.partial(_mm_res_kernel, n_x=len(xs)),
        grid=(m // tm, n // tn),
        in_specs=in_specs + w_specs + [pl.BlockSpec((tm, tn), lambda i, j: (i, j))],
        out_specs=pl.BlockSpec((tm, tn), lambda i, j: (i, j)),
        out_shape=jax.ShapeDtypeStruct((m, n), F32),
        compiler_params=_cparams("parallel", "parallel"),
        name="matmul_residual",
    )(*xs, *([w] * len(xs)), res)


def _mm_res_ksplit_kernel(x_ref, w_ref, res_ref, o_ref):
    @pl.when(pl.program_id(1) == 0)
    def _():
        o_ref[...] = res_ref[...]

    o_ref[...] += _dot(x_ref[...], w_ref[...])


def _matmul_residual_ksplit(x, w, res, *, layer, tm_cap=1024, tk_cap=1024):
    m, n = res.shape
    k = x.shape[1]
    tm = _tile(m, tm_cap, 256)
    tk = _tile(k, tk_cap, 256)
    return pl.pallas_call(
        _mm_res_ksplit_kernel,
        grid=(m // tm, k // tk),
        in_specs=[
            pl.BlockSpec((tm, tk), lambda i, kk: (i, kk)),
            pl.BlockSpec((None, tk, n), lambda i, kk: (layer, kk, 0)),
            pl.BlockSpec((tm, n), lambda i, kk: (i, 0)),
        ],
        out_specs=pl.BlockSpec((tm, n), lambda i, kk: (i, 0)),
        out_shape=jax.ShapeDtypeStruct((m, n), F32),
        compiler_params=_cparams("parallel", "arbitrary"),
        name="matmul_residual_ksplit",
    )(x, w, res)


def _ple_kernel(x_ref, g_ref, wg_ref, p_ref, wp_ref, gf_ref, o_ref, *, final_norm):
    d = x_ref.shape[1]
    hn = _rms(x_ref[...], g_ref[...]).astype(BF16)
    pb = p_ref[...].astype(BF16)
    nh = d // 2 if d % (2 * V7X_LANES) == 0 else d
    for c0 in range(0, d, nh):
        gate = jax.nn.sigmoid(_dot(hn, wg_ref[:, c0:c0 + nh]))
        pp = _dot(pb, wp_ref[:, c0:c0 + nh])
        o_ref[:, c0:c0 + nh] = x_ref[:, c0:c0 + nh] + gate * pp
    if final_norm:
        o_ref[...] = _rms(o_ref[...], gf_ref[...])


def _ple(x, g, wg, p, wp, gf, *, layer, final_norm, tm_cap=512):
    m, d = x.shape
    pd = p.shape[2]
    tm = _tile(m, tm_cap, 256)
    return pl.pallas_call(
        functools.partial(_ple_kernel, final_norm=final_norm),
        grid=(m // tm,),
        in_specs=[
            pl.BlockSpec((tm, d), lambda i: (i, 0)),
            pl.BlockSpec((1, d), lambda i: (0, 0)),
            pl.BlockSpec((None, d, d), lambda i: (layer, 0, 0)),
            pl.BlockSpec((None, tm, pd), lambda i: (layer, i, 0)),
            pl.BlockSpec((None, pd, d), lambda i: (layer, 0, 0)),
            pl.BlockSpec((1, d), lambda i: (0, 0)),
        ],
        out_specs=pl.BlockSpec((tm, d), lambda i: (i, 0)),
        out_shape=jax.ShapeDtypeStruct((m, d), F32),
        compiler_params=_cparams("parallel"),
        name="ple",
    )(x, g.reshape(1, d), wg, p, wp, gf.reshape(1, d))


def _causal_conv(x_ref, halo_ref, w_ref, first):
    x = x_ref[0]
    halo = jnp.where(first, 0.0, halo_ref[0])
    xp = jnp.concatenate([halo, x], axis=0)
    w = w_ref[...]
    y = pltpu.roll(xp, 3, axis=0)[V7X_SUBLANES:] * w[0:1]
    y = y + pltpu.roll(xp, 2, axis=0)[V7X_SUBLANES:] * w[1:2]
    y = y + pltpu.roll(xp, 1, axis=0)[V7X_SUBLANES:] * w[2:3]
    return y + x * w[3:4]


def _halo_map(col, rows_per_tile):
    nb = rows_per_tile // V7X_SUBLANES
    return lambda b, h, t: (b, jnp.maximum(t * nb - 1, 0), col(h))


def _unit_lower_inverses(ms, row, col):
    c = ms[0].shape[0]
    eye = (row == col).astype(F32)
    diag16 = (row >> 4) == (col >> 4)
    b16 = lambda xs: [x.astype(BF16) for x in xs]
    ns = [jnp.where(diag16, m, 0.0) for m in ms]
    invs = [eye - n for n in ns]
    qs = b16(ns)
    for step in range(3):
        qs = b16([_dot(q, q) for q in qs])
        invs = [inv + _dot(inv.astype(BF16), q) for inv, q in zip(invs, qs)]
    shift = 4
    while (1 << shift) < c:
        off = ((row >> (shift + 1)) == (col >> (shift + 1))) & ((row >> shift) != (col >> shift))
        inv16 = b16(invs)
        tmps = [_dot(inv, jnp.where(off, m, 0.0).astype(BF16)) for inv, m in zip(inv16, ms)]
        invs = [inv - _dot(tmp.astype(BF16), i16) for inv, tmp, i16 in zip(invs, tmps, inv16)]
        shift += 1
    xs = [_split_bf16(inv) for inv in invs]
    res = [eye - inv - _dot_f32(m, x) for inv, m, x in zip(invs, ms, xs)]
    return [inv + _dot_f32(x, r) for inv, x, r in zip(invs, xs, res)]


def _delta_kernel(q_ref, qh_ref, k_ref, kh_ref, v_ref, vh_ref, z_ref, ba_ref,
                  wq_ref, wk_ref, wv_ref, alog_ref, dtb_ref, nw_ref, o_ref, s_ref):
    hp = pl.program_id(1)
    first = pl.program_id(2) == 0
    tt = q_ref.shape[1]
    c = DN_CHUNK
    dh = V7X_LANES
    nheads = q_ref.shape[2] // dh
    nch = tt // c

    @pl.when(first)
    def _():
        s_ref[...] = jnp.zeros_like(s_ref)

    def l2n(x):
        return x * lax.rsqrt(jnp.sum(x * x, axis=-1, keepdims=True) + 1e-6)

    q_all = _silu(_causal_conv(q_ref, qh_ref, wq_ref, first))
    k_all = _silu(_causal_conv(k_ref, kh_ref, wk_ref, first))
    v_all = _silu(_causal_conv(v_ref, vh_ref, wv_ref, first))
    ba = ba_ref[0]
    lane = lax.broadcasted_iota(jnp.int32, ba.shape, 1)
    sig_ba = jax.nn.sigmoid(ba)
    g_all = -jnp.exp(alog_ref[...]) * jax.nn.softplus(ba + dtb_ref[...])

    row = lax.broadcasted_iota(jnp.int32, (c, c), 0)
    col = lax.broadcasted_iota(jnp.int32, (c, c), 1)
    causal = row >= col
    strict = row > col
    ltri = causal.astype(BF16)

    def cumsum_block(gblk):
        g_hi, g_lo = _split_bf16(gblk)
        g_lo2 = (gblk - g_hi.astype(F32) - g_lo.astype(F32)).astype(BF16)
        return _dot(ltri, g_hi) + (_dot(ltri, g_lo) + _dot(ltri, g_lo2))

    gc_blocks = [cumsum_block(g_all[ci * c:(ci + 1) * c]) for ci in range(nch)]
    gc_blocks_t = [gcb.T for gcb in gc_blocks]

    qs, ks, vs, bs, gcs, gc_rows = [], [], [], [], [], []
    for hi in range(nheads):
        hs = slice(hi * dh, (hi + 1) * dh)
        head = nheads * hp + hi
        qh = l2n(q_all[:, hs]) * (dh ** -0.5)
        kh = l2n(k_all[:, hs])
        vh = v_all[:, hs]
        beta = jnp.sum(jnp.where(lane == head, sig_ba, 0.0), axis=-1, keepdims=True)
        for ci in range(nch):
            sl = slice(ci * c, (ci + 1) * c)
            qs.append(qh[sl]); ks.append(kh[sl]); vs.append(vh[sl]); bs.append(beta[sl])
            gcol = jnp.sum(jnp.where(col == head + DN_HEADS, gc_blocks[ci], 0.0), axis=-1, keepdims=True)
            grow = jnp.sum(jnp.where(row == head + DN_HEADS, gc_blocks_t[ci], 0.0), axis=0, keepdims=True)
            gcs.append(jnp.broadcast_to(gcol, (c, c)))
            gc_rows.append(jnp.broadcast_to(grow, (c, c)))

    decays = [jnp.where(causal, jnp.exp(jnp.where(causal, gc - gr, 0.0)), 0.0) for gc, gr in zip(gcs, gc_rows)]
    kbs = [kc * bc for kc, bc in zip(ks, bs)]
    k16 = [kc.astype(BF16) for kc in ks]
    ms = [jnp.where(strict, _dot_nt(kb.astype(BF16), kc) * dec, 0.0) for kb, kc, dec in zip(kbs, k16, decays)]
    invs = _unit_lower_inverses(ms, row, col)
    egcs = [jnp.exp(gc) for gc in gcs]
    sols = [_dot_f32(inv, jnp.concatenate([vc * bc, kb * egc], axis=1))
            for inv, vc, bc, kb, egc in zip(invs, vs, bs, kbs, egcs)]
    us = [sol[:, :dh] for sol in sols]
    ws = [sol[:, dh:].astype(BF16) for sol in sols]
    qks = [(_dot_nt(qc.astype(BF16), kc) * dec).astype(BF16) for qc, kc, dec in zip(qs, k16, decays)]
    q_decs = [(qc * egc).astype(BF16) for qc, egc in zip(qs, egcs)]
    lasts = [gc[c - 1:c, :] for gc in gcs]
    k_decs = [(kc * jnp.exp(last - gc)).astype(BF16) for kc, last, gc in zip(ks, lasts, gcs)]
    g_tots = [jnp.exp(last) for last in lasts]

    states = [s_ref[hi] for hi in range(nheads)]
    for ci in range(nch):
        sl = slice(ci * c, (ci + 1) * c)
        for hi in range(nheads):
            i = hi * nch + ci
            hs = slice(hi * dh, (hi + 1) * dh)
            sb = states[hi].astype(BF16)
            v_new = (us[i] - _dot(ws[i], sb)).astype(BF16)
            o = _dot(q_decs[i], sb) + _dot(qks[i], v_new)
            states[hi] = states[hi] * g_tots[i] + _dot_tn(k_decs[i], v_new)
            o_ref[0, sl, hs] = (_rms(o, nw_ref[...]) * _silu(z_ref[0, sl, hs])).astype(o_ref.dtype)
    for hi in range(nheads):
        s_ref[hi] = states[hi]


def _delta_mixer(proj, conv_w, a_log, dt_bias, norm_w, *, tt_cap=512):
    b, t, _ = proj.shape
    tt = _tile(t, tt_cap, DN_CHUNK)
    hh = DN_HEADS
    hps = DN_HEADS_PER_STEP
    wd = hps * V7X_LANES
    npair = hh // hps
    blk = lambda off: pl.BlockSpec((1, tt, wd), lambda bi, h, ti, off=off: (bi, ti, off + h))
    halo = lambda off: pl.BlockSpec((1, V7X_SUBLANES, wd), _halo_map(lambda h, off=off: off + h, tt))
    cw = lambda off: pl.BlockSpec((4, wd), lambda bi, h, ti, off=off: (0, off + h))
    row_spec = pl.BlockSpec((1, V7X_LANES), lambda bi, h, ti: (0, 0))
    pad = jnp.zeros((V7X_LANES - 2 * hh,), F32)
    alog_row = jnp.concatenate([jnp.zeros((hh,), F32), a_log, pad]).reshape(1, V7X_LANES)
    dtb_row = jnp.concatenate([jnp.zeros((hh,), F32), dt_bias, pad]).reshape(1, V7X_LANES)
    ba_col = 6 * hh
    return pl.pallas_call(
        _delta_kernel,
        grid=(b, npair, t // tt),
        in_specs=[
            blk(0), halo(0), blk(npair), halo(npair), blk(2 * npair), halo(2 * npair), blk(3 * npair),
            pl.BlockSpec((1, tt, V7X_LANES), lambda bi, h, ti: (bi, ti, ba_col)),
            cw(0), cw(npair), cw(2 * npair), row_spec, row_spec, row_spec,
        ],
        out_specs=pl.BlockSpec((1, tt, wd), lambda bi, h, ti: (bi, ti, h)),
        out_shape=jax.ShapeDtypeStruct((b, t, hh * V7X_LANES), BF16),
        scratch_shapes=[pltpu.VMEM((hps, V7X_LANES, V7X_LANES), F32)],
        compiler_params=_cparams("parallel", "parallel", "arbitrary"),
        name="delta_mixer",
    )(proj, proj, proj, proj, proj, proj, proj, proj, conv_w, conv_w, conv_w,
      alog_row, dtb_row, norm_w.reshape(1, V7X_LANES))


def _lru_kernel(x_ref, xh_ref, y_ref, cw_ref, cb_ref, wa_ref, ba_ref, wx_ref, bx_ref, lam_ref,
                o_ref, h_ref):
    first = pl.program_id(2) == 0
    tt = x_ref.shape[1]

    @pl.when(first)
    def _():
        h_ref[...] = jnp.zeros_like(h_ref)

    xc = _causal_conv(x_ref, xh_ref, cw_ref, first) + cb_ref[...]
    xb = xc.astype(BF16)
    r = jax.nn.sigmoid(_dot(xb, wa_ref[0]) + ba_ref[...])
    i = jax.nn.sigmoid(_dot(xb, wx_ref[0]) + bx_ref[...])
    log_a = -LRU_C * r * jax.nn.softplus(-lam_ref[...])
    a = jnp.exp(log_a)
    u = jnp.sqrt(1.0 - a * a) * (i * xc)

    rowi = lax.broadcasted_iota(jnp.int32, (tt, V7X_LANES), 0)
    s = 1
    while s < tt:
        if s < V7X_SUBLANES:
            a_sh = jnp.where(rowi >= s, pltpu.roll(a, s, axis=0), 1.0)
            u_sh = jnp.where(rowi >= s, pltpu.roll(u, s, axis=0), 0.0)
        else:
            a_sh = jnp.concatenate([jnp.ones((s, V7X_LANES), F32), a[:tt - s]], axis=0)
            u_sh = jnp.concatenate([jnp.zeros((s, V7X_LANES), F32), u[:tt - s]], axis=0)
        u = a * u_sh + u
        a = a * a_sh
        s *= 2
    hs = u + a * h_ref[...]
    h_ref[...] = hs[tt - 1:tt]
    o_ref[0] = (hs * jax.nn.gelu(y_ref[0])).astype(o_ref.dtype)


def _lru_mixer(proj, conv_w, conv_b, wa, ba, wx, bx, lam, *, x_col, y_col, tt_cap=512):
    b, t, _ = proj.shape
    tt = _tile(t, tt_cap, 16)
    gg = LRU_BLOCKS
    vec = lambda a: a.reshape(1, gg * V7X_LANES)
    vspec = pl.BlockSpec((1, V7X_LANES), lambda bi, g, ti: (0, g))
    wspec = pl.BlockSpec((1, V7X_LANES, V7X_LANES), lambda bi, g, ti: (g, 0, 0))
    return pl.pallas_call(
        _lru_kernel,
        grid=(b, gg, t // tt),
        in_specs=[
            pl.BlockSpec((1, tt, V7X_LANES), lambda bi, g, ti: (bi, ti, x_col + g)),
            pl.BlockSpec((1, V7X_SUBLANES, V7X_LANES), _halo_map(lambda g: x_col + g, tt)),
            pl.BlockSpec((1, tt, V7X_LANES), lambda bi, g, ti: (bi, ti, y_col + g)),
            pl.BlockSpec((4, V7X_LANES), lambda bi, g, ti: (0, g)),
            vspec, wspec, vspec, wspec, vspec, vspec,
        ],
        out_specs=pl.BlockSpec((1, tt, V7X_LANES), lambda bi, g, ti: (bi, ti, g)),
        out_shape=jax.ShapeDtypeStruct((b, t, gg * V7X_LANES), BF16),
        scratch_shapes=[pltpu.VMEM((1, V7X_LANES), F32)],
        compiler_params=_cparams("parallel", "parallel", "arbitrary"),
        name="lru_mixer",
    )(proj, proj, proj, conv_w, vec(conv_b), wa.astype(BF16), vec(ba), wx.astype(BF16), vec(bx), vec(lam))


def _dilated_kernel(q_ref, kc_ref, kp_ref, vc_ref, vp_ref, o_ref, qa_ref, ka_ref, va_ref, sa_ref, sn_ref):
    h = pl.program_id(1)
    has_prev = pl.program_id(2) > 0
    tq = q_ref.shape[1]
    bq = SWA_BLOCK
    d1, d2, d3 = SWA_DILATIONS
    assert d1 == 1 and d3 % d2 == 0
    qq = tq // d2
    st = d3 // d2
    for r in range(d2):
        qa_ref[r * qq:(r + 1) * qq, :] = q_ref.at[0][pl.ds(r, qq, stride=d2), :]
        for src_p, src_c, dst in ((kp_ref, kc_ref, ka_ref), (vp_ref, vc_ref, va_ref)):
            dst[2 * r * qq:2 * r * qq + qq, :] = src_p.at[0][pl.ds(r, qq, stride=d2), :]
            dst[2 * r * qq + qq:2 * (r + 1) * qq, :] = src_c.at[0][pl.ds(r, qq, stride=d2), :]

    slope = jnp.exp2(-(jnp.full((1, 1), h, jnp.int32).astype(F32) + 1.0) * (8.0 / SWA_HEADS))
    iq = lax.broadcasted_iota(jnp.int32, (bq, 2 * bq), 0)
    ik = lax.broadcasted_iota(jnp.int32, (bq, 2 * bq), 1)
    rel = bq + iq - ik
    in_window = (rel >= 0) & (rel <= SWA_SPAN)
    relf = rel.astype(F32)
    scale = V7X_LANES ** -0.5
    ones_blk = jnp.ones((2 * bq, V7X_LANES), BF16)

    def attend(q_rows, kcat, vcat, d, prev_ok):
        qb = (q_rows * scale).astype(BF16)
        s = _dot_nt(qb, kcat.astype(BF16)) - (slope * float(d)) * relf
        s = jnp.where(in_window & ((ik >= bq) | prev_ok), s, MASK_VALUE)
        m_b = jnp.max(s, axis=-1, keepdims=True)
        p = jnp.exp(s - m_b).astype(BF16)
        pv = _dot(p, jnp.concatenate([vcat.astype(BF16), ones_blk], axis=1))
        return jnp.broadcast_to(m_b, (bq, V7X_LANES)), pv[:, V7X_LANES:], pv[:, :V7X_LANES]

    def merge(old, new):
        (m_o, l_o, acc_o), (m_b, l_b, acc_b) = old, new
        m_n = jnp.maximum(m_o, m_b)
        alpha = jnp.exp(m_o - m_n)
        beta = jnp.exp(m_b - m_n)
        return m_n, alpha * l_o + beta * l_b, alpha * acc_o + beta * acc_b

    nblk2 = qq // bq
    assert nblk2 & (nblk2 - 1) == 0

    def body2(idx, carry):
        r = idx >> (nblk2.bit_length() - 1)
        n = idx & (nblk2 - 1)
        q0 = pl.multiple_of(r * qq + n * bq, bq)
        k0 = pl.multiple_of(2 * r * qq + qq + (n - 1) * bq, bq)
        new = attend(qa_ref[pl.ds(q0, bq), :], ka_ref[pl.ds(k0, 2 * bq), :], va_ref[pl.ds(k0, 2 * bq), :],
                     d2, (n > 0) | has_prev)
        for kk in range(3):
            sa_ref[kk, pl.ds(q0, bq), :] = new[kk]
        return carry

    lax.fori_loop(0, d2 * nblk2, body2, 0, unroll=SWA_UNROLL)

    nblk3 = tq // (bq * d3)
    assert nblk3 >= 1 and nblk3 & (nblk3 - 1) == 0

    def body3(idx, carry):
        r3 = idx >> (nblk3.bit_length() - 1)
        n = idx & (nblk3 - 1)
        r = r3 & (d2 - 1)
        p0 = (r3 >> (d2.bit_length() - 1)) + n * (bq * st)
        q0 = r * qq + p0
        k0 = 2 * r * qq + qq + p0
        take = lambda ref, start: ref[pl.ds(start, bq, stride=st), :]
        kcat = jnp.concatenate([take(ka_ref, k0 - bq * st), take(ka_ref, k0)], axis=0)
        vcat = jnp.concatenate([take(va_ref, k0 - bq * st), take(va_ref, k0)], axis=0)
        new = attend(take(qa_ref, q0), kcat, vcat, d3, (n > 0) | has_prev)
        out = merge(tuple(take(sa_ref.at[kk], q0) for kk in range(3)), new)
        for kk in range(3):
            sa_ref.at[kk][pl.ds(q0, bq, stride=st), :] = out[kk]
        return carry

    assert d2 & (d2 - 1) == 0
    lax.fori_loop(0, d3 * nblk3, body3, 0, unroll=SWA_UNROLL)

    for r in range(d2):
        for kk in range(3):
            sn_ref.at[kk][pl.ds(r, qq, stride=d2), :] = sa_ref[kk, r * qq:(r + 1) * qq, :]

    def finish(base, kcat, vcat, prev_ok):
        new = attend(q_ref[0, pl.ds(base, bq), :], kcat, vcat, d1, prev_ok)
        _, l_f, acc_f = merge(tuple(sn_ref[kk, pl.ds(base, bq), :] for kk in range(3)), new)
        o_ref[0, pl.ds(base, bq), :] = (acc_f / l_f).astype(o_ref.dtype)

    finish(0, jnp.concatenate([kp_ref[0, tq - bq:tq, :], kc_ref[0, 0:bq, :]], axis=0),
           jnp.concatenate([vp_ref[0, tq - bq:tq, :], vc_ref[0, 0:bq, :]], axis=0), has_prev)

    def body1(n, carry):
        base = pl.multiple_of(n * bq, bq)
        k0 = pl.multiple_of(base - bq, bq)
        finish(base, kc_ref[0, pl.ds(k0, 2 * bq), :], vc_ref[0, pl.ds(k0, 2 * bq), :], True)
        return carry

    nblk1 = tq // bq
    lax.fori_loop(1, nblk1, body1, 0, unroll=SWA_UNROLL_D1)


def _dilated_mixer(proj, *, q_col, k_col, v_col, tq=2048):
    b, t, _ = proj.shape
    assert t % tq == 0 and tq % (SWA_BLOCK * max(SWA_DILATIONS)) == 0
    hh = SWA_HEADS
    cur = lambda off: pl.BlockSpec((1, tq, V7X_LANES), lambda bi, h, ti, off=off: (bi, ti, off + h))
    prev = lambda off: pl.BlockSpec((1, tq, V7X_LANES),
                                    lambda bi, h, ti, off=off: (bi, jnp.maximum(ti - 1, 0), off + h))
    return pl.pallas_call(
        _dilated_kernel,
        grid=(b, hh, t // tq),
        in_specs=[cur(q_col), cur(k_col), prev(k_col), cur(v_col), prev(v_col)],
        out_specs=pl.BlockSpec((1, tq, V7X_LANES), lambda bi, h, ti: (bi, ti, h)),
        out_shape=jax.ShapeDtypeStruct((b, t, hh * V7X_LANES), BF16),
        scratch_shapes=[pltpu.VMEM((tq, V7X_LANES), F32), pltpu.VMEM((2 * tq, V7X_LANES), F32),
                        pltpu.VMEM((2 * tq, V7X_LANES), F32), pltpu.VMEM((3, tq, V7X_LANES), F32),
                        pltpu.VMEM((3, tq, V7X_LANES), F32)],
        compiler_params=_cparams("parallel", "parallel", "parallel"),
        name="dilated_mixer",
    )(proj, proj, proj, proj, proj)


def _retention_kernel(q_ref, k_ref, v_ref, g_ref, o_ref, s_ref):
    h = pl.program_id(1)
    tt = q_ref.shape[1]
    c = min(RET_CHUNK, tt)
    nch = tt // c

    @pl.when(pl.program_id(2) == 0)
    def _():
        s_ref[...] = jnp.zeros_like(s_ref)

    hf = jnp.full((1, 1), h, jnp.int32).astype(F32)
    log_gamma = jnp.log1p(-jnp.exp2(-5.0 - hf))
    row = lax.broadcasted_iota(jnp.int32, (c, c), 0)
    col = lax.broadcasted_iota(jnp.int32, (c, c), 1)
    rel = (row - col).astype(F32)
    dmask = jnp.where(rel >= 0, jnp.exp(jnp.maximum(rel, 0.0) * log_gamma), 0.0)
    idx = lax.broadcasted_iota(jnp.int32, (c, 1), 0).astype(F32)
    q_scale = jnp.exp((idx + 1.0) * log_gamma)
    k_scale = jnp.exp((c - 1.0 - idx) * log_gamma)
    chunk_decay = jnp.exp(float(c) * log_gamma)

    sls = [slice(ci * c, (ci + 1) * c) for ci in range(nch)]
    qs = [q_ref[0, sl, :] for sl in sls]
    ks = [k_ref[0, sl, :] * (V7X_LANES ** -0.5) for sl in sls]
    vs = [v_ref[0, sl, :].astype(BF16) for sl in sls]
    intras = [_dot((_dot_nt(q.astype(BF16), k.astype(BF16)) * dmask).astype(BF16), v)
              for q, k, v in zip(qs, ks, vs)]
    kvs = [_dot_tn((k * k_scale).astype(BF16), v) for k, v in zip(ks, vs)]
    states = [s_ref[...]]
    for kv in kvs:
        states.append(states[-1] * chunk_decay + kv)
    s_ref[...] = states[-1]
    for sl, q, intra, state in zip(sls, qs, intras, states):
        o = intra + _dot((q * q_scale).astype(BF16), state.astype(BF16))
        mu = jnp.mean(o, axis=-1, keepdims=True)
        oc = o - mu
        o = oc * lax.rsqrt(jnp.mean(oc * oc, axis=-1, keepdims=True) + GN_EPS)
        o_ref[0, sl, :] = (o * _silu(g_ref[0, sl, :])).astype(o_ref.dtype)


def _retention_mixer(proj, *, q_col, k_col, v_col, g_col):
    b, t, _ = proj.shape
    c = _tile(t, RET_TILE, RET_CHUNK)
    hh = RET_HEADS
    wide = RET_DV // V7X_LANES
    assert v_col % wide == 0 and g_col % wide == 0
    nar = lambda off: pl.BlockSpec((1, c, V7X_LANES), lambda bi, h, ti, off=off: (bi, ti, off + h))
    wid = lambda off: pl.BlockSpec((1, c, RET_DV), lambda bi, h, ti, off=off: (bi, ti, off // wide + h))
    return pl.pallas_call(
        _retention_kernel,
        grid=(b, hh, t // c),
        in_specs=[nar(q_col), nar(k_col), wid(v_col), wid(g_col)],
        out_specs=pl.BlockSpec((1, c, RET_DV), lambda bi, h, ti: (bi, ti, h)),
        out_shape=jax.ShapeDtypeStruct((b, t, hh * RET_DV), BF16),
        scratch_shapes=[pltpu.VMEM((V7X_LANES, RET_DV), F32)],
        compiler_params=_cparams("parallel", "parallel", "arbitrary"),
        name="retention_mixer",
    )(proj, proj, proj, proj)


def _even_w_in_layout(w):
    d = w.shape[0]
    qkvz = 4 * DN_HEADS * V7X_LANES
    nba = 2 * DN_HEADS
    pad = jnp.zeros((d, 2 * V7X_LANES - nba), w.dtype)
    return jnp.concatenate([w[:, :qkvz], w[:, qkvz + nba:], w[:, qkvz:qkvz + nba], pad], axis=1)


def kernel(x, p, ln_mix_w, ln_mlp_w, ln_ple_w, w_up, w_down, w_ple_proj, w_ple_gate, ln_final_w,
           ev_w_in, ev_w_out, dn_conv_w, dn_a_log, dn_dt_bias, dn_norm_w,
           lru_conv_w, lru_conv_b, lru_wa, lru_ba, lru_wx, lru_bx, lru_lambda,
           od_w_in, od_w_out):
    b, t, d = x.shape
    depth = ln_mix_w.shape[0]
    m = b * t
    h = x.reshape(m, d)
    w_ple_proj, w_ple_gate, ev_w_in, ev_w_out, od_w_in, od_w_out = (
        _to_bf16(w) for w in (w_ple_proj, w_ple_gate, ev_w_in, ev_w_out, od_w_in, od_w_out))
    w_up = _to_bf16(w_up, tn=MLP_UP_TN)
    w_down = _to_bf16(w_down)
    p = p.reshape(depth, m, -1)
    for i in range(depth):
        j = i // 2
        if i % 2 == 0:
            w_in = _even_w_in_layout(ev_w_in[j])
            proj = _norm_matmul(h, ln_mix_w[i], w_in, act=None, out_dtype=F32)
            proj = proj.reshape(b, t, -1)
            y_a = _delta_mixer(proj, dn_conv_w[j], dn_a_log[j], dn_dt_bias[j], dn_norm_w[j])
            y_b = _lru_mixer(proj, lru_conv_w[j], lru_conv_b[j], lru_wa[j], lru_ba[j], lru_wx[j], lru_bx[j],
                             lru_lambda[j], x_col=4 * DN_HEADS, y_col=4 * DN_HEADS + LRU_BLOCKS)
            w_out = ev_w_out
        else:
            proj = _norm_matmul(h, ln_mix_w[i], od_w_in, layer=j, act=None, out_dtype=F32)
            proj = proj.reshape(b, t, -1)
            y_a = _dilated_mixer(proj, q_col=0, k_col=SWA_HEADS, v_col=2 * SWA_HEADS)
            y_b = _retention_mixer(proj, q_col=3 * SWA_HEADS, k_col=3 * SWA_HEADS + RET_HEADS,
                                   v_col=3 * SWA_HEADS + 2 * RET_HEADS,
                                   g_col=3 * SWA_HEADS + 2 * RET_HEADS + RET_HEADS * RET_DV // V7X_LANES)
            w_out = od_w_out
        h = _matmul_residual([y_a.reshape(m, -1), y_b.reshape(m, -1)], w_out, h, layer=j, tn_cap=d)
        up = _norm_matmul(h, ln_mlp_w[i], w_up, layer=i, act="relu2", out_dtype=BF16)
        h = _matmul_residual_ksplit(up, w_down, h, layer=i)
        h = _ple(h, ln_ple_w[i], w_ple_gate, p, w_ple_proj, ln_final_w, layer=i, final_norm=(i == depth - 1))
    return h.reshape(b, t, d)
```

```python
import functools

import jax
import jax.numpy as jnp
from jax import lax
from jax.experimental import pallas as pl
from jax.experimental.pallas import tpu as pltpu

F32 = jnp.float32
BF16 = jnp.bfloat16

V7X_LANES = 128
V7X_SUBLANES = 8
V7X_VMEM_LIMIT_BYTES = 56 * 1024 * 1024

NORM_EPS = 1e-6
GN_EPS = 1e-5
LRU_C = 8.0
DN_HEADS = 8
LRU_BLOCKS = 8
SWA_HEADS = 8
RET_HEADS = 4
RET_DV = 256
SWA_DILATIONS = (1, 4, 16)
SWA_SPAN = 128
SWA_BLOCK = 128
SWA_UNROLL_D1 = 5
SWA_UNROLL = 8
DN_CHUNK = 128
DN_HEADS_PER_STEP = 4
RET_CHUNK = 256
RET_TILE = 1024
MASK_VALUE = -1e30
MLP_UP_TN = 2048
MLP_UP_TM = 1024


def _cparams(*sem):
    return pltpu.CompilerParams(dimension_semantics=sem, vmem_limit_bytes=V7X_VMEM_LIMIT_BYTES)


def _dot(a, b):
    return jnp.dot(a, b, preferred_element_type=F32)


def _dot_nt(a, b):
    return lax.dot_general(a, b, (((1,), (1,)), ((), ())), preferred_element_type=F32)


def _dot_tn(a, b):
    return lax.dot_general(a, b, (((0,), (0,)), ((), ())), preferred_element_type=F32)


def _split_bf16(x):
    hi = x.astype(BF16)
    lo = (x - hi.astype(F32)).astype(BF16)
    return hi, lo


def _dot_f32(a, b):
    ah, al = a if isinstance(a, tuple) else _split_bf16(a)
    bh, bl = b if isinstance(b, tuple) else _split_bf16(b)
    n = bh.shape[1]
    wide = _dot(ah, jnp.concatenate([bh, bl], axis=1))
    return wide[:, :n] + (wide[:, n:] + _dot(al, bh))


def _rms(x, w):
    return x * lax.rsqrt(jnp.mean(x * x, axis=-1, keepdims=True) + NORM_EPS) * w


def _silu(x):
    return x * jax.nn.sigmoid(x)


def _tile(n, cap, quantum):
    if n <= cap:
        return n
    best = None
    for c in range(quantum, cap + 1, quantum):
        if n % c == 0:
            best = c
    assert best is not None, (n, cap, quantum)
    return best


CAST_BLOCK_BYTES = 8 * 1024 * 1024


def _cast_kernel(x_ref, o_ref):
    o_ref[...] = x_ref[...].astype(o_ref.dtype)


def _to_bf16(w, tn=None):
    l, r, c = w.shape
    tn = c if tn is None else tn
    tr = _tile(r, max(16, CAST_BLOCK_BYTES // (4 * tn) // 16 * 16), 16)
    out = pl.pallas_call(
        _cast_kernel,
        grid=(l, c // tn, r // tr),
        in_specs=[pl.BlockSpec((None, tr, tn), lambda i, j, k: (i, k, j))],
        out_specs=pl.BlockSpec((None, None, tr, tn), lambda i, j, k: (i, j, k, 0)),
        out_shape=jax.ShapeDtypeStruct((l, c // tn, r, tn), BF16),
        compiler_params=_cparams("parallel", "parallel", "parallel"),
        name="to_bf16",
    )(w)
    return out if tn != c else out.reshape(l, r, c)


def _w_spec(w, layer, kk, tn, row_block=0):
    if w.ndim == 2:
        return pl.BlockSpec((kk, tn), lambda i, j: (row_block, j))
    if w.ndim == 3:
        return pl.BlockSpec((None, kk, tn), lambda i, j: (layer, row_block, j))
    assert w.shape[3] == tn, (w.shape, tn)
    return pl.BlockSpec((None, None, kk, tn), lambda i, j: (layer, j, row_block, 0))


def _w_cols(w):
    return w.shape[-1] if w.ndim < 4 else w.shape[1] * w.shape[3]


def _norm_mm_kernel(x_ref, g_ref, w_ref, o_ref, hn_ref, *, act):
    @pl.when(pl.program_id(1) == 0)
    def _():
        hn_ref[...] = _rms(x_ref[...], g_ref[...]).astype(BF16)

    _mm_act_kernel(hn_ref, w_ref, o_ref, act=act)


def _mm_act_kernel(x_ref, w_ref, o_ref, *, act):
    a = _dot(x_ref[...], w_ref[...])
    if act == "relu2":
        a = jnp.square(jnp.maximum(a, 0.0))
    o_ref[...] = a.astype(o_ref.dtype)


def _norm_matmul(x, g, w, *, act, out_dtype, layer=0, tm_cap=1024, tn_cap=1536):
    m, k = x.shape
    n = _w_cols(w)
    tm = _tile(m, tm_cap, 256)
    tn = w.shape[3] if w.ndim == 4 else _tile(n, tn_cap, 256)
    x_spec = pl.BlockSpec((tm, k), lambda i, j: (i, 0))
    common = dict(
        grid=(m // tm, n // tn),
        out_specs=pl.BlockSpec((tm, tn), lambda i, j: (i, j)),
        out_shape=jax.ShapeDtypeStruct((m, n), out_dtype),
    )
    if g is None:
        assert x.dtype == BF16
        return pl.pallas_call(
            functools.partial(_mm_act_kernel, act=act),
            in_specs=[x_spec, _w_spec(w, layer, k, tn)],
            compiler_params=_cparams("parallel", "parallel"),
            name="matmul_act", **common,
        )(x, w)
    return pl.pallas_call(
        functools.partial(_norm_mm_kernel, act=act),
        in_specs=[x_spec, pl.BlockSpec((1, k), lambda i, j: (0, 0)), _w_spec(w, layer, k, tn)],
        scratch_shapes=[pltpu.VMEM((tm, k), BF16)],
        compiler_params=_cparams("parallel", "arbitrary"),
        name="norm_matmul", **common,
    )(x, g.reshape(1, k), w)


def _mm_res_kernel(*refs, n_x, with_norm):
    xs, ws = refs[:n_x], refs[n_x:2 * n_x]
    res_ref = refs[2 * n_x]
    outs = refs[2 * n_x + 1:]
    acc = res_ref[...]
    for x_ref, w_ref in zip(xs, ws):
        acc = acc + _dot(x_ref[...], w_ref[...])
    if with_norm:
        g_ref, o_ref, hn_ref = outs
        hn_ref[...] = _rms(acc, g_ref[...]).astype(BF16)
    else:
        o_ref, = outs
    o_ref[...] = acc


def _matmul_residual(xs, w, res, *, layer=0, next_norm_w=None, tm_cap=512, tn_cap=512):
    m, n = res.shape
    assert n == _w_cols(w)
    tm = _tile(m, tm_cap, 256)
    tn = w.shape[3] if w.ndim == 4 else _tile(n, tn_cap, 256)
    with_norm = next_norm_w is not None
    assert not with_norm or tn == n
    in_specs, row = [], 0
    for x in xs:
        in_specs.append(pl.BlockSpec((tm, x.shape[1]), lambda i, j: (i, 0)))
    w_specs = []
    for x in xs:
        kk = x.shape[1]
        assert row % kk == 0
        w_specs.append(_w_spec(w, layer, kk, tn, row_block=row // kk))
        row += kk
    assert row == w.shape[-2]
    tile = pl.BlockSpec((tm, tn), lambda i, j: (i, j))
    in_specs = in_specs + w_specs + [tile]
    args = [*xs, *([w] * len(xs)), res]
    out_specs, out_shape = tile, jax.ShapeDtypeStruct((m, n), F32)
    if with_norm:
        in_specs.append(pl.BlockSpec((1, n), lambda i, j: (0, 0)))
        args.append(next_norm_w.reshape(1, n))
        out_specs, out_shape = [tile, tile], [out_shape, jax.ShapeDtypeStruct((m, n), BF16)]
    return pl.pallas_call(
        functools.partial(_mm_res_kernel, n_x=len(xs), with_norm=with_norm),
        grid=(m // tm, n // tn),
        in_specs=in_specs,
        out_specs=out_specs,
        out_shape=out_shape,
        compiler_params=_cparams("parallel", "parallel"),
        name="matmul_residual",
    )(*args)


def _mm_res_ksplit_kernel(x_ref, w_ref, res_ref, o_ref):
    @pl.when(pl.program_id(1) == 0)
    def _():
        o_ref[...] = res_ref[...]

    o_ref[...] += _dot(x_ref[...], w_ref[...])


def _matmul_residual_ksplit(x, w, res, *, layer, tm_cap=1024, tk_cap=1024):
    m, n = res.shape
    k = x.shape[1]
    tm = _tile(m, tm_cap, 256)
    tk = _tile(k, tk_cap, 256)
    return pl.pallas_call(
        _mm_res_ksplit_kernel,
        grid=(m // tm, k // tk),
        in_specs=[
            pl.BlockSpec((tm, tk), lambda i, kk: (i, kk)),
            pl.BlockSpec((None, tk, n), lambda i, kk: (layer, kk, 0)),
            pl.BlockSpec((tm, n), lambda i, kk: (i, 0)),
        ],
        out_specs=pl.BlockSpec((tm, n), lambda i, kk: (i, 0)),
        out_shape=jax.ShapeDtypeStruct((m, n), F32),
        compiler_params=_cparams("parallel", "arbitrary"),
        name="matmul_residual_ksplit",
    )(x, w, res)


def _ple_kernel(x_ref, g_ref, wg_ref, p_ref, wp_ref, gt_ref, o_ref, *hn_refs, final_norm):
    d = x_ref.shape[1]
    hn = _rms(x_ref[...], g_ref[...]).astype(BF16)
    pb = p_ref[...].astype(BF16)
    nh = d // 2 if d % (2 * V7X_LANES) == 0 else d
    for c0 in range(0, d, nh):
        gate = jax.nn.sigmoid(_dot(hn, wg_ref[:, c0:c0 + nh]))
        pp = _dot(pb, wp_ref[:, c0:c0 + nh])
        o_ref[:, c0:c0 + nh] = x_ref[:, c0:c0 + nh] + gate * pp
    tail = _rms(o_ref[...], gt_ref[...])
    if final_norm:
        o_ref[...] = tail
    else:
        hn_refs[0][...] = tail.astype(BF16)


def _ple(x, g, wg, p, wp, g_tail, *, layer, final_norm, tm_cap=512):
    m, d = x.shape
    pd = p.shape[2]
    tm = _tile(m, tm_cap, 256)
    row_tile = pl.BlockSpec((tm, d), lambda i: (i, 0))
    out_specs, out_shape = row_tile, jax.ShapeDtypeStruct((m, d), F32)
    if not final_norm:
        out_specs, out_shape = [row_tile, row_tile], [out_shape, jax.ShapeDtypeStruct((m, d), BF16)]
    return pl.pallas_call(
        functools.partial(_ple_kernel, final_norm=final_norm),
        grid=(m // tm,),
        in_specs=[
            pl.BlockSpec((tm, d), lambda i: (i, 0)),
            pl.BlockSpec((1, d), lambda i: (0, 0)),
            pl.BlockSpec((None, d, d), lambda i: (layer, 0, 0)),
            pl.BlockSpec((None, tm, pd), lambda i: (layer, i, 0)),
            pl.BlockSpec((None, pd, d), lambda i: (layer, 0, 0)),
            pl.BlockSpec((1, d), lambda i: (0, 0)),
        ],
        out_specs=out_specs,
        out_shape=out_shape,
        compiler_params=_cparams("parallel"),
        name="ple",
    )(x, g.reshape(1, d), wg, p, wp, g_tail.reshape(1, d))


def _causal_conv(x_ref, halo_ref, w_ref, first):
    x = x_ref[0]
    halo = jnp.where(first, 0.0, halo_ref[0])
    xp = jnp.concatenate([halo, x], axis=0)
    w = w_ref[...]
    y = pltpu.roll(xp, 3, axis=0)[V7X_SUBLANES:] * w[0:1]
    y = y + pltpu.roll(xp, 2, axis=0)[V7X_SUBLANES:] * w[1:2]
    y = y + pltpu.roll(xp, 1, axis=0)[V7X_SUBLANES:] * w[2:3]
    return y + x * w[3:4]


def _halo_map(col, rows_per_tile):
    nb = rows_per_tile // V7X_SUBLANES
    return lambda b, h, t: (b, jnp.maximum(t * nb - 1, 0), col(h))


def _unit_lower_inverses(ms, row, col):
    c = ms[0].shape[0]
    eye = (row == col).astype(F32)
    diag16 = (row >> 4) == (col >> 4)
    b16 = lambda xs: [x.astype(BF16) for x in xs]
    ns = [jnp.where(diag16, m, 0.0) for m in ms]
    invs = [eye - n for n in ns]
    qs = b16(ns)
    for step in range(3):
        qs = b16([_dot(q, q) for q in qs])
        invs = [inv + _dot(inv.astype(BF16), q) for inv, q in zip(invs, qs)]
    shift = 4
    while (1 << shift) < c:
        off = ((row >> (shift + 1)) == (col >> (shift + 1))) & ((row >> shift) != (col >> shift))
        inv16 = b16(invs)
        tmps = [_dot(inv, jnp.where(off, m, 0.0).astype(BF16)) for inv, m in zip(inv16, ms)]
        invs = [inv - _dot(tmp.astype(BF16), i16) for inv, tmp, i16 in zip(invs, tmps, inv16)]
        shift += 1
    xs = [_split_bf16(inv) for inv in invs]
    res = [eye - inv - _dot_f32(m, x) for inv, m, x in zip(invs, ms, xs)]
    return [inv + _dot_f32(x, r) for inv, x, r in zip(invs, xs, res)]


def _delta_kernel(q_ref, qh_ref, k_ref, kh_ref, v_ref, vh_ref, z_ref, ba_ref,
                  wq_ref, wk_ref, wv_ref, alog_ref, dtb_ref, nw_ref, o_ref, s_ref):
    hp = pl.program_id(1)
    first = pl.program_id(2) == 0
    tt = q_ref.shape[1]
    c = DN_CHUNK
    dh = V7X_LANES
    nheads = q_ref.shape[2] // dh
    nch = tt // c

    @pl.when(first)
    def _():
        s_ref[...] = jnp.zeros_like(s_ref)

    def l2n(x):
        return x * lax.rsqrt(jnp.sum(x * x, axis=-1, keepdims=True) + 1e-6)

    ba = ba_ref[0]
    lane = lax.broadcasted_iota(jnp.int32, ba.shape, 1)
    sig_ba = jax.nn.sigmoid(ba)
    g_all = -jnp.exp(alog_ref[...]) * jax.nn.softplus(ba + dtb_ref[...])

    row = lax.broadcasted_iota(jnp.int32, (c, c), 0)
    col = lax.broadcasted_iota(jnp.int32, (c, c), 1)
    causal = row >= col
    strict = row > col
    ltri = causal.astype(BF16)

    def cumsum_block(gblk):
        g_hi, g_lo = _split_bf16(gblk)
        g_lo2 = (gblk - g_hi.astype(F32) - g_lo.astype(F32)).astype(BF16)
        return _dot(ltri, g_hi) + (_dot(ltri, g_lo) + _dot(ltri, g_lo2))

    gc_blocks = [cumsum_block(g_all[ci * c:(ci + 1) * c]) for ci in range(nch)]
    gc_blocks_t = [gcb.T for gcb in gc_blocks]

    q_all = _silu(_causal_conv(q_ref, qh_ref, wq_ref, first))
    k_all = _silu(_causal_conv(k_ref, kh_ref, wk_ref, first))
    v_all = _silu(_causal_conv(v_ref, vh_ref, wv_ref, first))
    qs, ks, vs, bs, gcs, gc_rows = [], [], [], [], [], []
    for hi in range(nheads):
        hs = slice(hi * dh, (hi + 1) * dh)
        head = nheads * hp + hi
        qh = l2n(q_all[:, hs]) * (dh ** -0.5)
        kh = l2n(k_all[:, hs])
        vh = v_all[:, hs]
        beta = jnp.sum(jnp.where(lane == head, sig_ba, 0.0), axis=-1, keepdims=True)
        for ci in range(nch):
            sl = slice(ci * c, (ci + 1) * c)
            qs.append(qh[sl]); ks.append(kh[sl]); vs.append(vh[sl]); bs.append(beta[sl])
            gcol = jnp.sum(jnp.where(col == head + DN_HEADS, gc_blocks[ci], 0.0), axis=-1, keepdims=True)
            grow = jnp.sum(jnp.where(row == head + DN_HEADS, gc_blocks_t[ci], 0.0), axis=0, keepdims=True)
            gcs.append(jnp.broadcast_to(gcol, (c, c)))
            gc_rows.append(jnp.broadcast_to(grow, (c, c)))

    decays = [jnp.where(causal, jnp.exp(jnp.where(causal, gc - gr, 0.0)), 0.0) for gc, gr in zip(gcs, gc_rows)]
    kbs = [kc * bc for kc, bc in zip(ks, bs)]
    k16 = [kc.astype(BF16) for kc in ks]
    ms = [jnp.where(strict, _dot_nt(kb.astype(BF16), kc) * dec, 0.0) for kb, kc, dec in zip(kbs, k16, decays)]
    invs = _unit_lower_inverses(ms, row, col)
    egcs = [jnp.exp(gc) for gc in gcs]
    sols = [_dot_f32(inv, jnp.concatenate([vc * bc, kb * egc], axis=1))
            for inv, vc, bc, kb, egc in zip(invs, vs, bs, kbs, egcs)]
    us = [sol[:, :dh] for sol in sols]
    ws = [sol[:, dh:].astype(BF16) for sol in sols]
    qks = [(_dot_nt(qc.astype(BF16), kc) * dec).astype(BF16) for qc, kc, dec in zip(qs, k16, decays)]
    q_decs = [(qc * egc).astype(BF16) for qc, egc in zip(qs, egcs)]
    lasts = [gc[c - 1:c, :] for gc in gcs]
    k_decs = [(kc * jnp.exp(last - gc)).astype(BF16) for kc, last, gc in zip(ks, lasts, gcs)]
    g_tots = [jnp.exp(last) for last in lasts]

    states = [s_ref[hi] for hi in range(nheads)]
    for ci in range(nch):
        sl = slice(ci * c, (ci + 1) * c)
        for hi in range(nheads):
            i = hi * nch + ci
            hs = slice(hi * dh, (hi + 1) * dh)
            sb = states[hi].astype(BF16)
            v_new = (us[i] - _dot(ws[i], sb)).astype(BF16)
            o = _dot(q_decs[i], sb) + _dot(qks[i], v_new)
            states[hi] = states[hi] * g_tots[i] + _dot_tn(k_decs[i], v_new)
            o_ref[0, sl, hs] = (_rms(o, nw_ref[...]) * _silu(z_ref[0, sl, hs])).astype(o_ref.dtype)
    for hi in range(nheads):
        s_ref[hi] = states[hi]


def _delta_mixer(proj, conv_w, a_log, dt_bias, norm_w, *, tt_cap=512):
    b, t, _ = proj.shape
    tt = _tile(t, tt_cap, DN_CHUNK)
    hh = DN_HEADS
    hps = DN_HEADS_PER_STEP
    wd = hps * V7X_LANES
    npair = hh // hps
    blk = lambda off: pl.BlockSpec((1, tt, wd), lambda bi, h, ti, off=off: (bi, ti, off + h))
    halo = lambda off: pl.BlockSpec((1, V7X_SUBLANES, wd), _halo_map(lambda h, off=off: off + h, tt))
    cw = lambda off: pl.BlockSpec((4, wd), lambda bi, h, ti, off=off: (0, off + h))
    row_spec = pl.BlockSpec((1, V7X_LANES), lambda bi, h, ti: (0, 0))
    pad = jnp.zeros((V7X_LANES - 2 * hh,), F32)
    alog_row = jnp.concatenate([jnp.zeros((hh,), F32), a_log, pad]).reshape(1, V7X_LANES)
    dtb_row = jnp.concatenate([jnp.zeros((hh,), F32), dt_bias, pad]).reshape(1, V7X_LANES)
    ba_col = 6 * hh
    return pl.pallas_call(
        _delta_kernel,
        grid=(b, npair, t // tt),
        in_specs=[
            blk(0), halo(0), blk(npair), halo(npair), blk(2 * npair), halo(2 * npair), blk(3 * npair),
            pl.BlockSpec((1, tt, V7X_LANES), lambda bi, h, ti: (bi, ti, ba_col)),
            cw(0), cw(npair), cw(2 * npair), row_spec, row_spec, row_spec,
        ],
        out_specs=pl.BlockSpec((1, tt, wd), lambda bi, h, ti: (bi, ti, h)),
        out_shape=jax.ShapeDtypeStruct((b, t, hh * V7X_LANES), BF16),
        scratch_shapes=[pltpu.VMEM((hps, V7X_LANES, V7X_LANES), F32)],
        compiler_params=_cparams("parallel", "parallel", "arbitrary"),
        name="delta_mixer",
    )(proj, proj, proj, proj, proj, proj, proj, proj, conv_w, conv_w, conv_w,
      alog_row, dtb_row, norm_w.reshape(1, V7X_LANES))


def _lru_kernel(x_ref, xh_ref, y_ref, cw_ref, cb_ref, wa_ref, ba_ref, wx_ref, bx_ref, lam_ref,
                o_ref, h_ref):
    first = pl.program_id(2) == 0
    tt = x_ref.shape[1]

    @pl.when(first)
    def _():
        h_ref[...] = jnp.zeros_like(h_ref)

    xc = _causal_conv(x_ref, xh_ref, cw_ref, first) + cb_ref[...]
    xb = xc.astype(BF16)
    r = jax.nn.sigmoid(_dot(xb, wa_ref[0]) + ba_ref[...])
    i = jax.nn.sigmoid(_dot(xb, wx_ref[0]) + bx_ref[...])
    log_a = -LRU_C * r * jax.nn.softplus(-lam_ref[...])
    a = jnp.exp(log_a)
    u = jnp.sqrt(1.0 - a * a) * (i * xc)

    rowi = lax.broadcasted_iota(jnp.int32, (tt, V7X_LANES), 0)
    s = 1
    while s < tt:
        if s < V7X_SUBLANES:
            a_sh = jnp.where(rowi >= s, pltpu.roll(a, s, axis=0), 1.0)
            u_sh = jnp.where(rowi >= s, pltpu.roll(u, s, axis=0), 0.0)
        else:
            a_sh = jnp.concatenate([jnp.ones((s, V7X_LANES), F32), a[:tt - s]], axis=0)
            u_sh = jnp.concatenate([jnp.zeros((s, V7X_LANES), F32), u[:tt - s]], axis=0)
        u = a * u_sh + u
        a = a * a_sh
        s *= 2
    hs = u + a * h_ref[...]
    h_ref[...] = hs[tt - 1:tt]
    o_ref[0] = (hs * jax.nn.gelu(y_ref[0])).astype(o_ref.dtype)


def _lru_mixer(proj, conv_w, conv_b, wa, ba, wx, bx, lam, *, x_col, y_col, tt_cap=512):
    b, t, _ = proj.shape
    tt = _tile(t, tt_cap, 16)
    gg = LRU_BLOCKS
    vec = lambda a: a.reshape(1, gg * V7X_LANES)
    vspec = pl.BlockSpec((1, V7X_LANES), lambda bi, g, ti: (0, g))
    wspec = pl.BlockSpec((1, V7X_LANES, V7X_LANES), lambda bi, g, ti: (g, 0, 0))
    return pl.pallas_call(
        _lru_kernel,
        grid=(b, gg, t // tt),
        in_specs=[
            pl.BlockSpec((1, tt, V7X_LANES), lambda bi, g, ti: (bi, ti, x_col + g)),
            pl.BlockSpec((1, V7X_SUBLANES, V7X_LANES), _halo_map(lambda g: x_col + g, tt)),
            pl.BlockSpec((1, tt, V7X_LANES), lambda bi, g, ti: (bi, ti, y_col + g)),
            pl.BlockSpec((4, V7X_LANES), lambda bi, g, ti: (0, g)),
            vspec, wspec, vspec, wspec, vspec, vspec,
        ],
        out_specs=pl.BlockSpec((1, tt, V7X_LANES), lambda bi, g, ti: (bi, ti, g)),
        out_shape=jax.ShapeDtypeStruct((b, t, gg * V7X_LANES), BF16),
        scratch_shapes=[pltpu.VMEM((1, V7X_LANES), F32)],
        compiler_params=_cparams("parallel", "parallel", "arbitrary"),
        name="lru_mixer",
    )(proj, proj, proj, conv_w, vec(conv_b), wa.astype(BF16), vec(ba), wx.astype(BF16), vec(bx), vec(lam))


def _dilated_kernel(q_ref, kc_ref, kp_ref, vc_ref, vp_ref, o_ref, qa_ref, ka_ref, va_ref, sa_ref, sn_ref):
    h = pl.program_id(1)
    has_prev = pl.program_id(2) > 0
    tq = q_ref.shape[1]
    bq = SWA_BLOCK
    d1, d2, d3 = SWA_DILATIONS
    assert d1 == 1 and d3 % d2 == 0
    qq = tq // d2
    st = d3 // d2
    for r in range(d2):
        qa_ref[r * qq:(r + 1) * qq, :] = q_ref.at[0][pl.ds(r, qq, stride=d2), :]
        for src_p, src_c, dst in ((kp_ref, kc_ref, ka_ref), (vp_ref, vc_ref, va_ref)):
            dst[2 * r * qq:2 * r * qq + qq, :] = src_p.at[0][pl.ds(r, qq, stride=d2), :]
            dst[2 * r * qq + qq:2 * (r + 1) * qq, :] = src_c.at[0][pl.ds(r, qq, stride=d2), :]

    slope = jnp.exp2(-(jnp.full((1, 1), h, jnp.int32).astype(F32) + 1.0) * (8.0 / SWA_HEADS))
    iq = lax.broadcasted_iota(jnp.int32, (bq, 2 * bq), 0)
    ik = lax.broadcasted_iota(jnp.int32, (bq, 2 * bq), 1)
    rel = bq + iq - ik
    in_window = (rel >= 0) & (rel <= SWA_SPAN)
    relf = rel.astype(F32)
    scale = V7X_LANES ** -0.5
    ones_blk = jnp.ones((2 * bq, V7X_LANES), BF16)

    def attend(q_rows, kcat, vcat, d, prev_ok):
        qb = (q_rows * scale).astype(BF16)
        s = _dot_nt(qb, kcat.astype(BF16)) - (slope * float(d)) * relf
        s = jnp.where(in_window & ((ik >= bq) | prev_ok), s, MASK_VALUE)
        m_b = jnp.max(s, axis=-1, keepdims=True)
        p = jnp.exp(s - m_b).astype(BF16)
        pv = _dot(p, jnp.concatenate([vcat.astype(BF16), ones_blk], axis=1))
        return jnp.broadcast_to(m_b, (bq, V7X_LANES)), pv[:, V7X_LANES:], pv[:, :V7X_LANES]

    def merge(old, new):
        (m_o, l_o, acc_o), (m_b, l_b, acc_b) = old, new
        m_n = jnp.maximum(m_o, m_b)
        alpha = jnp.exp(m_o - m_n)
        beta = jnp.exp(m_b - m_n)
        return m_n, alpha * l_o + beta * l_b, alpha * acc_o + beta * acc_b

    nblk2 = qq // bq
    assert nblk2 & (nblk2 - 1) == 0

    def body2(idx, carry):
        r = idx >> (nblk2.bit_length() - 1)
        n = idx & (nblk2 - 1)
        q0 = pl.multiple_of(r * qq + n * bq, bq)
        k0 = pl.multiple_of(2 * r * qq + qq + (n - 1) * bq, bq)
        new = attend(qa_ref[pl.ds(q0, bq), :], ka_ref[pl.ds(k0, 2 * bq), :], va_ref[pl.ds(k0, 2 * bq), :],
                     d2, (n > 0) | has_prev)
        for kk in range(3):
            sa_ref[kk, pl.ds(q0, bq), :] = new[kk]
        return carry

    lax.fori_loop(0, d2 * nblk2, body2, 0, unroll=SWA_UNROLL)

    nblk3 = tq // (bq * d3)
    assert nblk3 >= 1 and nblk3 & (nblk3 - 1) == 0

    def body3(idx, carry):
        r3 = idx >> (nblk3.bit_length() - 1)
        n = idx & (nblk3 - 1)
        r = r3 & (d2 - 1)
        p0 = (r3 >> (d2.bit_length() - 1)) + n * (bq * st)
        q0 = r * qq + p0
        k0 = 2 * r * qq + qq + p0
        take = lambda ref, start: ref[pl.ds(start, bq, stride=st), :]
        kcat = jnp.concatenate([take(ka_ref, k0 - bq * st), take(ka_ref, k0)], axis=0)
        vcat = jnp.concatenate([take(va_ref, k0 - bq * st), take(va_ref, k0)], axis=0)
        new = attend(take(qa_ref, q0), kcat, vcat, d3, (n > 0) | has_prev)
        out = merge(tuple(take(sa_ref.at[kk], q0) for kk in range(3)), new)
        for kk in range(3):
            sa_ref.at[kk][pl.ds(q0, bq, stride=st), :] = out[kk]
        return carry

    assert d2 & (d2 - 1) == 0
    lax.fori_loop(0, d3 * nblk3, body3, 0, unroll=SWA_UNROLL)

    for r in range(d2):
        for kk in range(3):
            sn_ref.at[kk][pl.ds(r, qq, stride=d2), :] = sa_ref[kk, r * qq:(r + 1) * qq, :]

    def finish(base, kcat, vcat, prev_ok):
        new = attend(q_ref[0, pl.ds(base, bq), :], kcat, vcat, d1, prev_ok)
        _, l_f, acc_f = merge(tuple(sn_ref[kk, pl.ds(base, bq), :] for kk in range(3)), new)
        o_ref[0, pl.ds(base, bq), :] = (acc_f / l_f).astype(o_ref.dtype)

    finish(0, jnp.concatenate([kp_ref[0, tq - bq:tq, :], kc_ref[0, 0:bq, :]], axis=0),
           jnp.concatenate([vp_ref[0, tq - bq:tq, :], vc_ref[0, 0:bq, :]], axis=0), has_prev)

    def body1(n, carry):
        base = pl.multiple_of(n * bq, bq)
        k0 = pl.multiple_of(base - bq, bq)
        finish(base, kc_ref[0, pl.ds(k0, 2 * bq), :], vc_ref[0, pl.ds(k0, 2 * bq), :], True)
        return carry

    nblk1 = tq // bq
    lax.fori_loop(1, nblk1, body1, 0, unroll=SWA_UNROLL_D1)


def _dilated_mixer(proj, *, q_col, k_col, v_col, tq=2048):
    b, t, _ = proj.shape
    assert t % tq == 0 and tq % (SWA_BLOCK * max(SWA_DILATIONS)) == 0
    hh = SWA_HEADS
    cur = lambda off: pl.BlockSpec((1, tq, V7X_LANES), lambda bi, h, ti, off=off: (bi, ti, off + h))
    prev = lambda off: pl.BlockSpec((1, tq, V7X_LANES),
                                    lambda bi, h, ti, off=off: (bi, jnp.maximum(ti - 1, 0), off + h))
    return pl.pallas_call(
        _dilated_kernel,
        grid=(b, hh, t // tq),
        in_specs=[cur(q_col), cur(k_col), prev(k_col), cur(v_col), prev(v_col)],
        out_specs=pl.BlockSpec((1, tq, V7X_LANES), lambda bi, h, ti: (bi, ti, h)),
        out_shape=jax.ShapeDtypeStruct((b, t, hh * V7X_LANES), BF16),
        scratch_shapes=[pltpu.VMEM((tq, V7X_LANES), F32), pltpu.VMEM((2 * tq, V7X_LANES), F32),
                        pltpu.VMEM((2 * tq, V7X_LANES), F32), pltpu.VMEM((3, tq, V7X_LANES), F32),
                        pltpu.VMEM((3, tq, V7X_LANES), F32)],
        compiler_params=_cparams("parallel", "parallel", "parallel"),
        name="dilated_mixer",
    )(proj, proj, proj, proj, proj)


def _retention_kernel(q_ref, k_ref, v_ref, g_ref, o_ref, s_ref):
    h = pl.program_id(1)
    tt = q_ref.shape[1]
    c = min(RET_CHUNK, tt)
    nch = tt // c

    @pl.when(pl.program_id(2) == 0)
    def _():
        s_ref[...] = jnp.zeros_like(s_ref)

    hf = jnp.full((1, 1), h, jnp.int32).astype(F32)
    log_gamma = jnp.log1p(-jnp.exp2(-5.0 - hf))
    row = lax.broadcasted_iota(jnp.int32, (c, c), 0)
    col = lax.broadcasted_iota(jnp.int32, (c, c), 1)
    rel = (row - col).astype(F32)
    dmask = jnp.where(rel >= 0, jnp.exp(jnp.maximum(rel, 0.0) * log_gamma), 0.0)
    idx = lax.broadcasted_iota(jnp.int32, (c, 1), 0).astype(F32)
    q_scale = jnp.exp((idx + 1.0) * log_gamma)
    k_scale = jnp.exp((c - 1.0 - idx) * log_gamma)
    chunk_decay = jnp.exp(float(c) * log_gamma)

    sls = [slice(ci * c, (ci + 1) * c) for ci in range(nch)]
    qs = [q_ref[0, sl, :] for sl in sls]
    ks = [k_ref[0, sl, :] * (V7X_LANES ** -0.5) for sl in sls]
    vs = [v_ref[0, sl, :].astype(BF16) for sl in sls]
    intras = [_dot((_dot_nt(q.astype(BF16), k.astype(BF16)) * dmask).astype(BF16), v)
              for q, k, v in zip(qs, ks, vs)]
    kvs = [_dot_tn((k * k_scale).astype(BF16), v) for k, v in zip(ks, vs)]
    states = [s_ref[...]]
    for kv in kvs:
        states.append(states[-1] * chunk_decay + kv)
    s_ref[...] = states[-1]
    for sl, q, intra, state in zip(sls, qs, intras, states):
        o = intra + _dot((q * q_scale).astype(BF16), state.astype(BF16))
        mu = jnp.mean(o, axis=-1, keepdims=True)
        oc = o - mu
        o = oc * lax.rsqrt(jnp.mean(oc * oc, axis=-1, keepdims=True) + GN_EPS)
        o_ref[0, sl, :] = (o * _silu(g_ref[0, sl, :])).astype(o_ref.dtype)


def _retention_mixer(proj, *, q_col, k_col, v_col, g_col):
    b, t, _ = proj.shape
    c = _tile(t, RET_TILE, RET_CHUNK)
    hh = RET_HEADS
    wide = RET_DV // V7X_LANES
    assert v_col % wide == 0 and g_col % wide == 0
    nar = lambda off: pl.BlockSpec((1, c, V7X_LANES), lambda bi, h, ti, off=off: (bi, ti, off + h))
    wid = lambda off: pl.BlockSpec((1, c, RET_DV), lambda bi, h, ti, off=off: (bi, ti, off // wide + h))
    return pl.pallas_call(
        _retention_kernel,
        grid=(b, hh, t // c),
        in_specs=[nar(q_col), nar(k_col), wid(v_col), wid(g_col)],
        out_specs=pl.BlockSpec((1, c, RET_DV), lambda bi, h, ti: (bi, ti, h)),
        out_shape=jax.ShapeDtypeStruct((b, t, hh * RET_DV), BF16),
        scratch_shapes=[pltpu.VMEM((V7X_LANES, RET_DV), F32)],
        compiler_params=_cparams("parallel", "parallel", "arbitrary"),
        name="retention_mixer",
    )(proj, proj, proj, proj)


def _even_w_in_layout(w):
    d = w.shape[0]
    qkvz = 4 * DN_HEADS * V7X_LANES
    nba = 2 * DN_HEADS
    pad = jnp.zeros((d, 2 * V7X_LANES - nba), w.dtype)
    return jnp.concatenate([w[:, :qkvz], w[:, qkvz + nba:], w[:, qkvz:qkvz + nba], pad], axis=1)


def kernel(x, p, ln_mix_w, ln_mlp_w, ln_ple_w, w_up, w_down, w_ple_proj, w_ple_gate, ln_final_w,
           ev_w_in, ev_w_out, dn_conv_w, dn_a_log, dn_dt_bias, dn_norm_w,
           lru_conv_w, lru_conv_b, lru_wa, lru_ba, lru_wx, lru_bx, lru_lambda,
           od_w_in, od_w_out):
    b, t, d = x.shape
    depth = ln_mix_w.shape[0]
    m = b * t
    h = x.reshape(m, d)
    w_ple_proj, w_ple_gate, ev_w_in, ev_w_out, od_w_in, od_w_out = (
        _to_bf16(w) for w in (w_ple_proj, w_ple_gate, ev_w_in, ev_w_out, od_w_in, od_w_out))
    w_up = _to_bf16(w_up, tn=MLP_UP_TN)
    w_down = _to_bf16(w_down)
    p = p.reshape(depth, m, -1)
    hn, mix_norm_w = h, ln_mix_w[0]
    for i in range(depth):
        j = i // 2
        if i % 2 == 0:
            w_in = _even_w_in_layout(ev_w_in[j])
            proj = _norm_matmul(hn, mix_norm_w, w_in, act=None, out_dtype=F32)
            proj = proj.reshape(b, t, -1)
            y_a = _delta_mixer(proj, dn_conv_w[j], dn_a_log[j], dn_dt_bias[j], dn_norm_w[j])
            y_b = _lru_mixer(proj, lru_conv_w[j], lru_conv_b[j], lru_wa[j], lru_ba[j], lru_wx[j], lru_bx[j],
                             lru_lambda[j], x_col=4 * DN_HEADS, y_col=4 * DN_HEADS + LRU_BLOCKS)
            w_out = ev_w_out
        else:
            proj = _norm_matmul(hn, mix_norm_w, od_w_in, layer=j, act=None, out_dtype=F32)
            proj = proj.reshape(b, t, -1)
            y_a = _dilated_mixer(proj, q_col=0, k_col=SWA_HEADS, v_col=2 * SWA_HEADS)
            y_b = _retention_mixer(proj, q_col=3 * SWA_HEADS, k_col=3 * SWA_HEADS + RET_HEADS,
                                   v_col=3 * SWA_HEADS + 2 * RET_HEADS,
                                   g_col=3 * SWA_HEADS + 2 * RET_HEADS + RET_HEADS * RET_DV // V7X_LANES)
            w_out = od_w_out
        h, hn = _matmul_residual([y_a.reshape(m, -1), y_b.reshape(m, -1)], w_out, h, layer=j,
                                 next_norm_w=ln_mlp_w[i], tn_cap=d)
        up = _norm_matmul(hn, None, w_up, layer=i, act="relu2", out_dtype=BF16, tm_cap=MLP_UP_TM)
        h = _matmul_residual_ksplit(up, w_down, h, layer=i)
        if i == depth - 1:
            return _ple(h, ln_ple_w[i], w_ple_gate, p, w_ple_proj, ln_final_w, layer=i,
                        final_norm=True).reshape(b, t, d)
        h, hn = _ple(h, ln_ple_w[i], w_ple_gate, p, w_ple_proj, ln_mix_w[i + 1], layer=i, final_norm=False)
        mix_norm_w = None
```

```python
import functools

import jax
import jax.numpy as jnp
from jax import lax
from jax.experimental import pallas as pl
from jax.experimental.pallas import tpu as pltpu

F32 = jnp.float32
BF16 = jnp.bfloat16

V7X_LANES = 128
V7X_SUBLANES = 8
V7X_VMEM_LIMIT_BYTES = 56 * 1024 * 1024

NORM_EPS = 1e-6
GN_EPS = 1e-5
LRU_C = 8.0
DN_HEADS = 8
LRU_BLOCKS = 8
SWA_HEADS = 8
RET_HEADS = 4
RET_DV = 256
SWA_DILATIONS = (1, 4, 16)
SWA_SPAN = 128
SWA_BLOCK = 128
DN_CHUNK = 128
DN_HEADS_PER_STEP = 4
RET_CHUNK = 256
RET_TILE = 1024
MASK_VALUE = -1e30
MLP_UP_TN = 2048
MLP_UP_TM = 1024


def _cparams(*sem):
    return pltpu.CompilerParams(dimension_semantics=sem, vmem_limit_bytes=V7X_VMEM_LIMIT_BYTES)


def _dot(a, b):
    return jnp.dot(a, b, preferred_element_type=F32)


def _dot_nt(a, b):
    return lax.dot_general(a, b, (((1,), (1,)), ((), ())), preferred_element_type=F32)


def _dot_tn(a, b):
    return lax.dot_general(a, b, (((0,), (0,)), ((), ())), preferred_element_type=F32)


def _split_bf16(x):
    hi = x.astype(BF16)
    lo = (x - hi.astype(F32)).astype(BF16)
    return hi, lo


def _dot_f32(a, b):
    ah, al = a if isinstance(a, tuple) else _split_bf16(a)
    bh, bl = b if isinstance(b, tuple) else _split_bf16(b)
    n = bh.shape[1]
    wide = _dot(ah, jnp.concatenate([bh, bl], axis=1))
    return wide[:, :n] + (wide[:, n:] + _dot(al, bh))


def _rms(x, w):
    return x * lax.rsqrt(jnp.mean(x * x, axis=-1, keepdims=True) + NORM_EPS) * w


def _silu(x):
    return x * jax.nn.sigmoid(x)


def _tile(n, cap, quantum):
    if n <= cap:
        return n
    best = None
    for c in range(quantum, cap + 1, quantum):
        if n % c == 0:
            best = c
    assert best is not None, (n, cap, quantum)
    return best


CAST_BLOCK_BYTES = 8 * 1024 * 1024


def _cast_kernel(x_ref, o_ref):
    o_ref[...] = x_ref[...].astype(o_ref.dtype)


def _to_bf16(w, tn=None):
    l, r, c = w.shape
    tn = c if tn is None else tn
    tr = _tile(r, max(16, CAST_BLOCK_BYTES // (4 * tn) // 16 * 16), 16)
    out = pl.pallas_call(
        _cast_kernel,
        grid=(l, c // tn, r // tr),
        in_specs=[pl.BlockSpec((None, tr, tn), lambda i, j, k: (i, k, j))],
        out_specs=pl.BlockSpec((None, None, tr, tn), lambda i, j, k: (i, j, k, 0)),
        out_shape=jax.ShapeDtypeStruct((l, c // tn, r, tn), BF16),
        compiler_params=_cparams("parallel", "parallel", "parallel"),
        name="to_bf16",
    )(w)
    return out if tn != c else out.reshape(l, r, c)


def _w_spec(w, layer, kk, tn, row_block=0):
    if w.ndim == 2:
        return pl.BlockSpec((kk, tn), lambda i, j: (row_block, j))
    if w.ndim == 3:
        return pl.BlockSpec((None, kk, tn), lambda i, j: (layer, row_block, j))
    assert w.shape[3] == tn, (w.shape, tn)
    return pl.BlockSpec((None, None, kk, tn), lambda i, j: (layer, j, row_block, 0))


def _w_cols(w):
    return w.shape[-1] if w.ndim < 4 else w.shape[1] * w.shape[3]


def _norm_mm_kernel(x_ref, g_ref, w_ref, o_ref, hn_ref, *, act):
    @pl.when(pl.program_id(1) == 0)
    def _():
        hn_ref[...] = _rms(x_ref[...], g_ref[...]).astype(BF16)

    _mm_act_kernel(hn_ref, w_ref, o_ref, act=act)


def _mm_act_kernel(x_ref, w_ref, o_ref, *, act):
    a = _dot(x_ref[...], w_ref[...])
    if act == "relu2":
        a = jnp.square(jnp.maximum(a, 0.0))
    o_ref[...] = a.astype(o_ref.dtype)


def _norm_matmul(x, g, w, *, act, out_dtype, layer=0, tm_cap=1024, tn_cap=1536):
    m, k = x.shape
    n = _w_cols(w)
    tm = _tile(m, tm_cap, 256)
    tn = w.shape[3] if w.ndim == 4 else _tile(n, tn_cap, 256)
    x_spec = pl.BlockSpec((tm, k), lambda i, j: (i, 0))
    common = dict(
        grid=(m // tm, n // tn),
        out_specs=pl.BlockSpec((tm, tn), lambda i, j: (i, j)),
        out_shape=jax.ShapeDtypeStruct((m, n), out_dtype),
    )
    if g is None:
        assert x.dtype == BF16
        return pl.pallas_call(
            functools.partial(_mm_act_kernel, act=act),
            in_specs=[x_spec, _w_spec(w, layer, k, tn)],
            compiler_params=_cparams("parallel", "parallel"),
            name="matmul_act", **common,
        )(x, w)
    return pl.pallas_call(
        functools.partial(_norm_mm_kernel, act=act),
        in_specs=[x_spec, pl.BlockSpec((1, k), lambda i, j: (0, 0)), _w_spec(w, layer, k, tn)],
        scratch_shapes=[pltpu.VMEM((tm, k), BF16)],
        compiler_params=_cparams("parallel", "arbitrary"),
        name="norm_matmul", **common,
    )(x, g.reshape(1, k), w)


def _mm_res_kernel(*refs, n_x, with_norm):
    xs, ws = refs[:n_x], refs[n_x:2 * n_x]
    res_ref = refs[2 * n_x]
    outs = refs[2 * n_x + 1:]
    acc = res_ref[...]
    for x_ref, w_ref in zip(xs, ws):
        acc = acc + _dot(x_ref[...], w_ref[...])
    if with_norm:
        g_ref, o_ref, hn_ref = outs
        hn_ref[...] = _rms(acc, g_ref[...]).astype(BF16)
    else:
        o_ref, = outs
    o_ref[...] = acc


def _matmul_residual(xs, w, res, *, layer=0, next_norm_w=None, tm_cap=512, tn_cap=512):
    m, n = res.shape
    assert n == _w_cols(w)
    tm = _tile(m, tm_cap, 256)
    tn = w.shape[3] if w.ndim == 4 else _tile(n, tn_cap, 256)
    with_norm = next_norm_w is not None
    assert not with_norm or tn == n
    in_specs, row = [], 0
    for x in xs:
        in_specs.append(pl.BlockSpec((tm, x.shape[1]), lambda i, j: (i, 0)))
    w_specs = []
    for x in xs:
        kk = x.shape[1]
        assert row % kk == 0
        w_specs.append(_w_spec(w, layer, kk, tn, row_block=row // kk))
        row += kk
    assert row == w.shape[-2]
    tile = pl.BlockSpec((tm, tn), lambda i, j: (i, j))
    in_specs = in_specs + w_specs + [tile]
    args = [*xs, *([w] * len(xs)), res]
    out_specs, out_shape = tile, jax.ShapeDtypeStruct((m, n), F32)
    if with_norm:
        in_specs.append(pl.BlockSpec((1, n), lambda i, j: (0, 0)))
        args.append(next_norm_w.reshape(1, n))
        out_specs, out_shape = [tile, tile], [out_shape, jax.ShapeDtypeStruct((m, n), BF16)]
    return pl.pallas_call(
        functools.partial(_mm_res_kernel, n_x=len(xs), with_norm=with_norm),
        grid=(m // tm, n // tn),
        in_specs=in_specs,
        out_specs=out_specs,
        out_shape=out_shape,
        compiler_params=_cparams("parallel", "parallel"),
        name="matmul_residual",
    )(*args)


def _mm_res_ksplit_kernel(x_ref, w_ref, res_ref, o_ref):
    @pl.when(pl.program_id(1) == 0)
    def _():
        o_ref[...] = res_ref[...]

    o_ref[...] += _dot(x_ref[...], w_ref[...])


def _matmul_residual_ksplit(x, w, res, *, layer, tm_cap=1024, tk_cap=1024):
    m, n = res.shape
    k = x.shape[1]
    tm = _tile(m, tm_cap, 256)
    tk = _tile(k, tk_cap, 256)
    return pl.pallas_call(
        _mm_res_ksplit_kernel,
        grid=(m // tm, k // tk),
        in_specs=[
            pl.BlockSpec((tm, tk), lambda i, kk: (i, kk)),
            pl.BlockSpec((None, tk, n), lambda i, kk: (layer, kk, 0)),
            pl.BlockSpec((tm, n), lambda i, kk: (i, 0)),
        ],
        out_specs=pl.BlockSpec((tm, n), lambda i, kk: (i, 0)),
        out_shape=jax.ShapeDtypeStruct((m, n), F32),
        compiler_params=_cparams("parallel", "arbitrary"),
        name="matmul_residual_ksplit",
    )(x, w, res)


def _ple_kernel(x_ref, g_ref, wg_ref, p_ref, wp_ref, gt_ref, o_ref, *hn_refs, final_norm):
    d = x_ref.shape[1]
    hn = _rms(x_ref[...], g_ref[...]).astype(BF16)
    pb = p_ref[...].astype(BF16)
    nh = d // 2 if d % (2 * V7X_LANES) == 0 else d
    for c0 in range(0, d, nh):
        gate = jax.nn.sigmoid(_dot(hn, wg_ref[:, c0:c0 + nh]))
        pp = _dot(pb, wp_ref[:, c0:c0 + nh])
        o_ref[:, c0:c0 + nh] = x_ref[:, c0:c0 + nh] + gate * pp
    tail = _rms(o_ref[...], gt_ref[...])
    if final_norm:
        o_ref[...] = tail
    else:
        hn_refs[0][...] = tail.astype(BF16)


def _ple(x, g, wg, p, wp, g_tail, *, layer, final_norm, tm_cap=512):
    m, d = x.shape
    pd = p.shape[2]
    tm = _tile(m, tm_cap, 256)
    row_tile = pl.BlockSpec((tm, d), lambda i: (i, 0))
    out_specs, out_shape = row_tile, jax.ShapeDtypeStruct((m, d), F32)
    if not final_norm:
        out_specs, out_shape = [row_tile, row_tile], [out_shape, jax.ShapeDtypeStruct((m, d), BF16)]
    return pl.pallas_call(
        functools.partial(_ple_kernel, final_norm=final_norm),
        grid=(m // tm,),
        in_specs=[
            row_tile,
            pl.BlockSpec((1, d), lambda i: (0, 0)),
            pl.BlockSpec((None, d, d), lambda i: (layer, 0, 0)),
            pl.BlockSpec((None, tm, pd), lambda i: (layer, i, 0)),
            pl.BlockSpec((None, pd, d), lambda i: (layer, 0, 0)),
            pl.BlockSpec((1, d), lambda i: (0, 0)),
        ],
        out_specs=out_specs,
        out_shape=out_shape,
        compiler_params=_cparams("parallel"),
        name="ple",
    )(x, g.reshape(1, d), wg, p, wp, g_tail.reshape(1, d))


def _causal_conv(x_ref, halo_ref, w_ref, first):
    x = x_ref[0]
    halo = jnp.where(first, 0.0, halo_ref[0])
    xp = jnp.concatenate([halo, x], axis=0)
    w = w_ref[...]
    y = pltpu.roll(xp, 3, axis=0)[V7X_SUBLANES:] * w[0:1]
    y = y + pltpu.roll(xp, 2, axis=0)[V7X_SUBLANES:] * w[1:2]
    y = y + pltpu.roll(xp, 1, axis=0)[V7X_SUBLANES:] * w[2:3]
    return y + x * w[3:4]


def _halo_map(col, rows_per_tile):
    nb = rows_per_tile // V7X_SUBLANES
    return lambda b, h, t: (b, jnp.maximum(t * nb - 1, 0), col(h))


def _unit_lower_inverses(ms, row, col):
    c = ms[0].shape[0]
    eye = (row == col).astype(F32)
    diag16 = (row >> 4) == (col >> 4)
    b16 = lambda xs: [x.astype(BF16) for x in xs]
    ns = [jnp.where(diag16, m, 0.0) for m in ms]
    invs = [eye - n for n in ns]
    qs = b16(ns)
    for step in range(3):
        qs = b16([_dot(q, q) for q in qs])
        invs = [inv + _dot(inv.astype(BF16), q) for inv, q in zip(invs, qs)]
    shift = 4
    while (1 << shift) < c:
        off = ((row >> (shift + 1)) == (col >> (shift + 1))) & ((row >> shift) != (col >> shift))
        inv16 = b16(invs)
        tmps = [_dot(inv, jnp.where(off, m, 0.0).astype(BF16)) for inv, m in zip(inv16, ms)]
        invs = [inv - _dot(tmp.astype(BF16), i16) for inv, tmp, i16 in zip(invs, tmps, inv16)]
        shift += 1
    xs = [_split_bf16(inv) for inv in invs]
    res = [eye - inv - _dot_f32(m, x) for inv, m, x in zip(invs, ms, xs)]
    return [inv + _dot_f32(x, r) for inv, x, r in zip(invs, xs, res)]


def _delta_kernel(q_ref, qh_ref, k_ref, kh_ref, v_ref, vh_ref, z_ref, ba_ref,
                  wq_ref, wk_ref, wv_ref, alog_ref, dtb_ref, nw_ref, o_ref, s_ref):
    hp = pl.program_id(1)
    first = pl.program_id(2) == 0
    tt = q_ref.shape[1]
    c = DN_CHUNK
    dh = V7X_LANES
    nheads = q_ref.shape[2] // dh
    nch = tt // c

    @pl.when(first)
    def _():
        s_ref[...] = jnp.zeros_like(s_ref)

    def l2n(x):
        return x * lax.rsqrt(jnp.sum(x * x, axis=-1, keepdims=True) + 1e-6)

    ba = ba_ref[0]
    lane = lax.broadcasted_iota(jnp.int32, ba.shape, 1)
    sig_ba = jax.nn.sigmoid(ba)
    g_all = -jnp.exp(alog_ref[...]) * jax.nn.softplus(ba + dtb_ref[...])

    row = lax.broadcasted_iota(jnp.int32, (c, c), 0)
    col = lax.broadcasted_iota(jnp.int32, (c, c), 1)
    causal = row >= col
    strict = row > col
    ltri = causal.astype(BF16)

    def cumsum_block(gblk):
        g_hi, g_lo = _split_bf16(gblk)
        g_lo2 = (gblk - g_hi.astype(F32) - g_lo.astype(F32)).astype(BF16)
        return _dot(ltri, g_hi) + (_dot(ltri, g_lo) + _dot(ltri, g_lo2))

    gc_blocks = [cumsum_block(g_all[ci * c:(ci + 1) * c]) for ci in range(nch)]
    gc_blocks_t = [gcb.T for gcb in gc_blocks]

    q_all = _silu(_causal_conv(q_ref, qh_ref, wq_ref, first))
    k_all = _silu(_causal_conv(k_ref, kh_ref, wk_ref, first))
    v_all = _silu(_causal_conv(v_ref, vh_ref, wv_ref, first))
    qs, ks, vs, bs, gcs, gc_rows = [], [], [], [], [], []
    for hi in range(nheads):
        hs = slice(hi * dh, (hi + 1) * dh)
        head = nheads * hp + hi
        qh = l2n(q_all[:, hs]) * (dh ** -0.5)
        kh = l2n(k_all[:, hs])
        vh = v_all[:, hs]
        beta = jnp.sum(jnp.where(lane == head, sig_ba, 0.0), axis=-1, keepdims=True)
        for ci in range(nch):
            sl = slice(ci * c, (ci + 1) * c)
            qs.append(qh[sl]); ks.append(kh[sl]); vs.append(vh[sl]); bs.append(beta[sl])
            gcol = jnp.sum(jnp.where(col == head + DN_HEADS, gc_blocks[ci], 0.0), axis=-1, keepdims=True)
            grow = jnp.sum(jnp.where(row == head + DN_HEADS, gc_blocks_t[ci], 0.0), axis=0, keepdims=True)
            gcs.append(jnp.broadcast_to(gcol, (c, c)))
            gc_rows.append(jnp.broadcast_to(grow, (c, c)))

    decays = [jnp.where(causal, jnp.exp(jnp.where(causal, gc - gr, 0.0)), 0.0) for gc, gr in zip(gcs, gc_rows)]
    kbs = [kc * bc for kc, bc in zip(ks, bs)]
    k16 = [kc.astype(BF16) for kc in ks]
    ms = [jnp.where(strict, _dot_nt(kb.astype(BF16), kc) * dec, 0.0) for kb, kc, dec in zip(kbs, k16, decays)]
    invs = _unit_lower_inverses(ms, row, col)
    egcs = [jnp.exp(gc) for gc in gcs]
    sols = [_dot_f32(inv, jnp.concatenate([vc * bc, kb * egc], axis=1))
            for inv, vc, bc, kb, egc in zip(invs, vs, bs, kbs, egcs)]
    us = [sol[:, :dh] for sol in sols]
    ws = [sol[:, dh:].astype(BF16) for sol in sols]
    qks = [(_dot_nt(qc.astype(BF16), kc) * dec).astype(BF16) for qc, kc, dec in zip(qs, k16, decays)]
    q_decs = [(qc * egc).astype(BF16) for qc, egc in zip(qs, egcs)]
    lasts = [gc[c - 1:c, :] for gc in gcs]
    k_decs = [(kc * jnp.exp(last - gc)).astype(BF16) for kc, last, gc in zip(ks, lasts, gcs)]
    g_tots = [jnp.exp(last) for last in lasts]

    states = [s_ref[hi] for hi in range(nheads)]
    for ci in range(nch):
        sl = slice(ci * c, (ci + 1) * c)
        for hi in range(nheads):
            i = hi * nch + ci
            hs = slice(hi * dh, (hi + 1) * dh)
            sb = states[hi].astype(BF16)
            v_new = (us[i] - _dot(ws[i], sb)).astype(BF16)
            o = _dot(q_decs[i], sb) + _dot(qks[i], v_new)
            states[hi] = states[hi] * g_tots[i] + _dot_tn(k_decs[i], v_new)
            o_ref[0, sl, hs] = (_rms(o, nw_ref[...]) * _silu(z_ref[0, sl, hs])).astype(o_ref.dtype)
    for hi in range(nheads):
        s_ref[hi] = states[hi]


def _delta_mixer(proj, conv_w, a_log, dt_bias, norm_w, *, tt_cap=512):
    b, t, _ = proj.shape
    tt = _tile(t, tt_cap, DN_CHUNK)
    hh = DN_HEADS
    hps = DN_HEADS_PER_STEP
    wd = hps * V7X_LANES
    npair = hh // hps
    blk = lambda off: pl.BlockSpec((1, tt, wd), lambda bi, h, ti, off=off: (bi, ti, off + h))
    halo = lambda off: pl.BlockSpec((1, V7X_SUBLANES, wd), _halo_map(lambda h, off=off: off + h, tt))
    cw = lambda off: pl.BlockSpec((4, wd), lambda bi, h, ti, off=off: (0, off + h))
    row_spec = pl.BlockSpec((1, V7X_LANES), lambda bi, h, ti: (0, 0))
    pad = jnp.zeros((V7X_LANES - 2 * hh,), F32)
    alog_row = jnp.concatenate([jnp.zeros((hh,), F32), a_log, pad]).reshape(1, V7X_LANES)
    dtb_row = jnp.concatenate([jnp.zeros((hh,), F32), dt_bias, pad]).reshape(1, V7X_LANES)
    ba_col = 6 * hh
    return pl.pallas_call(
        _delta_kernel,
        grid=(b, npair, t // tt),
        in_specs=[
            blk(0), halo(0), blk(npair), halo(npair), blk(2 * npair), halo(2 * npair), blk(3 * npair),
            pl.BlockSpec((1, tt, V7X_LANES), lambda bi, h, ti: (bi, ti, ba_col)),
            cw(0), cw(npair), cw(2 * npair), row_spec, row_spec, row_spec,
        ],
        out_specs=pl.BlockSpec((1, tt, wd), lambda bi, h, ti: (bi, ti, h)),
        out_shape=jax.ShapeDtypeStruct((b, t, hh * V7X_LANES), BF16),
        scratch_shapes=[pltpu.VMEM((hps, V7X_LANES, V7X_LANES), F32)],
        compiler_params=_cparams("parallel", "parallel", "arbitrary"),
        name="delta_mixer",
    )(proj, proj, proj, proj, proj, proj, proj, proj, conv_w, conv_w, conv_w,
      alog_row, dtb_row, norm_w.reshape(1, V7X_LANES))


def _lru_kernel(x_ref, xh_ref, y_ref, cw_ref, cb_ref, wa_ref, ba_ref, wx_ref, bx_ref, lam_ref,
                o_ref, h_ref):
    first = pl.program_id(2) == 0
    tt = x_ref.shape[1]

    @pl.when(first)
    def _():
        h_ref[...] = jnp.zeros_like(h_ref)

    xc = _causal_conv(x_ref, xh_ref, cw_ref, first) + cb_ref[...]
    xb = xc.astype(BF16)
    r = jax.nn.sigmoid(_dot(xb, wa_ref[0]) + ba_ref[...])
    i = jax.nn.sigmoid(_dot(xb, wx_ref[0]) + bx_ref[...])
    log_a = -LRU_C * r * jax.nn.softplus(-lam_ref[...])
    a = jnp.exp(log_a)
    u = jnp.sqrt(1.0 - a * a) * (i * xc)

    rowi = lax.broadcasted_iota(jnp.int32, (tt, V7X_LANES), 0)
    s = 1
    while s < tt:
        if s < V7X_SUBLANES:
            a_sh = jnp.where(rowi >= s, pltpu.roll(a, s, axis=0), 1.0)
            u_sh = jnp.where(rowi >= s, pltpu.roll(u, s, axis=0), 0.0)
        else:
            a_sh = jnp.concatenate([jnp.ones((s, V7X_LANES), F32), a[:tt - s]], axis=0)
            u_sh = jnp.concatenate([jnp.zeros((s, V7X_LANES), F32), u[:tt - s]], axis=0)
        u = a * u_sh + u
        a = a * a_sh
        s *= 2
    hs = u + a * h_ref[...]
    h_ref[...] = hs[tt - 1:tt]
    o_ref[0] = (hs * jax.nn.gelu(y_ref[0])).astype(o_ref.dtype)


def _lru_mixer(proj, conv_w, conv_b, wa, ba, wx, bx, lam, *, x_col, y_col, tt_cap=512):
    b, t, _ = proj.shape
    tt = _tile(t, tt_cap, 16)
    gg = LRU_BLOCKS
    vec = lambda a: a.reshape(1, gg * V7X_LANES)
    vspec = pl.BlockSpec((1, V7X_LANES), lambda bi, g, ti: (0, g))
    wspec = pl.BlockSpec((1, V7X_LANES, V7X_LANES), lambda bi, g, ti: (g, 0, 0))
    return pl.pallas_call(
        _lru_kernel,
        grid=(b, gg, t // tt),
        in_specs=[
            pl.BlockSpec((1, tt, V7X_LANES), lambda bi, g, ti: (bi, ti, x_col + g)),
            pl.BlockSpec((1, V7X_SUBLANES, V7X_LANES), _halo_map(lambda g: x_col + g, tt)),
            pl.BlockSpec((1, tt, V7X_LANES), lambda bi, g, ti: (bi, ti, y_col + g)),
            pl.BlockSpec((4, V7X_LANES), lambda bi, g, ti: (0, g)),
            vspec, wspec, vspec, wspec, vspec, vspec,
        ],
        out_specs=pl.BlockSpec((1, tt, V7X_LANES), lambda bi, g, ti: (bi, ti, g)),
        out_shape=jax.ShapeDtypeStruct((b, t, gg * V7X_LANES), BF16),
        scratch_shapes=[pltpu.VMEM((1, V7X_LANES), F32)],
        compiler_params=_cparams("parallel", "parallel", "arbitrary"),
        name="lru_mixer",
    )(proj, proj, proj, conv_w, vec(conv_b), wa.astype(BF16), vec(ba), wx.astype(BF16), vec(bx), vec(lam))


def _dilated_kernel(q_ref, kc_ref, kp_ref, vc_ref, vp_ref, o_ref, qa_ref, ka_ref, va_ref, sa_ref, sn_ref):
    h = pl.program_id(1)
    has_prev = pl.program_id(2) > 0
    tq = q_ref.shape[1]
    bq = SWA_BLOCK
    d1, d2, d3 = SWA_DILATIONS
    assert d1 == 1 and d3 % d2 == 0
    qq = tq // d2
    st = d3 // d2
    for r in range(d2):
        qa_ref[r * qq:(r + 1) * qq, :] = q_ref.at[0][pl.ds(r, qq, stride=d2), :]
        for src_p, src_c, dst in ((kp_ref, kc_ref, ka_ref), (vp_ref, vc_ref, va_ref)):
            dst[2 * r * qq:2 * r * qq + qq, :] = src_p.at[0][pl.ds(r, qq, stride=d2), :]
            dst[2 * r * qq + qq:2 * (r + 1) * qq, :] = src_c.at[0][pl.ds(r, qq, stride=d2), :]

    slope = jnp.exp2(-(jnp.full((1, 1), h, jnp.int32).astype(F32) + 1.0) * (8.0 / SWA_HEADS))
    iq = lax.broadcasted_iota(jnp.int32, (bq, 2 * bq), 0)
    ik = lax.broadcasted_iota(jnp.int32, (bq, 2 * bq), 1)
    rel = bq + iq - ik
    in_window = (rel >= 0) & (rel <= SWA_SPAN)
    relf = rel.astype(F32)
    scale = V7X_LANES ** -0.5
    ones_blk = jnp.ones((2 * bq, V7X_LANES), BF16)

    def attend(q_rows, kcat, vcat, d, prev_ok):
        qb = (q_rows * scale).astype(BF16)
        s = _dot_nt(qb, kcat.astype(BF16)) - (slope * float(d)) * relf
        valid = in_window if prev_ok is True else in_window & ((ik >= bq) | prev_ok)
        s = jnp.where(valid, s, MASK_VALUE)
        m_b = jnp.max(s, axis=-1, keepdims=True)
        p = jnp.exp(s - m_b).astype(BF16)
        pv = _dot(p, jnp.concatenate([vcat.astype(BF16), ones_blk], axis=1))
        return jnp.broadcast_to(m_b, (bq, V7X_LANES)), pv[:, V7X_LANES:], pv[:, :V7X_LANES]

    def merge(old, new):
        (m_o, l_o, acc_o), (m_b, l_b, acc_b) = old, new
        m_n = jnp.maximum(m_o, m_b)
        alpha = jnp.exp(m_o - m_n)
        beta = jnp.exp(m_b - m_n)
        return m_n, alpha * l_o + beta * l_b, alpha * acc_o + beta * acc_b

    for r in range(d2):
        for n in range(qq // bq):
            q0 = r * qq + n * bq
            k0 = 2 * r * qq + qq + (n - 1) * bq
            new = attend(qa_ref[q0:q0 + bq, :], ka_ref[k0:k0 + 2 * bq, :], va_ref[k0:k0 + 2 * bq, :],
                         d2, has_prev if n == 0 else True)
            for kk in range(3):
                sa_ref[kk, q0:q0 + bq, :] = new[kk]

    take = lambda ref, start: ref[pl.ds(start, bq, stride=st), :]
    for r3 in range(d3):
        for n in range(tq // (bq * d3)):
            r = r3 % d2
            p0 = r3 // d2 + n * (bq * st)
            q0 = r * qq + p0
            k0 = 2 * r * qq + qq + p0
            kcat = jnp.concatenate([take(ka_ref, k0 - bq * st), take(ka_ref, k0)], axis=0)
            vcat = jnp.concatenate([take(va_ref, k0 - bq * st), take(va_ref, k0)], axis=0)
            new = attend(take(qa_ref, q0), kcat, vcat, d3, has_prev if n == 0 else True)
            out = merge(tuple(take(sa_ref.at[kk], q0) for kk in range(3)), new)
            for kk in range(3):
                sa_ref.at[kk][pl.ds(q0, bq, stride=st), :] = out[kk]

    for r in range(d2):
        for kk in range(3):
            sn_ref.at[kk][pl.ds(r, qq, stride=d2), :] = sa_ref[kk, r * qq:(r + 1) * qq, :]

    for n in range(tq // bq):
        base = n * bq
        if n == 0:
            kcat = jnp.concatenate([kp_ref[0, tq - bq:tq, :], kc_ref[0, 0:bq, :]], axis=0)
            vcat = jnp.concatenate([vp_ref[0, tq - bq:tq, :], vc_ref[0, 0:bq, :]], axis=0)
        else:
            kcat, vcat = kc_ref[0, base - bq:base + bq, :], vc_ref[0, base - bq:base + bq, :]
        new = attend(q_ref[0, base:base + bq, :], kcat, vcat, d1, has_prev if n == 0 else True)
        _, l_f, acc_f = merge(tuple(sn_ref[kk, base:base + bq, :] for kk in range(3)), new)
        o_ref[0, base:base + bq, :] = (acc_f / l_f).astype(o_ref.dtype)


def _dilated_mixer(proj, *, q_col, k_col, v_col, tq=2048):
    b, t, _ = proj.shape
    assert t % tq == 0 and tq % (SWA_BLOCK * max(SWA_DILATIONS)) == 0
    hh = SWA_HEADS
    cur = lambda off: pl.BlockSpec((1, tq, V7X_LANES), lambda bi, h, ti, off=off: (bi, ti, off + h))
    prev = lambda off: pl.BlockSpec((1, tq, V7X_LANES),
                                    lambda bi, h, ti, off=off: (bi, jnp.maximum(ti - 1, 0), off + h))
    return pl.pallas_call(
        _dilated_kernel,
        grid=(b, hh, t // tq),
        in_specs=[cur(q_col), cur(k_col), prev(k_col), cur(v_col), prev(v_col)],
        out_specs=pl.BlockSpec((1, tq, V7X_LANES), lambda bi, h, ti: (bi, ti, h)),
        out_shape=jax.ShapeDtypeStruct((b, t, hh * V7X_LANES), BF16),
        scratch_shapes=[pltpu.VMEM((tq, V7X_LANES), F32), pltpu.VMEM((2 * tq, V7X_LANES), F32),
                        pltpu.VMEM((2 * tq, V7X_LANES), F32), pltpu.VMEM((3, tq, V7X_LANES), F32),
                        pltpu.VMEM((3, tq, V7X_LANES), F32)],
        compiler_params=_cparams("parallel", "parallel", "parallel"),
        name="dilated_mixer",
    )(proj, proj, proj, proj, proj)


def _retention_kernel(q_ref, k_ref, v_ref, g_ref, o_ref, s_ref):
    h = pl.program_id(1)
    tt = q_ref.shape[1]
    c = min(RET_CHUNK, tt)
    nch = tt // c

    @pl.when(pl.program_id(2) == 0)
    def _():
        s_ref[...] = jnp.zeros_like(s_ref)

    hf = jnp.full((1, 1), h, jnp.int32).astype(F32)
    log_gamma = jnp.log1p(-jnp.exp2(-5.0 - hf))
    row = lax.broadcasted_iota(jnp.int32, (c, c), 0)
    col = lax.broadcasted_iota(jnp.int32, (c, c), 1)
    rel = (row - col).astype(F32)
    dmask = jnp.where(rel >= 0, jnp.exp(jnp.maximum(rel, 0.0) * log_gamma), 0.0)
    idx = lax.broadcasted_iota(jnp.int32, (c, 1), 0).astype(F32)
    q_scale = jnp.exp((idx + 1.0) * log_gamma)
    k_scale = jnp.exp((c - 1.0 - idx) * log_gamma)
    chunk_decay = jnp.exp(float(c) * log_gamma)

    sls = [slice(ci * c, (ci + 1) * c) for ci in range(nch)]
    qs = [q_ref[0, sl, :] for sl in sls]
    ks = [k_ref[0, sl, :] * (V7X_LANES ** -0.5) for sl in sls]
    vs = [v_ref[0, sl, :].astype(BF16) for sl in sls]
    intras = [_dot((_dot_nt(q.astype(BF16), k.astype(BF16)) * dmask).astype(BF16), v)
              for q, k, v in zip(qs, ks, vs)]
    kvs = [_dot_tn((k * k_scale).astype(BF16), v) for k, v in zip(ks, vs)]
    states = [s_ref[...]]
    for kv in kvs:
        states.append(states[-1] * chunk_decay + kv)
    s_ref[...] = states[-1]
    for sl, q, intra, state in zip(sls, qs, intras, states):
        o = intra + _dot((q * q_scale).astype(BF16), state.astype(BF16))
        mu = jnp.mean(o, axis=-1, keepdims=True)
        oc = o - mu
        o = oc * lax.rsqrt(jnp.mean(oc * oc, axis=-1, keepdims=True) + GN_EPS)
        o_ref[0, sl, :] = (o * _silu(g_ref[0, sl, :])).astype(o_ref.dtype)


def _retention_mixer(proj, *, q_col, k_col, v_col, g_col):
    b, t, _ = proj.shape
    c = _tile(t, RET_TILE, RET_CHUNK)
    hh = RET_HEADS
    wide = RET_DV // V7X_LANES
    assert v_col % wide == 0 and g_col % wide == 0
    nar = lambda off: pl.BlockSpec((1, c, V7X_LANES), lambda bi, h, ti, off=off: (bi, ti, off + h))
    wid = lambda off: pl.BlockSpec((1, c, RET_DV), lambda bi, h, ti, off=off: (bi, ti, off // wide + h))
    return pl.pallas_call(
        _retention_kernel,
        grid=(b, hh, t // c),
        in_specs=[nar(q_col), nar(k_col), wid(v_col), wid(g_col)],
        out_specs=pl.BlockSpec((1, c, RET_DV), lambda bi, h, ti: (bi, ti, h)),
        out_shape=jax.ShapeDtypeStruct((b, t, hh * RET_DV), BF16),
        scratch_shapes=[pltpu.VMEM((V7X_LANES, RET_DV), F32)],
        compiler_params=_cparams("parallel", "parallel", "arbitrary"),
        name="retention_mixer",
    )(proj, proj, proj, proj)


def _even_w_in_layout(w):
    d = w.shape[0]
    qkvz = 4 * DN_HEADS * V7X_LANES
    nba = 2 * DN_HEADS
    pad = jnp.zeros((d, 2 * V7X_LANES - nba), w.dtype)
    return jnp.concatenate([w[:, :qkvz], w[:, qkvz + nba:], w[:, qkvz:qkvz + nba], pad], axis=1)


def kernel(x, p, ln_mix_w, ln_mlp_w, ln_ple_w, w_up, w_down, w_ple_proj, w_ple_gate, ln_final_w,
           ev_w_in, ev_w_out, dn_conv_w, dn_a_log, dn_dt_bias, dn_norm_w,
           lru_conv_w, lru_conv_b, lru_wa, lru_ba, lru_wx, lru_bx, lru_lambda,
           od_w_in, od_w_out):
    b, t, d = x.shape
    depth = ln_mix_w.shape[0]
    m = b * t
    h = x.reshape(m, d)
    w_ple_proj, w_ple_gate, ev_w_in, ev_w_out, od_w_in, od_w_out = (
        _to_bf16(w) for w in (w_ple_proj, w_ple_gate, ev_w_in, ev_w_out, od_w_in, od_w_out))
    w_up = _to_bf16(w_up, tn=MLP_UP_TN)
    w_down = _to_bf16(w_down)
    p = p.reshape(depth, m, -1)
    hn, mix_norm_w = h, ln_mix_w[0]
    for i in range(depth):
        j = i // 2
        if i % 2 == 0:
            w_in = _even_w_in_layout(ev_w_in[j])
            proj = _norm_matmul(hn, mix_norm_w, w_in, act=None, out_dtype=F32)
            proj = proj.reshape(b, t, -1)
            y_a = _delta_mixer(proj, dn_conv_w[j], dn_a_log[j], dn_dt_bias[j], dn_norm_w[j])
            y_b = _lru_mixer(proj, lru_conv_w[j], lru_conv_b[j], lru_wa[j], lru_ba[j], lru_wx[j], lru_bx[j],
                             lru_lambda[j], x_col=4 * DN_HEADS, y_col=4 * DN_HEADS + LRU_BLOCKS)
            w_out = ev_w_out
        else:
            proj = _norm_matmul(hn, mix_norm_w, od_w_in, layer=j, act=None, out_dtype=F32)
            proj = proj.reshape(b, t, -1)
            y_a = _dilated_mixer(proj, q_col=0, k_col=SWA_HEADS, v_col=2 * SWA_HEADS)
            y_b = _retention_mixer(proj, q_col=3 * SWA_HEADS, k_col=3 * SWA_HEADS + RET_HEADS,
                                   v_col=3 * SWA_HEADS + 2 * RET_HEADS,
                                   g_col=3 * SWA_HEADS + 2 * RET_HEADS + RET_HEADS * RET_DV // V7X_LANES)
            w_out = od_w_out
        h, hn = _matmul_residual([y_a.reshape(m, -1), y_b.reshape(m, -1)], w_out, h, layer=j,
                                 next_norm_w=ln_mlp_w[i], tn_cap=d)
        up = _norm_matmul(hn, None, w_up, layer=i, act="relu2", out_dtype=BF16, tm_cap=MLP_UP_TM)
        h = _matmul_residual_ksplit(up, w_down, h, layer=i)
        if i == depth - 1:
            return _ple(h, ln_ple_w[i], w_ple_gate, p, w_ple_proj, ln_final_w, layer=i,
                        final_norm=True).reshape(b, t, d)
        h, hn = _ple(h, ln_ple_w[i], w_ple_gate, p, w_ple_proj, ln_mix_w[i + 1], layer=i, final_norm=False)
        mix_norm_w = None
```

```python
import functools

import jax
import jax.numpy as jnp
from jax import lax
from jax.experimental import pallas as pl
from jax.experimental.pallas import tpu as pltpu

F32 = jnp.float32
BF16 = jnp.bfloat16

V7X_LANES = 128
V7X_SUBLANES = 8
V7X_VMEM_LIMIT_BYTES = 56 * 1024 * 1024

NORM_EPS = 1e-6
GN_EPS = 1e-5
LRU_C = 8.0
DN_HEADS = 8
LRU_BLOCKS = 8
LRU_BLOCKS_PER_STEP = 4
SWA_HEADS = 8
RET_HEADS = 4
RET_DV = 256
SWA_DILATIONS = (1, 4, 16)
SWA_SPAN = 128
SWA_BLOCK = 128
DN_CHUNK = 128
DN_HEADS_PER_STEP = 4
RET_CHUNK = 256
RET_TILE = 1024
MASK_VALUE = -1e30
MLP_UP_TN = 2048
MLP_UP_TM = 1024


def _cparams(*sem):
    return pltpu.CompilerParams(dimension_semantics=sem, vmem_limit_bytes=V7X_VMEM_LIMIT_BYTES)


def _dot(a, b):
    return jnp.dot(a, b, preferred_element_type=F32)


def _dot_nt(a, b):
    return lax.dot_general(a, b, (((1,), (1,)), ((), ())), preferred_element_type=F32)


def _dot_tn(a, b):
    return lax.dot_general(a, b, (((0,), (0,)), ((), ())), preferred_element_type=F32)


def _split_bf16(x):
    hi = x.astype(BF16)
    lo = (x - hi.astype(F32)).astype(BF16)
    return hi, lo


def _dot_f32(a, b):
    ah, al = a if isinstance(a, tuple) else _split_bf16(a)
    bh, bl = b if isinstance(b, tuple) else _split_bf16(b)
    n = bh.shape[1]
    wide = _dot(ah, jnp.concatenate([bh, bl], axis=1))
    return wide[:, :n] + (wide[:, n:] + _dot(al, bh))


def _rms(x, w):
    return x * lax.rsqrt(jnp.mean(x * x, axis=-1, keepdims=True) + NORM_EPS) * w


def _silu(x):
    return x * jax.nn.sigmoid(x)


def _tile(n, cap, quantum):
    if n <= cap:
        return n
    best = None
    for c in range(quantum, cap + 1, quantum):
        if n % c == 0:
            best = c
    assert best is not None, (n, cap, quantum)
    return best


CAST_BLOCK_BYTES = 8 * 1024 * 1024


def _cast_kernel(x_ref, o_ref):
    o_ref[...] = x_ref[...].astype(o_ref.dtype)


def _to_bf16(w, tn=None):
    l, r, c = w.shape
    tn = c if tn is None else tn
    tr = _tile(r, max(16, CAST_BLOCK_BYTES // (4 * tn) // 16 * 16), 16)
    out = pl.pallas_call(
        _cast_kernel,
        grid=(l, c // tn, r // tr),
        in_specs=[pl.BlockSpec((None, tr, tn), lambda i, j, k: (i, k, j))],
        out_specs=pl.BlockSpec((None, None, tr, tn), lambda i, j, k: (i, j, k, 0)),
        out_shape=jax.ShapeDtypeStruct((l, c // tn, r, tn), BF16),
        compiler_params=_cparams("parallel", "parallel", "parallel"),
        name="to_bf16",
    )(w)
    return out if tn != c else out.reshape(l, r, c)


def _w_spec(w, layer, kk, tn, row_block=0):
    if w.ndim == 2:
        return pl.BlockSpec((kk, tn), lambda i, j: (row_block, j))
    if w.ndim == 3:
        return pl.BlockSpec((None, kk, tn), lambda i, j: (layer, row_block, j))
    assert w.shape[3] == tn, (w.shape, tn)
    return pl.BlockSpec((None, None, kk, tn), lambda i, j: (layer, j, row_block, 0))


def _w_cols(w):
    return w.shape[-1] if w.ndim < 4 else w.shape[1] * w.shape[3]


def _norm_mm_kernel(x_ref, g_ref, w_ref, o_ref, hn_ref, *, act):
    @pl.when(pl.program_id(1) == 0)
    def _():
        hn_ref[...] = _rms(x_ref[...], g_ref[...]).astype(BF16)

    _mm_act_kernel(hn_ref, w_ref, o_ref, act=act)


def _mm_act_kernel(x_ref, w_ref, o_ref, *, act):
    a = _dot(x_ref[...], w_ref[...])
    if act == "relu2":
        a = jnp.square(jnp.maximum(a, 0.0))
    o_ref[...] = a.astype(o_ref.dtype)


def _norm_matmul(x, g, w, *, act, out_dtype, layer=0, tm_cap=1024, tn_cap=1536):
    m, k = x.shape
    n = _w_cols(w)
    tm = _tile(m, tm_cap, 256)
    tn = w.shape[3] if w.ndim == 4 else _tile(n, tn_cap, 256)
    x_spec = pl.BlockSpec((tm, k), lambda i, j: (i, 0))
    common = dict(
        grid=(m // tm, n // tn),
        out_specs=pl.BlockSpec((tm, tn), lambda i, j: (i, j)),
        out_shape=jax.ShapeDtypeStruct((m, n), out_dtype),
    )
    if g is None:
        assert x.dtype == BF16
        return pl.pallas_call(
            functools.partial(_mm_act_kernel, act=act),
            in_specs=[x_spec, _w_spec(w, layer, k, tn)],
            compiler_params=_cparams("parallel", "parallel"),
            name="matmul_act", **common,
        )(x, w)
    return pl.pallas_call(
        functools.partial(_norm_mm_kernel, act=act),
        in_specs=[x_spec, pl.BlockSpec((1, k), lambda i, j: (0, 0)), _w_spec(w, layer, k, tn)],
        scratch_shapes=[pltpu.VMEM((tm, k), BF16)],
        compiler_params=_cparams("parallel", "arbitrary"),
        name="norm_matmul", **common,
    )(x, g.reshape(1, k), w)


def _mm_res_kernel(*refs, n_x, with_norm):
    xs, ws = refs[:n_x], refs[n_x:2 * n_x]
    res_ref = refs[2 * n_x]
    outs = refs[2 * n_x + 1:]
    acc = res_ref[...]
    for x_ref, w_ref in zip(xs, ws):
        acc = acc + _dot(x_ref[...], w_ref[...])
    if with_norm:
        g_ref, o_ref, hn_ref = outs
        hn_ref[...] = _rms(acc, g_ref[...]).astype(BF16)
    else:
        o_ref, = outs
    o_ref[...] = acc


def _matmul_residual(xs, w, res, *, layer=0, next_norm_w=None, tm_cap=512, tn_cap=512):
    m, n = res.shape
    assert n == _w_cols(w)
    tm = _tile(m, tm_cap, 256)
    tn = w.shape[3] if w.ndim == 4 else _tile(n, tn_cap, 256)
    with_norm = next_norm_w is not None
    assert not with_norm or tn == n
    in_specs, row = [], 0
    for x in xs:
        in_specs.append(pl.BlockSpec((tm, x.shape[1]), lambda i, j: (i, 0)))
    w_specs = []
    for x in xs:
        kk = x.shape[1]
        assert row % kk == 0
        w_specs.append(_w_spec(w, layer, kk, tn, row_block=row // kk))
        row += kk
    assert row == w.shape[-2]
    tile = pl.BlockSpec((tm, tn), lambda i, j: (i, j))
    in_specs = in_specs + w_specs + [tile]
    args = [*xs, *([w] * len(xs)), res]
    out_specs, out_shape = tile, jax.ShapeDtypeStruct((m, n), F32)
    if with_norm:
        in_specs.append(pl.BlockSpec((1, n), lambda i, j: (0, 0)))
        args.append(next_norm_w.reshape(1, n))
        out_specs, out_shape = [tile, tile], [out_shape, jax.ShapeDtypeStruct((m, n), BF16)]
    return pl.pallas_call(
        functools.partial(_mm_res_kernel, n_x=len(xs), with_norm=with_norm),
        grid=(m // tm, n // tn),
        in_specs=in_specs,
        out_specs=out_specs,
        out_shape=out_shape,
        compiler_params=_cparams("parallel", "parallel"),
        name="matmul_residual",
    )(*args)


def _mm_res_ksplit_kernel(x_ref, w_ref, res_ref, o_ref):
    @pl.when(pl.program_id(1) == 0)
    def _():
        o_ref[...] = res_ref[...]

    o_ref[...] += _dot(x_ref[...], w_ref[...])


def _matmul_residual_ksplit(x, w, res, *, layer, tm_cap=1024, tk_cap=1024):
    m, n = res.shape
    k = x.shape[1]
    tm = _tile(m, tm_cap, 256)
    tk = _tile(k, tk_cap, 256)
    return pl.pallas_call(
        _mm_res_ksplit_kernel,
        grid=(m // tm, k // tk),
        in_specs=[
            pl.BlockSpec((tm, tk), lambda i, kk: (i, kk)),
            pl.BlockSpec((None, tk, n), lambda i, kk: (layer, kk, 0)),
            pl.BlockSpec((tm, n), lambda i, kk: (i, 0)),
        ],
        out_specs=pl.BlockSpec((tm, n), lambda i, kk: (i, 0)),
        out_shape=jax.ShapeDtypeStruct((m, n), F32),
        compiler_params=_cparams("parallel", "arbitrary"),
        name="matmul_residual_ksplit",
    )(x, w, res)


def _ple_kernel(x_ref, g_ref, wg_ref, p_ref, wp_ref, gt_ref, o_ref, *hn_refs, final_norm):
    d = x_ref.shape[1]
    hn = _rms(x_ref[...], g_ref[...]).astype(BF16)
    pb = p_ref[...].astype(BF16)
    nh = d // 2 if d % (2 * V7X_LANES) == 0 else d
    for c0 in range(0, d, nh):
        gate = jax.nn.sigmoid(_dot(hn, wg_ref[:, c0:c0 + nh]))
        pp = _dot(pb, wp_ref[:, c0:c0 + nh])
        o_ref[:, c0:c0 + nh] = x_ref[:, c0:c0 + nh] + gate * pp
    tail = _rms(o_ref[...], gt_ref[...])
    if final_norm:
        o_ref[...] = tail
    else:
        hn_refs[0][...] = tail.astype(BF16)


def _ple(x, g, wg, p, wp, g_tail, *, layer, final_norm, tm_cap=512):
    m, d = x.shape
    t, pd = p.shape[2], p.shape[3]
    tm = _tile(t, tm_cap, 256)
    tiles_per_seq = t // tm
    row_tile = pl.BlockSpec((tm, d), lambda i: (i, 0))
    out_specs, out_shape = row_tile, jax.ShapeDtypeStruct((m, d), F32)
    if not final_norm:
        out_specs, out_shape = [row_tile, row_tile], [out_shape, jax.ShapeDtypeStruct((m, d), BF16)]
    return pl.pallas_call(
        functools.partial(_ple_kernel, final_norm=final_norm),
        grid=(m // tm,),
        in_specs=[
            row_tile,
            pl.BlockSpec((1, d), lambda i: (0, 0)),
            pl.BlockSpec((None, d, d), lambda i: (layer, 0, 0)),
            pl.BlockSpec((None, None, tm, pd), lambda i: (layer, i // tiles_per_seq, i % tiles_per_seq, 0)),
            pl.BlockSpec((None, pd, d), lambda i: (layer, 0, 0)),
            pl.BlockSpec((1, d), lambda i: (0, 0)),
        ],
        out_specs=out_specs,
        out_shape=out_shape,
        compiler_params=_cparams("parallel"),
        name="ple",
    )(x, g.reshape(1, d), wg, p, wp, g_tail.reshape(1, d))


def _causal_conv(x_ref, halo_ref, w_ref, first):
    x = x_ref[0]
    halo = jnp.where(first, 0.0, halo_ref[0])
    xp = jnp.concatenate([halo, x], axis=0)
    w = w_ref[...]
    y = pltpu.roll(xp, 3, axis=0)[V7X_SUBLANES:] * w[0:1]
    y = y + pltpu.roll(xp, 2, axis=0)[V7X_SUBLANES:] * w[1:2]
    y = y + pltpu.roll(xp, 1, axis=0)[V7X_SUBLANES:] * w[2:3]
    return y + x * w[3:4]


def _halo_map(col, rows_per_tile):
    nb = rows_per_tile // V7X_SUBLANES
    return lambda b, h, t: (b, jnp.maximum(t * nb - 1, 0), col(h))


def _unit_lower_inverses(ms, row, col):
    c = ms[0].shape[0]
    eye = (row == col).astype(F32)
    diag16 = (row >> 4) == (col >> 4)
    b16 = lambda xs: [x.astype(BF16) for x in xs]
    ns = [jnp.where(diag16, m, 0.0) for m in ms]
    invs = [eye - n for n in ns]
    qs = b16(ns)
    for step in range(3):
        qs = b16([_dot(q, q) for q in qs])
        invs = [inv + _dot(inv.astype(BF16), q) for inv, q in zip(invs, qs)]
    shift = 4
    while (1 << shift) < c:
        off = ((row >> (shift + 1)) == (col >> (shift + 1))) & ((row >> shift) != (col >> shift))
        inv16 = b16(invs)
        tmps = [_dot(inv, jnp.where(off, m, 0.0).astype(BF16)) for inv, m in zip(inv16, ms)]
        invs = [inv - _dot(tmp.astype(BF16), i16) for inv, tmp, i16 in zip(invs, tmps, inv16)]
        shift += 1
    xs = [_split_bf16(inv) for inv in invs]
    res = [eye - inv - _dot_f32(m, x) for inv, m, x in zip(invs, ms, xs)]
    return [inv + _dot_f32(x, r) for inv, x, r in zip(invs, xs, res)]


def _delta_kernel(q_ref, qh_ref, k_ref, kh_ref, v_ref, vh_ref, z_ref, ba_ref,
                  wq_ref, wk_ref, wv_ref, alog_ref, dtb_ref, nw_ref, o_ref, s_ref):
    hp = pl.program_id(1)
    first = pl.program_id(2) == 0
    tt = q_ref.shape[1]
    c = DN_CHUNK
    dh = V7X_LANES
    nheads = q_ref.shape[2] // dh
    nch = tt // c

    @pl.when(first)
    def _():
        s_ref[...] = jnp.zeros_like(s_ref)

    def l2n(x):
        return x * lax.rsqrt(jnp.sum(x * x, axis=-1, keepdims=True) + 1e-6)

    ba = ba_ref[0]
    lane = lax.broadcasted_iota(jnp.int32, ba.shape, 1)
    sig_ba = jax.nn.sigmoid(ba)
    g_all = -jnp.exp(alog_ref[...]) * jax.nn.softplus(ba + dtb_ref[...])

    row = lax.broadcasted_iota(jnp.int32, (c, c), 0)
    col = lax.broadcasted_iota(jnp.int32, (c, c), 1)
    causal = row >= col
    strict = row > col
    ltri = causal.astype(BF16)

    def cumsum_block(gblk):
        g_hi, g_lo = _split_bf16(gblk)
        g_lo2 = (gblk - g_hi.astype(F32) - g_lo.astype(F32)).astype(BF16)
        return _dot(ltri, g_hi) + (_dot(ltri, g_lo) + _dot(ltri, g_lo2))

    gc_blocks = [cumsum_block(g_all[ci * c:(ci + 1) * c]) for ci in range(nch)]
    gc_blocks_t = [gcb.T for gcb in gc_blocks]

    q_all = _silu(_causal_conv(q_ref, qh_ref, wq_ref, first))
    k_all = _silu(_causal_conv(k_ref, kh_ref, wk_ref, first))
    v_all = _silu(_causal_conv(v_ref, vh_ref, wv_ref, first))
    qs, ks, vs, bs, gcs, gc_rows = [], [], [], [], [], []
    for hi in range(nheads):
        hs = slice(hi * dh, (hi + 1) * dh)
        head = nheads * hp + hi
        qh = l2n(q_all[:, hs]) * (dh ** -0.5)
        kh = l2n(k_all[:, hs])
        vh = v_all[:, hs]
        beta = jnp.sum(jnp.where(lane == head, sig_ba, 0.0), axis=-1, keepdims=True)
        for ci in range(nch):
            sl = slice(ci * c, (ci + 1) * c)
            qs.append(qh[sl]); ks.append(kh[sl]); vs.append(vh[sl]); bs.append(beta[sl])
            gcol = jnp.sum(jnp.where(col == head + DN_HEADS, gc_blocks[ci], 0.0), axis=-1, keepdims=True)
            grow = jnp.sum(jnp.where(row == head + DN_HEADS, gc_blocks_t[ci], 0.0), axis=0, keepdims=True)
            gcs.append(jnp.broadcast_to(gcol, (c, c)))
            gc_rows.append(jnp.broadcast_to(grow, (c, c)))

    decays = [jnp.where(causal, jnp.exp(jnp.where(causal, gc - gr, 0.0)), 0.0) for gc, gr in zip(gcs, gc_rows)]
    kbs = [kc * bc for kc, bc in zip(ks, bs)]
    k16 = [kc.astype(BF16) for kc in ks]
    ms = [jnp.where(strict, _dot_nt(kb.astype(BF16), kc) * dec, 0.0) for kb, kc, dec in zip(kbs, k16, decays)]
    invs = _unit_lower_inverses(ms, row, col)
    egcs = [jnp.exp(gc) for gc in gcs]
    sols = [_dot_f32(inv, jnp.concatenate([vc * bc, kb * egc], axis=1))
            for inv, vc, bc, kb, egc in zip(invs, vs, bs, kbs, egcs)]
    us = [sol[:, :dh] for sol in sols]
    ws = [sol[:, dh:].astype(BF16) for sol in sols]
    qks = [(_dot_nt(qc.astype(BF16), kc) * dec).astype(BF16) for qc, kc, dec in zip(qs, k16, decays)]
    q_decs = [(qc * egc).astype(BF16) for qc, egc in zip(qs, egcs)]
    lasts = [gc[c - 1:c, :] for gc in gcs]
    k_decs = [(kc * jnp.exp(last - gc)).astype(BF16) for kc, last, gc in zip(ks, lasts, gcs)]
    g_tots = [jnp.exp(last) for last in lasts]

    states = [s_ref[hi] for hi in range(nheads)]
    for ci in range(nch):
        sl = slice(ci * c, (ci + 1) * c)
        for hi in range(nheads):
            i = hi * nch + ci
            hs = slice(hi * dh, (hi + 1) * dh)
            sb = states[hi].astype(BF16)
            v_new = (us[i] - _dot(ws[i], sb)).astype(BF16)
            o = _dot(q_decs[i], sb) + _dot(qks[i], v_new)
            states[hi] = states[hi] * g_tots[i] + _dot_tn(k_decs[i], v_new)
            o_ref[0, sl, hs] = (_rms(o, nw_ref[...]) * _silu(z_ref[0, sl, hs])).astype(o_ref.dtype)
    for hi in range(nheads):
        s_ref[hi] = states[hi]


def _delta_mixer(proj, conv_w, a_log, dt_bias, norm_w, *, tt_cap=512):
    b, t, _ = proj.shape
    tt = _tile(t, tt_cap, DN_CHUNK)
    hh = DN_HEADS
    hps = DN_HEADS_PER_STEP
    wd = hps * V7X_LANES
    npair = hh // hps
    blk = lambda off: pl.BlockSpec((1, tt, wd), lambda bi, h, ti, off=off: (bi, ti, off + h))
    halo = lambda off: pl.BlockSpec((1, V7X_SUBLANES, wd), _halo_map(lambda h, off=off: off + h, tt))
    cw = lambda off: pl.BlockSpec((4, wd), lambda bi, h, ti, off=off: (0, off + h))
    row_spec = pl.BlockSpec((1, V7X_LANES), lambda bi, h, ti: (0, 0))
    pad = jnp.zeros((V7X_LANES - 2 * hh,), F32)
    alog_row = jnp.concatenate([jnp.zeros((hh,), F32), a_log, pad]).reshape(1, V7X_LANES)
    dtb_row = jnp.concatenate([jnp.zeros((hh,), F32), dt_bias, pad]).reshape(1, V7X_LANES)
    ba_col = 6 * hh
    return pl.pallas_call(
        _delta_kernel,
        grid=(b, npair, t // tt),
        in_specs=[
            blk(0), halo(0), blk(npair), halo(npair), blk(2 * npair), halo(2 * npair), blk(3 * npair),
            pl.BlockSpec((1, tt, V7X_LANES), lambda bi, h, ti: (bi, ti, ba_col)),
            cw(0), cw(npair), cw(2 * npair), row_spec, row_spec, row_spec,
        ],
        out_specs=pl.BlockSpec((1, tt, wd), lambda bi, h, ti: (bi, ti, h)),
        out_shape=jax.ShapeDtypeStruct((b, t, hh * V7X_LANES), BF16),
        scratch_shapes=[pltpu.VMEM((hps, V7X_LANES, V7X_LANES), F32)],
        compiler_params=_cparams("parallel", "parallel", "arbitrary"),
        name="delta_mixer",
    )(proj, proj, proj, proj, proj, proj, proj, proj, conv_w, conv_w, conv_w,
      alog_row, dtb_row, norm_w.reshape(1, V7X_LANES))


def _lru_kernel(x_ref, xh_ref, y_ref, cw_ref, cb_ref, wa_ref, ba_ref, wx_ref, bx_ref, lam_ref,
                o_ref, h_ref):
    first = pl.program_id(2) == 0
    tt = x_ref.shape[1]

    @pl.when(first)
    def _():
        h_ref[...] = jnp.zeros_like(h_ref)

    wc = x_ref.shape[2]
    xc = _causal_conv(x_ref, xh_ref, cw_ref, first) + cb_ref[...]
    xb = xc.astype(BF16)

    def block_diag(w_ref):
        return jnp.concatenate([_dot(xb[:, g * V7X_LANES:(g + 1) * V7X_LANES], w_ref[g])
                                for g in range(wc // V7X_LANES)], axis=1)

    r = jax.nn.sigmoid(block_diag(wa_ref) + ba_ref[...])
    i = jax.nn.sigmoid(block_diag(wx_ref) + bx_ref[...])
    log_a = -LRU_C * r * jax.nn.softplus(-lam_ref[...])
    a = jnp.exp(log_a)
    u = jnp.sqrt(1.0 - a * a) * (i * xc)

    rowi = lax.broadcasted_iota(jnp.int32, (tt, wc), 0)
    s = 1
    while s < tt:
        if s < V7X_SUBLANES:
            a_sh = jnp.where(rowi >= s, pltpu.roll(a, s, axis=0), 1.0)
            u_sh = jnp.where(rowi >= s, pltpu.roll(u, s, axis=0), 0.0)
        else:
            a_sh = jnp.concatenate([jnp.ones((s, wc), F32), a[:tt - s]], axis=0)
            u_sh = jnp.concatenate([jnp.zeros((s, wc), F32), u[:tt - s]], axis=0)
        u = a * u_sh + u
        a = a * a_sh
        s *= 2
    hs = u + a * h_ref[...]
    h_ref[...] = hs[tt - 1:tt]
    o_ref[0] = (hs * jax.nn.gelu(y_ref[0])).astype(o_ref.dtype)


def _lru_mixer(proj, conv_w, conv_b, wa, ba, wx, bx, lam, *, x_col, y_col, tt_cap=512):
    b, t, _ = proj.shape
    tt = _tile(t, tt_cap, 16)
    gg = LRU_BLOCKS
    gs = LRU_BLOCKS_PER_STEP
    wc = gs * V7X_LANES
    assert gg % gs == 0 and x_col % gs == 0 and y_col % gs == 0
    vec = lambda a: a.reshape(1, gg * V7X_LANES)
    vspec = pl.BlockSpec((1, wc), lambda bi, g, ti: (0, g))
    wspec = pl.BlockSpec((gs, V7X_LANES, V7X_LANES), lambda bi, g, ti: (g, 0, 0))
    return pl.pallas_call(
        _lru_kernel,
        grid=(b, gg // gs, t // tt),
        in_specs=[
            pl.BlockSpec((1, tt, wc), lambda bi, g, ti: (bi, ti, x_col // gs + g)),
            pl.BlockSpec((1, V7X_SUBLANES, wc), _halo_map(lambda g: x_col // gs + g, tt)),
            pl.BlockSpec((1, tt, wc), lambda bi, g, ti: (bi, ti, y_col // gs + g)),
            pl.BlockSpec((4, wc), lambda bi, g, ti: (0, g)),
            vspec, wspec, vspec, wspec, vspec, vspec,
        ],
        out_specs=pl.BlockSpec((1, tt, wc), lambda bi, g, ti: (bi, ti, g)),
        out_shape=jax.ShapeDtypeStruct((b, t, gg * V7X_LANES), BF16),
        scratch_shapes=[pltpu.VMEM((1, wc), F32)],
        compiler_params=_cparams("parallel", "parallel", "arbitrary"),
        name="lru_mixer",
    )(proj, proj, proj, conv_w, vec(conv_b), wa.astype(BF16), vec(ba), wx.astype(BF16), vec(bx), vec(lam))


def _dilated_kernel(q_ref, kc_ref, kp_ref, vc_ref, vp_ref, o_ref, qa_ref, ka_ref, va_ref, sa_ref, sn_ref):
    h = pl.program_id(1)
    has_prev = pl.program_id(2) > 0
    tq = q_ref.shape[1]
    bq = SWA_BLOCK
    d1, d2, d3 = SWA_DILATIONS
    assert d1 == 1 and d3 % d2 == 0
    qq = tq // d2
    st = d3 // d2
    for r in range(d2):
        qa_ref[r * qq:(r + 1) * qq, :] = q_ref.at[0][pl.ds(r, qq, stride=d2), :]
        for src_p, src_c, dst in ((kp_ref, kc_ref, ka_ref), (vp_ref, vc_ref, va_ref)):
            dst[2 * r * qq:2 * r * qq + qq, :] = src_p.at[0][pl.ds(r, qq, stride=d2), :]
            dst[2 * r * qq + qq:2 * (r + 1) * qq, :] = src_c.at[0][pl.ds(r, qq, stride=d2), :]

    slope = jnp.exp2(-(jnp.full((1, 1), h, jnp.int32).astype(F32) + 1.0) * (8.0 / SWA_HEADS))
    iq = lax.broadcasted_iota(jnp.int32, (bq, 2 * bq), 0)
    ik = lax.broadcasted_iota(jnp.int32, (bq, 2 * bq), 1)
    rel = bq + iq - ik
    in_window = (rel >= 0) & (rel <= SWA_SPAN)
    relf = rel.astype(F32)
    scale = V7X_LANES ** -0.5
    ones_blk = jnp.ones((2 * bq, V7X_LANES), BF16)

    def attend(q_rows, kcat, vcat, d, prev_ok):
        qb = (q_rows * scale).astype(BF16)
        s = _dot_nt(qb, kcat.astype(BF16)) - (slope * float(d)) * relf
        valid = in_window if prev_ok is True else in_window & ((ik >= bq) | prev_ok)
        s = jnp.where(valid, s, MASK_VALUE)
        m_b = jnp.max(s, axis=-1, keepdims=True)
        p = jnp.exp(s - m_b).astype(BF16)
        pv = _dot(p, jnp.concatenate([vcat.astype(BF16), ones_blk], axis=1))
        return jnp.broadcast_to(m_b, (bq, V7X_LANES)), pv[:, V7X_LANES:], pv[:, :V7X_LANES]

    def merge(old, new):
        (m_o, l_o, acc_o), (m_b, l_b, acc_b) = old, new
        m_n = jnp.maximum(m_o, m_b)
        alpha = jnp.exp(m_o - m_n)
        beta = jnp.exp(m_b - m_n)
        return m_n, alpha * l_o + beta * l_b, alpha * acc_o + beta * acc_b

    for r in range(d2):
        for n in range(qq // bq):
            q0 = r * qq + n * bq
            k0 = 2 * r * qq + qq + (n - 1) * bq
            new = attend(qa_ref[q0:q0 + bq, :], ka_ref[k0:k0 + 2 * bq, :], va_ref[k0:k0 + 2 * bq, :],
                         d2, has_prev if n == 0 else True)
            for kk in range(3):
                sa_ref[kk, q0:q0 + bq, :] = new[kk]

    take = lambda ref, start: ref[pl.ds(start, bq, stride=st), :]
    for r3 in range(d3):
        for n in range(tq // (bq * d3)):
            r = r3 % d2
            p0 = r3 // d2 + n * (bq * st)
            q0 = r * qq + p0
            k0 = 2 * r * qq + qq + p0
            kcat = jnp.concatenate([take(ka_ref, k0 - bq * st), take(ka_ref, k0)], axis=0)
            vcat = jnp.concatenate([take(va_ref, k0 - bq * st), take(va_ref, k0)], axis=0)
            new = attend(take(qa_ref, q0), kcat, vcat, d3, has_prev if n == 0 else True)
            out = merge(tuple(take(sa_ref.at[kk], q0) for kk in range(3)), new)
            for kk in range(3):
                sa_ref.at[kk][pl.ds(q0, bq, stride=st), :] = out[kk]

    for r in range(d2):
        for kk in range(3):
            sn_ref.at[kk][pl.ds(r, qq, stride=d2), :] = sa_ref[kk, r * qq:(r + 1) * qq, :]

    for n in range(tq // bq):
        base = n * bq
        if n == 0:
            kcat = jnp.concatenate([kp_ref[0, tq - bq:tq, :], kc_ref[0, 0:bq, :]], axis=0)
            vcat = jnp.concatenate([vp_ref[0, tq - bq:tq, :], vc_ref[0, 0:bq, :]], axis=0)
        else:
            kcat, vcat = kc_ref[0, base - bq:base + bq, :], vc_ref[0, base - bq:base + bq, :]
        new = attend(q_ref[0, base:base + bq, :], kcat, vcat, d1, has_prev if n == 0 else True)
        _, l_f, acc_f = merge(tuple(sn_ref[kk, base:base + bq, :] for kk in range(3)), new)
        o_ref[0, base:base + bq, :] = (acc_f / l_f).astype(o_ref.dtype)


def _dilated_mixer(proj, *, q_col, k_col, v_col, tq=2048):
    b, t, _ = proj.shape
    assert t % tq == 0 and tq % (SWA_BLOCK * max(SWA_DILATIONS)) == 0
    hh = SWA_HEADS
    cur = lambda off: pl.BlockSpec((1, tq, V7X_LANES), lambda bi, h, ti, off=off: (bi, ti, off + h))
    prev = lambda off: pl.BlockSpec((1, tq, V7X_LANES),
                                    lambda bi, h, ti, off=off: (bi, jnp.maximum(ti - 1, 0), off + h))
    return pl.pallas_call(
        _dilated_kernel,
        grid=(b, hh, t // tq),
        in_specs=[cur(q_col), cur(k_col), prev(k_col), cur(v_col), prev(v_col)],
        out_specs=pl.BlockSpec((1, tq, V7X_LANES), lambda bi, h, ti: (bi, ti, h)),
        out_shape=jax.ShapeDtypeStruct((b, t, hh * V7X_LANES), BF16),
        scratch_shapes=[pltpu.VMEM((tq, V7X_LANES), F32), pltpu.VMEM((2 * tq, V7X_LANES), F32),
                        pltpu.VMEM((2 * tq, V7X_LANES), F32), pltpu.VMEM((3, tq, V7X_LANES), F32),
                        pltpu.VMEM((3, tq, V7X_LANES), F32)],
        compiler_params=_cparams("parallel", "parallel", "parallel"),
        name="dilated_mixer",
    )(proj, proj, proj, proj, proj)


def _retention_kernel(q_ref, k_ref, v_ref, g_ref, o_ref, s_ref):
    h = pl.program_id(1)
    tt = q_ref.shape[1]
    c = min(RET_CHUNK, tt)
    nch = tt // c

    @pl.when(pl.program_id(2) == 0)
    def _():
        s_ref[...] = jnp.zeros_like(s_ref)

    hf = jnp.full((1, 1), h, jnp.int32).astype(F32)
    log_gamma = jnp.log1p(-jnp.exp2(-5.0 - hf))
    row = lax.broadcasted_iota(jnp.int32, (c, c), 0)
    col = lax.broadcasted_iota(jnp.int32, (c, c), 1)
    rel = (row - col).astype(F32)
    dmask = jnp.where(rel >= 0, jnp.exp(jnp.maximum(rel, 0.0) * log_gamma), 0.0)
    idx = lax.broadcasted_iota(jnp.int32, (c, 1), 0).astype(F32)
    q_scale = jnp.exp((idx + 1.0) * log_gamma)
    k_scale = jnp.exp((c - 1.0 - idx) * log_gamma)
    chunk_decay = jnp.exp(float(c) * log_gamma)

    sls = [slice(ci * c, (ci + 1) * c) for ci in range(nch)]
    qs = [q_ref[0, sl, :] for sl in sls]
    ks = [k_ref[0, sl, :] * (V7X_LANES ** -0.5) for sl in sls]
    vs = [v_ref[0, sl, :].astype(BF16) for sl in sls]
    intras = [_dot((_dot_nt(q.astype(BF16), k.astype(BF16)) * dmask).astype(BF16), v)
              for q, k, v in zip(qs, ks, vs)]
    kvs = [_dot_tn((k * k_scale).astype(BF16), v) for k, v in zip(ks, vs)]
    states = [s_ref[...]]
    for kv in kvs:
        states.append(states[-1] * chunk_decay + kv)
    s_ref[...] = states[-1]
    for sl, q, intra, state in zip(sls, qs, intras, states):
        o = intra + _dot((q * q_scale).astype(BF16), state.astype(BF16))
        mu = jnp.mean(o, axis=-1, keepdims=True)
        oc = o - mu
        o = oc * lax.rsqrt(jnp.mean(oc * oc, axis=-1, keepdims=True) + GN_EPS)
        o_ref[0, sl, :] = (o * _silu(g_ref[0, sl, :])).astype(o_ref.dtype)


def _retention_mixer(proj, *, q_col, k_col, v_col, g_col):
    b, t, _ = proj.shape
    c = _tile(t, RET_TILE, RET_CHUNK)
    hh = RET_HEADS
    wide = RET_DV // V7X_LANES
    assert v_col % wide == 0 and g_col % wide == 0
    nar = lambda off: pl.BlockSpec((1, c, V7X_LANES), lambda bi, h, ti, off=off: (bi, ti, off + h))
    wid = lambda off: pl.BlockSpec((1, c, RET_DV), lambda bi, h, ti, off=off: (bi, ti, off // wide + h))
    return pl.pallas_call(
        _retention_kernel,
        grid=(b, hh, t // c),
        in_specs=[nar(q_col), nar(k_col), wid(v_col), wid(g_col)],
        out_specs=pl.BlockSpec((1, c, RET_DV), lambda bi, h, ti: (bi, ti, h)),
        out_shape=jax.ShapeDtypeStruct((b, t, hh * RET_DV), BF16),
        scratch_shapes=[pltpu.VMEM((V7X_LANES, RET_DV), F32)],
        compiler_params=_cparams("parallel", "parallel", "arbitrary"),
        name="retention_mixer",
    )(proj, proj, proj, proj)


def _even_w_in_kernel(x_ref, o_ref):
    qkvz = 4 * DN_HEADS * V7X_LANES
    nba = 2 * DN_HEADS
    lru = 2 * LRU_BLOCKS * V7X_LANES
    rows = x_ref.shape[0]
    o_ref[:, :qkvz] = x_ref[:, :qkvz].astype(BF16)
    o_ref[:, qkvz:qkvz + lru] = x_ref[:, qkvz + nba:qkvz + nba + lru].astype(BF16)
    tail = jnp.concatenate([x_ref[:, qkvz:qkvz + nba], jnp.zeros((rows, 2 * V7X_LANES - nba), F32)], axis=1)
    o_ref[:, qkvz + lru:] = tail.astype(BF16)


def _even_w_in_bf16(w):
    l, d, c = w.shape
    n = 4 * DN_HEADS * V7X_LANES + 2 * LRU_BLOCKS * V7X_LANES + 2 * V7X_LANES
    tr = _tile(d, 256, 16)
    return pl.pallas_call(
        _even_w_in_kernel,
        grid=(l, d // tr),
        in_specs=[pl.BlockSpec((None, tr, c), lambda i, j: (i, j, 0))],
        out_specs=pl.BlockSpec((None, tr, n), lambda i, j: (i, j, 0)),
        out_shape=jax.ShapeDtypeStruct((l, d, n), BF16),
        compiler_params=_cparams("parallel", "parallel"),
        name="even_w_in_bf16",
    )(w)


def kernel(x, p, ln_mix_w, ln_mlp_w, ln_ple_w, w_up, w_down, w_ple_proj, w_ple_gate, ln_final_w,
           ev_w_in, ev_w_out, dn_conv_w, dn_a_log, dn_dt_bias, dn_norm_w,
           lru_conv_w, lru_conv_b, lru_wa, lru_ba, lru_wx, lru_bx, lru_lambda,
           od_w_in, od_w_out):
    b, t, d = x.shape
    depth = ln_mix_w.shape[0]
    m = b * t
    h = x.reshape(m, d)
    w_ple_proj, w_ple_gate, ev_w_out, od_w_in, od_w_out = (
        _to_bf16(w) for w in (w_ple_proj, w_ple_gate, ev_w_out, od_w_in, od_w_out))
    ev_w_in = _even_w_in_bf16(ev_w_in)
    w_up = _to_bf16(w_up, tn=MLP_UP_TN)
    w_down = _to_bf16(w_down)
    hn, mix_norm_w = h, ln_mix_w[0]
    for i in range(depth):
        j = i // 2
        if i % 2 == 0:
            proj = _norm_matmul(hn, mix_norm_w, ev_w_in, layer=j, act=None, out_dtype=F32)
            proj = proj.reshape(b, t, -1)
            y_a = _delta_mixer(proj, dn_conv_w[j], dn_a_log[j], dn_dt_bias[j], dn_norm_w[j])
            y_b = _lru_mixer(proj, lru_conv_w[j], lru_conv_b[j], lru_wa[j], lru_ba[j], lru_wx[j], lru_bx[j],
                             lru_lambda[j], x_col=4 * DN_HEADS, y_col=4 * DN_HEADS + LRU_BLOCKS)
            w_out = ev_w_out
        else:
            proj = _norm_matmul(hn, mix_norm_w, od_w_in, layer=j, act=None, out_dtype=F32)
            proj = proj.reshape(b, t, -1)
            y_a = _dilated_mixer(proj, q_col=0, k_col=SWA_HEADS, v_col=2 * SWA_HEADS)
            y_b = _retention_mixer(proj, q_col=3 * SWA_HEADS, k_col=3 * SWA_HEADS + RET_HEADS,
                                   v_col=3 * SWA_HEADS + 2 * RET_HEADS,
                                   g_col=3 * SWA_HEADS + 2 * RET_HEADS + RET_HEADS * RET_DV // V7X_LANES)
            w_out = od_w_out
        h, hn = _matmul_residual([y_a.reshape(m, -1), y_b.reshape(m, -1)], w_out, h, layer=j,
                                 next_norm_w=ln_mlp_w[i], tn_cap=d)
        up = _norm_matmul(hn, None, w_up, layer=i, act="relu2", out_dtype=BF16, tm_cap=MLP_UP_TM)
        h = _matmul_residual_ksplit(up, w_down, h, layer=i)
        if i == depth - 1:
            return _ple(h, ln_ple_w[i], w_ple_gate, p, w_ple_proj, ln_final_w, layer=i,
                        final_norm=True).reshape(b, t, d)
        h, hn = _ple(h, ln_ple_w[i], w_ple_gate, p, w_ple_proj, ln_mix_w[i + 1], layer=i, final_norm=False)
        mix_norm_w = None
```

```python
import functools

import jax
import jax.numpy as jnp
from jax import lax
from jax.experimental import pallas as pl
from jax.experimental.pallas import tpu as pltpu

F32 = jnp.float32
BF16 = jnp.bfloat16

V7X_LANES = 128
V7X_SUBLANES = 8
V7X_VMEM_LIMIT_BYTES = 56 * 1024 * 1024

NORM_EPS = 1e-6
GN_EPS = 1e-5
LRU_C = 8.0
DN_HEADS = 8
LRU_BLOCKS = 8
LRU_BLOCKS_PER_STEP = 4
SWA_HEADS = 8
RET_HEADS = 4
RET_DV = 256
SWA_DILATIONS = (1, 4, 16)
SWA_SPAN = 128
SWA_BLOCK = 128
SWA_TILE = 2048
DN_CHUNK = 128
DN_HEADS_PER_STEP = 4
RET_CHUNK = 256
RET_TILE = 1024
MASK_VALUE = -1e30
MLP_UP_TN = 2048
MLP_UP_TM = 1024


def _cparams(*sem):
    return pltpu.CompilerParams(dimension_semantics=sem, vmem_limit_bytes=V7X_VMEM_LIMIT_BYTES)


def _dot(a, b):
    return jnp.dot(a, b, preferred_element_type=F32)


def _dot_nt(a, b):
    return lax.dot_general(a, b, (((1,), (1,)), ((), ())), preferred_element_type=F32)


def _dot_tn(a, b):
    return lax.dot_general(a, b, (((0,), (0,)), ((), ())), preferred_element_type=F32)


def _split_bf16(x):
    hi = x.astype(BF16)
    lo = (x - hi.astype(F32)).astype(BF16)
    return hi, lo


def _dot_f32(a, b):
    ah, al = a if isinstance(a, tuple) else _split_bf16(a)
    bh, bl = b if isinstance(b, tuple) else _split_bf16(b)
    n = bh.shape[1]
    wide = _dot(ah, jnp.concatenate([bh, bl], axis=1))
    return wide[:, :n] + (wide[:, n:] + _dot(al, bh))


def _rms(x, w):
    return x * lax.rsqrt(jnp.mean(x * x, axis=-1, keepdims=True) + NORM_EPS) * w


def _silu(x):
    return x * jax.nn.sigmoid(x)


def _tile(n, cap, quantum):
    if n <= cap:
        return n
    best = None
    for c in range(quantum, cap + 1, quantum):
        if n % c == 0:
            best = c
    assert best is not None, (n, cap, quantum)
    return best


CAST_BLOCK_BYTES = 8 * 1024 * 1024


def _cast_kernel(x_ref, o_ref):
    o_ref[...] = x_ref[...].astype(o_ref.dtype)


def _to_bf16(w, tn=None):
    l, r, c = w.shape
    tn = c if tn is None else tn
    tr = _tile(r, max(16, CAST_BLOCK_BYTES // (4 * tn) // 16 * 16), 16)
    out = pl.pallas_call(
        _cast_kernel,
        grid=(l, c // tn, r // tr),
        in_specs=[pl.BlockSpec((None, tr, tn), lambda i, j, k: (i, k, j))],
        out_specs=pl.BlockSpec((None, None, tr, tn), lambda i, j, k: (i, j, k, 0)),
        out_shape=jax.ShapeDtypeStruct((l, c // tn, r, tn), BF16),
        compiler_params=_cparams("parallel", "parallel", "parallel"),
        name="to_bf16",
    )(w)
    return out if tn != c else out.reshape(l, r, c)


def _w_spec(w, layer, kk, tn, row_block=0):
    if w.ndim == 2:
        return pl.BlockSpec((kk, tn), lambda i, j: (row_block, j))
    if w.ndim == 3:
        return pl.BlockSpec((None, kk, tn), lambda i, j: (layer, row_block, j))
    assert w.shape[3] == tn, (w.shape, tn)
    return pl.BlockSpec((None, None, kk, tn), lambda i, j: (layer, j, row_block, 0))


def _w_cols(w):
    return w.shape[-1] if w.ndim < 4 else w.shape[1] * w.shape[3]


def _norm_mm_kernel(x_ref, g_ref, w_ref, o_ref, hn_ref, *, act):
    @pl.when(pl.program_id(1) == 0)
    def _():
        hn_ref[...] = _rms(x_ref[...], g_ref[...]).astype(BF16)

    _mm_act_kernel(hn_ref, w_ref, o_ref, act=act)


def _mm_act_kernel(x_ref, w_ref, o_ref, *, act):
    a = _dot(x_ref[...], w_ref[...])
    if act == "relu2":
        a = jnp.square(jnp.maximum(a, 0.0))
    o_ref[...] = a.astype(o_ref.dtype)


def _norm_matmul(x, g, w, *, act, out_dtype, layer=0, tm_cap=1024, tn_cap=1536):
    m, k = x.shape
    n = _w_cols(w)
    tm = _tile(m, tm_cap, 256)
    tn = w.shape[3] if w.ndim == 4 else _tile(n, tn_cap, 256)
    x_spec = pl.BlockSpec((tm, k), lambda i, j: (i, 0))
    common = dict(
        grid=(m // tm, n // tn),
        out_specs=pl.BlockSpec((tm, tn), lambda i, j: (i, j)),
        out_shape=jax.ShapeDtypeStruct((m, n), out_dtype),
    )
    if g is None:
        assert x.dtype == BF16
        return pl.pallas_call(
            functools.partial(_mm_act_kernel, act=act),
            in_specs=[x_spec, _w_spec(w, layer, k, tn)],
            compiler_params=_cparams("parallel", "parallel"),
            name="matmul_act", **common,
        )(x, w)
    return pl.pallas_call(
        functools.partial(_norm_mm_kernel, act=act),
        in_specs=[x_spec, pl.BlockSpec((1, k), lambda i, j: (0, 0)), _w_spec(w, layer, k, tn)],
        scratch_shapes=[pltpu.VMEM((tm, k), BF16)],
        compiler_params=_cparams("parallel", "arbitrary"),
        name="norm_matmul", **common,
    )(x, g.reshape(1, k), w)


def _mm_res_kernel(*refs, n_x, with_norm):
    xs, ws = refs[:n_x], refs[n_x:2 * n_x]
    res_ref = refs[2 * n_x]
    outs = refs[2 * n_x + 1:]
    acc = res_ref[...]
    for x_ref, w_ref in zip(xs, ws):
        acc = acc + _dot(x_ref[...], w_ref[...])
    if with_norm:
        g_ref, o_ref, hn_ref = outs
        hn_ref[...] = _rms(acc, g_ref[...]).astype(BF16)
    else:
        o_ref, = outs
    o_ref[...] = acc


def _matmul_residual(xs, w, res, *, layer=0, next_norm_w=None, tm_cap=512, tn_cap=512):
    m, n = res.shape
    assert n == _w_cols(w)
    tm = _tile(m, tm_cap, 256)
    tn = w.shape[3] if w.ndim == 4 else _tile(n, tn_cap, 256)
    with_norm = next_norm_w is not None
    assert not with_norm or tn == n
    in_specs, row = [], 0
    for x in xs:
        in_specs.append(pl.BlockSpec((tm, x.shape[1]), lambda i, j: (i, 0)))
    w_specs = []
    for x in xs:
        kk = x.shape[1]
        assert row % kk == 0
        w_specs.append(_w_spec(w, layer, kk, tn, row_block=row // kk))
        row += kk
    assert row == w.shape[-2]
    tile = pl.BlockSpec((tm, tn), lambda i, j: (i, j))
    in_specs = in_specs + w_specs + [tile]
    args = [*xs, *([w] * len(xs)), res]
    out_specs, out_shape = tile, jax.ShapeDtypeStruct((m, n), F32)
    if with_norm:
        in_specs.append(pl.BlockSpec((1, n), lambda i, j: (0, 0)))
        args.append(next_norm_w.reshape(1, n))
        out_specs, out_shape = [tile, tile], [out_shape, jax.ShapeDtypeStruct((m, n), BF16)]
    return pl.pallas_call(
        functools.partial(_mm_res_kernel, n_x=len(xs), with_norm=with_norm),
        grid=(m // tm, n // tn),
        in_specs=in_specs,
        out_specs=out_specs,
        out_shape=out_shape,
        compiler_params=_cparams("parallel", "parallel"),
        name="matmul_residual",
    )(*args)


def _mm_res_ksplit_kernel(x_ref, w_ref, res_ref, o_ref):
    @pl.when(pl.program_id(1) == 0)
    def _():
        o_ref[...] = res_ref[...]

    o_ref[...] += _dot(x_ref[...], w_ref[...])


def _matmul_residual_ksplit(x, w, res, *, layer, tm_cap=1024, tk_cap=1024):
    m, n = res.shape
    k = x.shape[1]
    tm = _tile(m, tm_cap, 256)
    tk = _tile(k, tk_cap, 256)
    return pl.pallas_call(
        _mm_res_ksplit_kernel,
        grid=(m // tm, k // tk),
        in_specs=[
            pl.BlockSpec((tm, tk), lambda i, kk: (i, kk)),
            pl.BlockSpec((None, tk, n), lambda i, kk: (layer, kk, 0)),
            pl.BlockSpec((tm, n), lambda i, kk: (i, 0)),
        ],
        out_specs=pl.BlockSpec((tm, n), lambda i, kk: (i, 0)),
        out_shape=jax.ShapeDtypeStruct((m, n), F32),
        compiler_params=_cparams("parallel", "arbitrary"),
        name="matmul_residual_ksplit",
    )(x, w, res)


def _ple_kernel(x_ref, g_ref, wg_ref, p_ref, wp_ref, gt_ref, o_ref, *hn_refs, final_norm):
    d = x_ref.shape[1]
    hn = _rms(x_ref[...], g_ref[...]).astype(BF16)
    pb = p_ref[...].astype(BF16)
    nh = d // 2 if d % (2 * V7X_LANES) == 0 else d
    for c0 in range(0, d, nh):
        gate = jax.nn.sigmoid(_dot(hn, wg_ref[:, c0:c0 + nh]))
        pp = _dot(pb, wp_ref[:, c0:c0 + nh])
        o_ref[:, c0:c0 + nh] = x_ref[:, c0:c0 + nh] + gate * pp
    tail = _rms(o_ref[...], gt_ref[...])
    if final_norm:
        o_ref[...] = tail
    else:
        hn_refs[0][...] = tail.astype(BF16)


def _ple(x, g, wg, p, wp, g_tail, *, layer, final_norm, tm_cap=512):
    m, d = x.shape
    t, pd = p.shape[2], p.shape[3]
    tm = _tile(t, tm_cap, 256)
    tiles_per_seq = t // tm
    row_tile = pl.BlockSpec((tm, d), lambda i: (i, 0))
    out_specs, out_shape = row_tile, jax.ShapeDtypeStruct((m, d), F32)
    if not final_norm:
        out_specs, out_shape = [row_tile, row_tile], [out_shape, jax.ShapeDtypeStruct((m, d), BF16)]
    return pl.pallas_call(
        functools.partial(_ple_kernel, final_norm=final_norm),
        grid=(m // tm,),
        in_specs=[
            row_tile,
            pl.BlockSpec((1, d), lambda i: (0, 0)),
            pl.BlockSpec((None, d, d), lambda i: (layer, 0, 0)),
            pl.BlockSpec((None, None, tm, pd), lambda i: (layer, i // tiles_per_seq, i % tiles_per_seq, 0)),
            pl.BlockSpec((None, pd, d), lambda i: (layer, 0, 0)),
            pl.BlockSpec((1, d), lambda i: (0, 0)),
        ],
        out_specs=out_specs,
        out_shape=out_shape,
        compiler_params=_cparams("parallel"),
        name="ple",
    )(x, g.reshape(1, d), wg, p, wp, g_tail.reshape(1, d))


def _causal_conv(x_ref, halo_ref, w_ref, first):
    x = x_ref[0]
    halo = jnp.where(first, 0.0, halo_ref[0])
    xp = jnp.concatenate([halo, x], axis=0)
    w = w_ref[...]
    y = pltpu.roll(xp, 3, axis=0)[V7X_SUBLANES:] * w[0:1]
    y = y + pltpu.roll(xp, 2, axis=0)[V7X_SUBLANES:] * w[1:2]
    y = y + pltpu.roll(xp, 1, axis=0)[V7X_SUBLANES:] * w[2:3]
    return y + x * w[3:4]


def _halo_map(col, rows_per_tile):
    nb = rows_per_tile // V7X_SUBLANES
    return lambda b, h, t: (b, jnp.maximum(t * nb - 1, 0), col(h))


def _unit_lower_inverses(ms, row, col):
    c = ms[0].shape[0]
    eye = (row == col).astype(F32)
    diag16 = (row >> 4) == (col >> 4)
    b16 = lambda xs: [x.astype(BF16) for x in xs]
    ns = [jnp.where(diag16, m, 0.0) for m in ms]
    invs = [eye - n for n in ns]
    qs = b16(ns)
    for step in range(3):
        qs = b16([_dot(q, q) for q in qs])
        invs = [inv + _dot(inv.astype(BF16), q) for inv, q in zip(invs, qs)]
    shift = 4
    while (1 << shift) < c:
        off = ((row >> (shift + 1)) == (col >> (shift + 1))) & ((row >> shift) != (col >> shift))
        inv16 = b16(invs)
        tmps = [_dot(inv, jnp.where(off, m, 0.0).astype(BF16)) for inv, m in zip(inv16, ms)]
        invs = [inv - _dot(tmp.astype(BF16), i16) for inv, tmp, i16 in zip(invs, tmps, inv16)]
        shift += 1
    xs = [_split_bf16(inv) for inv in invs]
    res = [eye - inv - _dot_f32(m, x) for inv, m, x in zip(invs, ms, xs)]
    return [inv + _dot_f32(x, r) for inv, x, r in zip(invs, xs, res)]


def _delta_kernel(q_ref, qh_ref, k_ref, kh_ref, v_ref, vh_ref, z_ref, ba_ref,
                  wq_ref, wk_ref, wv_ref, alog_ref, dtb_ref, nw_ref, o_ref, s_ref):
    hp = pl.program_id(1)
    first = pl.program_id(2) == 0
    tt = q_ref.shape[1]
    c = DN_CHUNK
    dh = V7X_LANES
    nheads = q_ref.shape[2] // dh
    nch = tt // c

    @pl.when(first)
    def _():
        s_ref[...] = jnp.zeros_like(s_ref)

    def l2n(x):
        return x * lax.rsqrt(jnp.sum(x * x, axis=-1, keepdims=True) + 1e-6)

    ba = ba_ref[0]
    lane = lax.broadcasted_iota(jnp.int32, ba.shape, 1)
    sig_ba = jax.nn.sigmoid(ba)
    g_all = -jnp.exp(alog_ref[...]) * jax.nn.softplus(ba + dtb_ref[...])

    row = lax.broadcasted_iota(jnp.int32, (c, c), 0)
    col = lax.broadcasted_iota(jnp.int32, (c, c), 1)
    causal = row >= col
    strict = row > col
    ltri = causal.astype(BF16)

    def cumsum_block(gblk):
        g_hi, g_lo = _split_bf16(gblk)
        g_lo2 = (gblk - g_hi.astype(F32) - g_lo.astype(F32)).astype(BF16)
        return _dot(ltri, g_hi) + (_dot(ltri, g_lo) + _dot(ltri, g_lo2))

    gc_blocks = [cumsum_block(g_all[ci * c:(ci + 1) * c]) for ci in range(nch)]
    gc_blocks_t = [gcb.T for gcb in gc_blocks]

    q_all = _silu(_causal_conv(q_ref, qh_ref, wq_ref, first))
    k_all = _silu(_causal_conv(k_ref, kh_ref, wk_ref, first))
    v_all = _silu(_causal_conv(v_ref, vh_ref, wv_ref, first))
    qs, ks, vs, bs, gcs, gc_rows = [], [], [], [], [], []
    for hi in range(nheads):
        hs = slice(hi * dh, (hi + 1) * dh)
        head = nheads * hp + hi
        qh = l2n(q_all[:, hs]) * (dh ** -0.5)
        kh = l2n(k_all[:, hs])
        vh = v_all[:, hs]
        beta = jnp.sum(jnp.where(lane == head, sig_ba, 0.0), axis=-1, keepdims=True)
        for ci in range(nch):
            sl = slice(ci * c, (ci + 1) * c)
            qs.append(qh[sl]); ks.append(kh[sl]); vs.append(vh[sl]); bs.append(beta[sl])
            gcol = jnp.sum(jnp.where(col == head + DN_HEADS, gc_blocks[ci], 0.0), axis=-1, keepdims=True)
            grow = jnp.sum(jnp.where(row == head + DN_HEADS, gc_blocks_t[ci], 0.0), axis=0, keepdims=True)
            gcs.append(jnp.broadcast_to(gcol, (c, c)))
            gc_rows.append(jnp.broadcast_to(grow, (c, c)))

    decays = [jnp.where(causal, jnp.exp(jnp.where(causal, gc - gr, 0.0)), 0.0) for gc, gr in zip(gcs, gc_rows)]
    kbs = [kc * bc for kc, bc in zip(ks, bs)]
    k16 = [kc.astype(BF16) for kc in ks]
    ms = [jnp.where(strict, _dot_nt(kb.astype(BF16), kc) * dec, 0.0) for kb, kc, dec in zip(kbs, k16, decays)]
    invs = _unit_lower_inverses(ms, row, col)
    egcs = [jnp.exp(gc) for gc in gcs]
    sols = [_dot_f32(inv, jnp.concatenate([vc * bc, kb * egc], axis=1))
            for inv, vc, bc, kb, egc in zip(invs, vs, bs, kbs, egcs)]
    us = [sol[:, :dh] for sol in sols]
    ws = [sol[:, dh:].astype(BF16) for sol in sols]
    qks = [(_dot_nt(qc.astype(BF16), kc) * dec).astype(BF16) for qc, kc, dec in zip(qs, k16, decays)]
    q_decs = [(qc * egc).astype(BF16) for qc, egc in zip(qs, egcs)]
    lasts = [gc[c - 1:c, :] for gc in gcs]
    k_decs = [(kc * jnp.exp(last - gc)).astype(BF16) for kc, last, gc in zip(ks, lasts, gcs)]
    g_tots = [jnp.exp(last) for last in lasts]

    states = [s_ref[hi] for hi in range(nheads)]
    for ci in range(nch):
        sl = slice(ci * c, (ci + 1) * c)
        for hi in range(nheads):
            i = hi * nch + ci
            hs = slice(hi * dh, (hi + 1) * dh)
            sb = states[hi].astype(BF16)
            v_new = (us[i] - _dot(ws[i], sb)).astype(BF16)
            o = _dot(q_decs[i], sb) + _dot(qks[i], v_new)
            states[hi] = states[hi] * g_tots[i] + _dot_tn(k_decs[i], v_new)
            o_ref[0, sl, hs] = (_rms(o, nw_ref[...]) * _silu(z_ref[0, sl, hs])).astype(o_ref.dtype)
    for hi in range(nheads):
        s_ref[hi] = states[hi]


def _delta_mixer(proj, conv_w, a_log, dt_bias, norm_w, *, tt_cap=512):
    b, t, _ = proj.shape
    tt = _tile(t, tt_cap, DN_CHUNK)
    hh = DN_HEADS
    hps = DN_HEADS_PER_STEP
    wd = hps * V7X_LANES
    npair = hh // hps
    blk = lambda off: pl.BlockSpec((1, tt, wd), lambda bi, h, ti, off=off: (bi, ti, off + h))
    halo = lambda off: pl.BlockSpec((1, V7X_SUBLANES, wd), _halo_map(lambda h, off=off: off + h, tt))
    cw = lambda off: pl.BlockSpec((4, wd), lambda bi, h, ti, off=off: (0, off + h))
    row_spec = pl.BlockSpec((1, V7X_LANES), lambda bi, h, ti: (0, 0))
    pad = jnp.zeros((V7X_LANES - 2 * hh,), F32)
    alog_row = jnp.concatenate([jnp.zeros((hh,), F32), a_log, pad]).reshape(1, V7X_LANES)
    dtb_row = jnp.concatenate([jnp.zeros((hh,), F32), dt_bias, pad]).reshape(1, V7X_LANES)
    ba_col = 6 * hh
    return pl.pallas_call(
        _delta_kernel,
        grid=(b, npair, t // tt),
        in_specs=[
            blk(0), halo(0), blk(npair), halo(npair), blk(2 * npair), halo(2 * npair), blk(3 * npair),
            pl.BlockSpec((1, tt, V7X_LANES), lambda bi, h, ti: (bi, ti, ba_col)),
            cw(0), cw(npair), cw(2 * npair), row_spec, row_spec, row_spec,
        ],
        out_specs=pl.BlockSpec((1, tt, wd), lambda bi, h, ti: (bi, ti, h)),
        out_shape=jax.ShapeDtypeStruct((b, t, hh * V7X_LANES), BF16),
        scratch_shapes=[pltpu.VMEM((hps, V7X_LANES, V7X_LANES), F32)],
        compiler_params=_cparams("parallel", "parallel", "arbitrary"),
        name="delta_mixer",
    )(proj, proj, proj, proj, proj, proj, proj, proj, conv_w, conv_w, conv_w,
      alog_row, dtb_row, norm_w.reshape(1, V7X_LANES))


def _lru_kernel(x_ref, xh_ref, y_ref, cw_ref, cb_ref, wa_ref, ba_ref, wx_ref, bx_ref, lam_ref,
                o_ref, h_ref):
    first = pl.program_id(2) == 0
    tt = x_ref.shape[1]

    @pl.when(first)
    def _():
        h_ref[...] = jnp.zeros_like(h_ref)

    wc = x_ref.shape[2]
    xc = _causal_conv(x_ref, xh_ref, cw_ref, first) + cb_ref[...]
    xb = xc.astype(BF16)

    def block_diag(w_ref):
        return jnp.concatenate([_dot(xb[:, g * V7X_LANES:(g + 1) * V7X_LANES], w_ref[g])
                                for g in range(wc // V7X_LANES)], axis=1)

    r = jax.nn.sigmoid(block_diag(wa_ref) + ba_ref[...])
    i = jax.nn.sigmoid(block_diag(wx_ref) + bx_ref[...])
    log_a = -LRU_C * r * jax.nn.softplus(-lam_ref[...])
    a = jnp.exp(log_a)
    u = jnp.sqrt(1.0 - a * a) * (i * xc)

    hr = V7X_SUBLANES
    rowg = lax.broadcasted_iota(jnp.int32, (tt, wc), 0) & (hr - 1)
    s = 1
    while s < hr:
        a_sh = jnp.where(rowg >= s, pltpu.roll(a, s, axis=0), 1.0)
        u_sh = jnp.where(rowg >= s, pltpu.roll(u, s, axis=0), 0.0)
        u = a * u_sh + u
        a = a * a_sh
        s *= 2
    carry = h_ref[...]
    gelu_y = jax.nn.gelu(y_ref[0])
    out_rows = 2 * hr
    for g0 in range(0, tt, out_rows):
        parts = []
        for g in range(g0, g0 + out_rows, hr):
            hs = u[g:g + hr] + a[g:g + hr] * carry
            carry = hs[hr - 1:hr]
            parts.append(hs)
        rows = slice(g0, g0 + out_rows)
        o_ref[0, rows, :] = (jnp.concatenate(parts, axis=0) * gelu_y[rows]).astype(o_ref.dtype)
    h_ref[...] = carry


def _lru_mixer(proj, conv_w, conv_b, wa, ba, wx, bx, lam, *, x_col, y_col, tt_cap=512):
    b, t, _ = proj.shape
    tt = _tile(t, tt_cap, 16)
    gg = LRU_BLOCKS
    gs = LRU_BLOCKS_PER_STEP
    wc = gs * V7X_LANES
    assert gg % gs == 0 and x_col % gs == 0 and y_col % gs == 0
    vec = lambda a: a.reshape(1, gg * V7X_LANES)
    vspec = pl.BlockSpec((1, wc), lambda bi, g, ti: (0, g))
    wspec = pl.BlockSpec((gs, V7X_LANES, V7X_LANES), lambda bi, g, ti: (g, 0, 0))
    return pl.pallas_call(
        _lru_kernel,
        grid=(b, gg // gs, t // tt),
        in_specs=[
            pl.BlockSpec((1, tt, wc), lambda bi, g, ti: (bi, ti, x_col // gs + g)),
            pl.BlockSpec((1, V7X_SUBLANES, wc), _halo_map(lambda g: x_col // gs + g, tt)),
            pl.BlockSpec((1, tt, wc), lambda bi, g, ti: (bi, ti, y_col // gs + g)),
            pl.BlockSpec((4, wc), lambda bi, g, ti: (0, g)),
            vspec, wspec, vspec, wspec, vspec, vspec,
        ],
        out_specs=pl.BlockSpec((1, tt, wc), lambda bi, g, ti: (bi, ti, g)),
        out_shape=jax.ShapeDtypeStruct((b, t, gg * V7X_LANES), BF16),
        scratch_shapes=[pltpu.VMEM((1, wc), F32)],
        compiler_params=_cparams("parallel", "parallel", "arbitrary"),
        name="lru_mixer",
    )(proj, proj, proj, conv_w, vec(conv_b), wa.astype(BF16), vec(ba), wx.astype(BF16), vec(bx), vec(lam))


def _dilated_kernel(q_ref, kc_ref, kp_ref, vc_ref, vp_ref, o_ref, qa_ref, ka_ref, va_ref, sa_ref, sn_ref):
    h = pl.program_id(1)
    has_prev = pl.program_id(2) > 0
    tq = q_ref.shape[1]
    bq = SWA_BLOCK
    d1, d2, d3 = SWA_DILATIONS
    assert d1 == 1 and d3 % d2 == 0
    pw = kp_ref.shape[1]
    qq = tq // d2
    pq = pw // d2
    cs = pq + qq
    st = d3 // d2
    assert pw == bq * d3 and pq >= bq * st
    for r in range(d2):
        qa_ref[r * qq:(r + 1) * qq, :] = q_ref.at[0][pl.ds(r, qq, stride=d2), :]
        for src_p, src_c, dst in ((kp_ref, kc_ref, ka_ref), (vp_ref, vc_ref, va_ref)):
            dst[r * cs:r * cs + pq, :] = src_p.at[0][pl.ds(r, pq, stride=d2), :]
            dst[r * cs + pq:(r + 1) * cs, :] = src_c.at[0][pl.ds(r, qq, stride=d2), :]

    slope = jnp.exp2(-(jnp.full((1, 1), h, jnp.int32).astype(F32) + 1.0) * (8.0 / SWA_HEADS))
    iq = lax.broadcasted_iota(jnp.int32, (bq, 2 * bq), 0)
    ik = lax.broadcasted_iota(jnp.int32, (bq, 2 * bq), 1)
    rel = bq + iq - ik
    in_window = (rel >= 0) & (rel <= SWA_SPAN)
    relf = rel.astype(F32)
    scale = V7X_LANES ** -0.5
    ones_blk = jnp.ones((2 * bq, V7X_LANES), BF16)

    def attend(q_rows, kcat, vcat, d, prev_ok):
        qb = (q_rows * scale).astype(BF16)
        s = _dot_nt(qb, kcat.astype(BF16)) - (slope * float(d)) * relf
        valid = in_window if prev_ok is True else in_window & ((ik >= bq) | prev_ok)
        s = jnp.where(valid, s, MASK_VALUE)
        m_b = jnp.max(s, axis=-1, keepdims=True)
        p = jnp.exp(s - m_b).astype(BF16)
        pv = _dot(p, jnp.concatenate([vcat.astype(BF16), ones_blk], axis=1))
        return jnp.broadcast_to(m_b, (bq, V7X_LANES)), pv[:, V7X_LANES:], pv[:, :V7X_LANES]

    def merge(old, new):
        (m_o, l_o, acc_o), (m_b, l_b, acc_b) = old, new
        m_n = jnp.maximum(m_o, m_b)
        alpha = jnp.exp(m_o - m_n)
        beta = jnp.exp(m_b - m_n)
        return m_n, alpha * l_o + beta * l_b, alpha * acc_o + beta * acc_b

    for r in range(d2):
        for n in range(qq // bq):
            q0 = r * qq + n * bq
            k0 = r * cs + pq + (n - 1) * bq
            new = attend(qa_ref[q0:q0 + bq, :], ka_ref[k0:k0 + 2 * bq, :], va_ref[k0:k0 + 2 * bq, :],
                         d2, has_prev if n == 0 else True)
            for kk in range(3):
                sa_ref[kk, q0:q0 + bq, :] = new[kk]

    take = lambda ref, start: ref[pl.ds(start, bq, stride=st), :]
    for r3 in range(d3):
        for n in range(tq // (bq * d3)):
            r = r3 % d2
            p0 = r3 // d2 + n * (bq * st)
            q0 = r * qq + p0
            k0 = r * cs + pq + p0
            kcat = jnp.concatenate([take(ka_ref, k0 - bq * st), take(ka_ref, k0)], axis=0)
            vcat = jnp.concatenate([take(va_ref, k0 - bq * st), take(va_ref, k0)], axis=0)
            new = attend(take(qa_ref, q0), kcat, vcat, d3, has_prev if n == 0 else True)
            out = merge(tuple(take(sa_ref.at[kk], q0) for kk in range(3)), new)
            for kk in range(3):
                sa_ref.at[kk][pl.ds(q0, bq, stride=st), :] = out[kk]

    for r in range(d2):
        for kk in range(3):
            sn_ref.at[kk][pl.ds(r, qq, stride=d2), :] = sa_ref[kk, r * qq:(r + 1) * qq, :]

    for n in range(tq // bq):
        base = n * bq
        if n == 0:
            kcat = jnp.concatenate([kp_ref[0, pw - bq:pw, :], kc_ref[0, 0:bq, :]], axis=0)
            vcat = jnp.concatenate([vp_ref[0, pw - bq:pw, :], vc_ref[0, 0:bq, :]], axis=0)
        else:
            kcat, vcat = kc_ref[0, base - bq:base + bq, :], vc_ref[0, base - bq:base + bq, :]
        new = attend(q_ref[0, base:base + bq, :], kcat, vcat, d1, has_prev if n == 0 else True)
        _, l_f, acc_f = merge(tuple(sn_ref[kk, base:base + bq, :] for kk in range(3)), new)
        o_ref[0, base:base + bq, :] = (acc_f / l_f).astype(o_ref.dtype)


def _dilated_mixer(proj, *, q_col, k_col, v_col, tq=SWA_TILE):
    b, t, _ = proj.shape
    pw = SWA_BLOCK * max(SWA_DILATIONS)
    tq = min(tq, t)
    assert t % tq == 0 and tq % pw == 0
    hh = SWA_HEADS
    cur = lambda off: pl.BlockSpec((1, tq, V7X_LANES), lambda bi, h, ti, off=off: (bi, ti, off + h))
    prev = lambda off: pl.BlockSpec(
        (1, pw, V7X_LANES), lambda bi, h, ti, off=off: (bi, jnp.maximum(ti * (tq // pw) - 1, 0), off + h))
    return pl.pallas_call(
        _dilated_kernel,
        grid=(b, hh, t // tq),
        in_specs=[cur(q_col), cur(k_col), prev(k_col), cur(v_col), prev(v_col)],
        out_specs=pl.BlockSpec((1, tq, V7X_LANES), lambda bi, h, ti: (bi, ti, h)),
        out_shape=jax.ShapeDtypeStruct((b, t, hh * V7X_LANES), BF16),
        scratch_shapes=[pltpu.VMEM((tq, V7X_LANES), F32), pltpu.VMEM((pw + tq, V7X_LANES), F32),
                        pltpu.VMEM((pw + tq, V7X_LANES), F32), pltpu.VMEM((3, tq, V7X_LANES), F32),
                        pltpu.VMEM((3, tq, V7X_LANES), F32)],
        compiler_params=_cparams("parallel", "parallel", "parallel"),
        name="dilated_mixer",
    )(proj, proj, proj, proj, proj)


def _retention_kernel(q_ref, k_ref, v_ref, g_ref, o_ref, s_ref):
    h = pl.program_id(1)
    tt = q_ref.shape[1]
    c = min(RET_CHUNK, tt)
    nch = tt // c

    @pl.when(pl.program_id(2) == 0)
    def _():
        s_ref[...] = jnp.zeros_like(s_ref)

    hf = jnp.full((1, 1), h, jnp.int32).astype(F32)
    log_gamma = jnp.log1p(-jnp.exp2(-5.0 - hf))
    row = lax.broadcasted_iota(jnp.int32, (c, c), 0)
    col = lax.broadcasted_iota(jnp.int32, (c, c), 1)
    rel = (row - col).astype(F32)
    dmask = jnp.where(rel >= 0, jnp.exp(jnp.maximum(rel, 0.0) * log_gamma), 0.0)
    idx = lax.broadcasted_iota(jnp.int32, (c, 1), 0).astype(F32)
    q_scale = jnp.exp((idx + 1.0) * log_gamma)
    k_scale = jnp.exp((c - 1.0 - idx) * log_gamma)
    chunk_decay = jnp.exp(float(c) * log_gamma)

    sls = [slice(ci * c, (ci + 1) * c) for ci in range(nch)]
    qs = [q_ref[0, sl, :] for sl in sls]
    ks = [k_ref[0, sl, :] * (V7X_LANES ** -0.5) for sl in sls]
    vs = [v_ref[0, sl, :].astype(BF16) for sl in sls]
    intras = [_dot((_dot_nt(q.astype(BF16), k.astype(BF16)) * dmask).astype(BF16), v)
              for q, k, v in zip(qs, ks, vs)]
    kvs = [_dot_tn((k * k_scale).astype(BF16), v) for k, v in zip(ks, vs)]
    states = [s_ref[...]]
    for kv in kvs:
        states.append(states[-1] * chunk_decay + kv)
    s_ref[...] = states[-1]
    for sl, q, intra, state in zip(sls, qs, intras, states):
        o = intra + _dot((q * q_scale).astype(BF16), state.astype(BF16))
        mu = jnp.mean(o, axis=-1, keepdims=True)
        oc = o - mu
        o = oc * lax.rsqrt(jnp.mean(oc * oc, axis=-1, keepdims=True) + GN_EPS)
        o_ref[0, sl, :] = (o * _silu(g_ref[0, sl, :])).astype(o_ref.dtype)


def _retention_mixer(proj, *, q_col, k_col, v_col, g_col):
    b, t, _ = proj.shape
    c = _tile(t, RET_TILE, RET_CHUNK)
    hh = RET_HEADS
    wide = RET_DV // V7X_LANES
    assert v_col % wide == 0 and g_col % wide == 0
    nar = lambda off: pl.BlockSpec((1, c, V7X_LANES), lambda bi, h, ti, off=off: (bi, ti, off + h))
    wid = lambda off: pl.BlockSpec((1, c, RET_DV), lambda bi, h, ti, off=off: (bi, ti, off // wide + h))
    return pl.pallas_call(
        _retention_kernel,
        grid=(b, hh, t // c),
        in_specs=[nar(q_col), nar(k_col), wid(v_col), wid(g_col)],
        out_specs=pl.BlockSpec((1, c, RET_DV), lambda bi, h, ti: (bi, ti, h)),
        out_shape=jax.ShapeDtypeStruct((b, t, hh * RET_DV), BF16),
        scratch_shapes=[pltpu.VMEM((V7X_LANES, RET_DV), F32)],
        compiler_params=_cparams("parallel", "parallel", "arbitrary"),
        name="retention_mixer",
    )(proj, proj, proj, proj)


def _even_w_in_kernel(x_ref, o_ref):
    qkvz = 4 * DN_HEADS * V7X_LANES
    nba = 2 * DN_HEADS
    lru = 2 * LRU_BLOCKS * V7X_LANES
    rows = x_ref.shape[0]
    o_ref[:, :qkvz] = x_ref[:, :qkvz].astype(BF16)
    o_ref[:, qkvz:qkvz + lru] = x_ref[:, qkvz + nba:qkvz + nba + lru].astype(BF16)
    tail = jnp.concatenate([x_ref[:, qkvz:qkvz + nba], jnp.zeros((rows, 2 * V7X_LANES - nba), F32)], axis=1)
    o_ref[:, qkvz + lru:] = tail.astype(BF16)


def _even_w_in_bf16(w):
    l, d, c = w.shape
    n = 4 * DN_HEADS * V7X_LANES + 2 * LRU_BLOCKS * V7X_LANES + 2 * V7X_LANES
    tr = _tile(d, 256, 16)
    return pl.pallas_call(
        _even_w_in_kernel,
        grid=(l, d // tr),
        in_specs=[pl.BlockSpec((None, tr, c), lambda i, j: (i, j, 0))],
        out_specs=pl.BlockSpec((None, tr, n), lambda i, j: (i, j, 0)),
        out_shape=jax.ShapeDtypeStruct((l, d, n), BF16),
        compiler_params=_cparams("parallel", "parallel"),
        name="even_w_in_bf16",
    )(w)


def kernel(x, p, ln_mix_w, ln_mlp_w, ln_ple_w, w_up, w_down, w_ple_proj, w_ple_gate, ln_final_w,
           ev_w_in, ev_w_out, dn_conv_w, dn_a_log, dn_dt_bias, dn_norm_w,
           lru_conv_w, lru_conv_b, lru_wa, lru_ba, lru_wx, lru_bx, lru_lambda,
           od_w_in, od_w_out):
    b, t, d = x.shape
    depth = ln_mix_w.shape[0]
    m = b * t
    h = x.reshape(m, d)
    w_ple_proj, w_ple_gate, ev_w_out, od_w_in, od_w_out = (
        _to_bf16(w) for w in (w_ple_proj, w_ple_gate, ev_w_out, od_w_in, od_w_out))
    ev_w_in = _even_w_in_bf16(ev_w_in)
    w_up = _to_bf16(w_up, tn=MLP_UP_TN)
    w_down = _to_bf16(w_down)
    hn, mix_norm_w = h, ln_mix_w[0]
    for i in range(depth):
        j = i // 2
        if i % 2 == 0:
            proj = _norm_matmul(hn, mix_norm_w, ev_w_in, layer=j, act=None, out_dtype=F32)
            proj = proj.reshape(b, t, -1)
            y_a = _delta_mixer(proj, dn_conv_w[j], dn_a_log[j], dn_dt_bias[j], dn_norm_w[j])
            y_b = _lru_mixer(proj, lru_conv_w[j], lru_conv_b[j], lru_wa[j], lru_ba[j], lru_wx[j], lru_bx[j],
                             lru_lambda[j], x_col=4 * DN_HEADS, y_col=4 * DN_HEADS + LRU_BLOCKS)
            w_out = ev_w_out
        else:
            proj = _norm_matmul(hn, mix_norm_w, od_w_in, layer=j, act=None, out_dtype=F32)
            proj = proj.reshape(b, t, -1)
            y_a = _dilated_mixer(proj, q_col=0, k_col=SWA_HEADS, v_col=2 * SWA_HEADS)
            y_b = _retention_mixer(proj, q_col=3 * SWA_HEADS, k_col=3 * SWA_HEADS + RET_HEADS,
                                   v_col=3 * SWA_HEADS + 2 * RET_HEADS,
                                   g_col=3 * SWA_HEADS + 2 * RET_HEADS + RET_HEADS * RET_DV // V7X_LANES)
            w_out = od_w_out
        h, hn = _matmul_residual([y_a.reshape(m, -1), y_b.reshape(m, -1)], w_out, h, layer=j,
                                 next_norm_w=ln_mlp_w[i], tn_cap=d)
        up = _norm_matmul(hn, None, w_up, layer=i, act="relu2", out_dtype=BF16, tm_cap=MLP_UP_TM)
        h = _matmul_residual_ksplit(up, w_down, h, layer=i)
        if i == depth - 1:
            return _ple(h, ln_ple_w[i], w_ple_gate, p, w_ple_proj, ln_final_w, layer=i,
                        final_norm=True).reshape(b, t, d)
        h, hn = _ple(h, ln_ple_w[i], w_ple_gate, p, w_ple_proj, ln_mix_w[i + 1], layer=i, final_norm=False)
        mix_norm_w = None
```

```python
import functools

import jax
import jax.numpy as jnp
from jax import lax
from jax.experimental import pallas as pl
from jax.experimental.pallas import tpu as pltpu

F32 = jnp.float32
BF16 = jnp.bfloat16

V7X_LANES = 128
V7X_SUBLANES = 8
V7X_VMEM_LIMIT_BYTES = 56 * 1024 * 1024

NORM_EPS = 1e-6
GN_EPS = 1e-5
LRU_C = 8.0
DN_HEADS = 8
LRU_BLOCKS = 8
LRU_BLOCKS_PER_STEP = 4
SWA_HEADS = 8
RET_HEADS = 4
RET_DV = 256
SWA_DILATIONS = (1, 4, 16)
SWA_SPAN = 128
SWA_BLOCK = 128
SWA_TILE = 2048
DN_CHUNK = 128
DN_HEADS_PER_STEP = 8
RET_CHUNK = 256
RET_TILE = 1024
MASK_VALUE = -1e30
MLP_UP_TN = 2048
MLP_UP_TM = 1024


def _cparams(*sem):
    return pltpu.CompilerParams(dimension_semantics=sem, vmem_limit_bytes=V7X_VMEM_LIMIT_BYTES)


def _dot(a, b):
    return jnp.dot(a, b, preferred_element_type=F32)


def _dot_nt(a, b):
    return lax.dot_general(a, b, (((1,), (1,)), ((), ())), preferred_element_type=F32)


def _dot_tn(a, b):
    return lax.dot_general(a, b, (((0,), (0,)), ((), ())), preferred_element_type=F32)


def _split_bf16(x):
    hi = x.astype(BF16)
    lo = (x - hi.astype(F32)).astype(BF16)
    return hi, lo


def _dot_f32(a, b):
    ah, al = a if isinstance(a, tuple) else _split_bf16(a)
    bh, bl = b if isinstance(b, tuple) else _split_bf16(b)
    n = bh.shape[1]
    wide = _dot(ah, jnp.concatenate([bh, bl], axis=1))
    return wide[:, :n] + (wide[:, n:] + _dot(al, bh))


def _rms(x, w):
    return x * lax.rsqrt(jnp.mean(x * x, axis=-1, keepdims=True) + NORM_EPS) * w


def _silu(x):
    return x * jax.nn.sigmoid(x)


def _tile(n, cap, quantum):
    if n <= cap:
        return n
    best = None
    for c in range(quantum, cap + 1, quantum):
        if n % c == 0:
            best = c
    assert best is not None, (n, cap, quantum)
    return best


CAST_BLOCK_BYTES = 8 * 1024 * 1024


def _cast_kernel(x_ref, o_ref):
    o_ref[...] = x_ref[...].astype(o_ref.dtype)


def _to_bf16(w, tn=None):
    l, r, c = w.shape
    tn = c if tn is None else tn
    tr = _tile(r, max(16, CAST_BLOCK_BYTES // (4 * tn) // 16 * 16), 16)
    out = pl.pallas_call(
        _cast_kernel,
        grid=(l, c // tn, r // tr),
        in_specs=[pl.BlockSpec((None, tr, tn), lambda i, j, k: (i, k, j))],
        out_specs=pl.BlockSpec((None, None, tr, tn), lambda i, j, k: (i, j, k, 0)),
        out_shape=jax.ShapeDtypeStruct((l, c // tn, r, tn), BF16),
        compiler_params=_cparams("parallel", "parallel", "parallel"),
        name="to_bf16",
    )(w)
    return out if tn != c else out.reshape(l, r, c)


def _w_spec(w, layer, kk, tn, row_block=0):
    if w.ndim == 2:
        return pl.BlockSpec((kk, tn), lambda i, j: (row_block, j))
    if w.ndim == 3:
        return pl.BlockSpec((None, kk, tn), lambda i, j: (layer, row_block, j))
    assert w.shape[3] == tn, (w.shape, tn)
    return pl.BlockSpec((None, None, kk, tn), lambda i, j: (layer, j, row_block, 0))


def _w_cols(w):
    return w.shape[-1] if w.ndim < 4 else w.shape[1] * w.shape[3]


def _norm_mm_kernel(x_ref, g_ref, w_ref, o_ref, hn_ref, *, act):
    @pl.when(pl.program_id(1) == 0)
    def _():
        hn_ref[...] = _rms(x_ref[...], g_ref[...]).astype(BF16)

    _mm_act_kernel(hn_ref, w_ref, o_ref, act=act)


def _mm_act_kernel(x_ref, w_ref, o_ref, *, act):
    a = _dot(x_ref[...], w_ref[...])
    if act == "relu2":
        a = jnp.square(jnp.maximum(a, 0.0))
    o_ref[...] = a.astype(o_ref.dtype)


def _norm_matmul(x, g, w, *, act, out_dtype, layer=0, tm_cap=1024, tn_cap=1536):
    m, k = x.shape
    n = _w_cols(w)
    tm = _tile(m, tm_cap, 256)
    tn = w.shape[3] if w.ndim == 4 else _tile(n, tn_cap, 256)
    x_spec = pl.BlockSpec((tm, k), lambda i, j: (i, 0))
    common = dict(
        grid=(m // tm, n // tn),
        out_specs=pl.BlockSpec((tm, tn), lambda i, j: (i, j)),
        out_shape=jax.ShapeDtypeStruct((m, n), out_dtype),
    )
    if g is None:
        assert x.dtype == BF16
        return pl.pallas_call(
            functools.partial(_mm_act_kernel, act=act),
            in_specs=[x_spec, _w_spec(w, layer, k, tn)],
            compiler_params=_cparams("parallel", "parallel"),
            name="matmul_act", **common,
        )(x, w)
    return pl.pallas_call(
        functools.partial(_norm_mm_kernel, act=act),
        in_specs=[x_spec, pl.BlockSpec((1, k), lambda i, j: (0, 0)), _w_spec(w, layer, k, tn)],
        scratch_shapes=[pltpu.VMEM((tm, k), BF16)],
        compiler_params=_cparams("parallel", "arbitrary"),
        name="norm_matmul", **common,
    )(x, g.reshape(1, k), w)


def _mm_res_kernel(*refs, n_x, with_norm):
    xs, ws = refs[:n_x], refs[n_x:2 * n_x]
    res_ref = refs[2 * n_x]
    outs = refs[2 * n_x + 1:]
    acc = res_ref[...]
    for x_ref, w_ref in zip(xs, ws):
        acc = acc + _dot(x_ref[...], w_ref[...])
    if with_norm:
        g_ref, o_ref, hn_ref = outs
        hn_ref[...] = _rms(acc, g_ref[...]).astype(BF16)
    else:
        o_ref, = outs
    o_ref[...] = acc


def _matmul_residual(xs, w, res, *, layer=0, next_norm_w=None, tm_cap=512, tn_cap=512):
    m, n = res.shape
    assert n == _w_cols(w)
    tm = _tile(m, tm_cap, 256)
    tn = w.shape[3] if w.ndim == 4 else _tile(n, tn_cap, 256)
    with_norm = next_norm_w is not None
    assert not with_norm or tn == n
    in_specs, row = [], 0
    for x in xs:
        in_specs.append(pl.BlockSpec((tm, x.shape[1]), lambda i, j: (i, 0)))
    w_specs = []
    for x in xs:
        kk = x.shape[1]
        assert row % kk == 0
        w_specs.append(_w_spec(w, layer, kk, tn, row_block=row // kk))
        row += kk
    assert row == w.shape[-2]
    tile = pl.BlockSpec((tm, tn), lambda i, j: (i, j))
    in_specs = in_specs + w_specs + [tile]
    args = [*xs, *([w] * len(xs)), res]
    out_specs, out_shape = tile, jax.ShapeDtypeStruct((m, n), F32)
    if with_norm:
        in_specs.append(pl.BlockSpec((1, n), lambda i, j: (0, 0)))
        args.append(next_norm_w.reshape(1, n))
        out_specs, out_shape = [tile, tile], [out_shape, jax.ShapeDtypeStruct((m, n), BF16)]
    return pl.pallas_call(
        functools.partial(_mm_res_kernel, n_x=len(xs), with_norm=with_norm),
        grid=(m // tm, n // tn),
        in_specs=in_specs,
        out_specs=out_specs,
        out_shape=out_shape,
        compiler_params=_cparams("parallel", "parallel"),
        name="matmul_residual",
    )(*args)


def _mm_res_ksplit_kernel(x_ref, w_ref, res_ref, o_ref):
    @pl.when(pl.program_id(1) == 0)
    def _():
        o_ref[...] = res_ref[...]

    o_ref[...] += _dot(x_ref[...], w_ref[...])


def _matmul_residual_ksplit(x, w, res, *, layer, tm_cap=1024, tk_cap=1024):
    m, n = res.shape
    k = x.shape[1]
    tm = _tile(m, tm_cap, 256)
    tk = _tile(k, tk_cap, 256)
    return pl.pallas_call(
        _mm_res_ksplit_kernel,
        grid=(m // tm, k // tk),
        in_specs=[
            pl.BlockSpec((tm, tk), lambda i, kk: (i, kk)),
            pl.BlockSpec((None, tk, n), lambda i, kk: (layer, kk, 0)),
            pl.BlockSpec((tm, n), lambda i, kk: (i, 0)),
        ],
        out_specs=pl.BlockSpec((tm, n), lambda i, kk: (i, 0)),
        out_shape=jax.ShapeDtypeStruct((m, n), F32),
        compiler_params=_cparams("parallel", "arbitrary"),
        name="matmul_residual_ksplit",
    )(x, w, res)


def _ple_kernel(x_ref, g_ref, wg_ref, p_ref, wp_ref, gt_ref, o_ref, *hn_refs, final_norm):
    d = x_ref.shape[1]
    hn = _rms(x_ref[...], g_ref[...]).astype(BF16)
    pb = p_ref[...].astype(BF16)
    nh = d // 2 if d % (2 * V7X_LANES) == 0 else d
    for c0 in range(0, d, nh):
        gate = jax.nn.sigmoid(_dot(hn, wg_ref[:, c0:c0 + nh]))
        pp = _dot(pb, wp_ref[:, c0:c0 + nh])
        o_ref[:, c0:c0 + nh] = x_ref[:, c0:c0 + nh] + gate * pp
    tail = _rms(o_ref[...], gt_ref[...])
    if final_norm:
        o_ref[...] = tail
    else:
        hn_refs[0][...] = tail.astype(BF16)


def _ple(x, g, wg, p, wp, g_tail, *, layer, final_norm, tm_cap=512):
    m, d = x.shape
    t, pd = p.shape[2], p.shape[3]
    tm = _tile(t, tm_cap, 256)
    tiles_per_seq = t // tm
    row_tile = pl.BlockSpec((tm, d), lambda i: (i, 0))
    out_specs, out_shape = row_tile, jax.ShapeDtypeStruct((m, d), F32)
    if not final_norm:
        out_specs, out_shape = [row_tile, row_tile], [out_shape, jax.ShapeDtypeStruct((m, d), BF16)]
    return pl.pallas_call(
        functools.partial(_ple_kernel, final_norm=final_norm),
        grid=(m // tm,),
        in_specs=[
            row_tile,
            pl.BlockSpec((1, d), lambda i: (0, 0)),
            pl.BlockSpec((None, d, d), lambda i: (layer, 0, 0)),
            pl.BlockSpec((None, None, tm, pd), lambda i: (layer, i // tiles_per_seq, i % tiles_per_seq, 0)),
            pl.BlockSpec((None, pd, d), lambda i: (layer, 0, 0)),
            pl.BlockSpec((1, d), lambda i: (0, 0)),
        ],
        out_specs=out_specs,
        out_shape=out_shape,
        compiler_params=_cparams("parallel"),
        name="ple",
    )(x, g.reshape(1, d), wg, p, wp, g_tail.reshape(1, d))


def _causal_conv(x_ref, halo_ref, w_ref, first):
    x = x_ref[0]
    halo = jnp.where(first, 0.0, halo_ref[0])
    xp = jnp.concatenate([halo, x], axis=0)
    w = w_ref[...]
    y = pltpu.roll(xp, 3, axis=0)[V7X_SUBLANES:] * w[0:1]
    y = y + pltpu.roll(xp, 2, axis=0)[V7X_SUBLANES:] * w[1:2]
    y = y + pltpu.roll(xp, 1, axis=0)[V7X_SUBLANES:] * w[2:3]
    return y + x * w[3:4]


def _halo_map(col, rows_per_tile):
    nb = rows_per_tile // V7X_SUBLANES
    return lambda b, h, t: (b, jnp.maximum(t * nb - 1, 0), col(h))


def _unit_lower_inverses(ms, row, col):
    c = ms[0].shape[0]
    eye = (row == col).astype(F32)
    diag16 = (row >> 4) == (col >> 4)
    b16 = lambda xs: [x.astype(BF16) for x in xs]
    ns = [jnp.where(diag16, m, 0.0) for m in ms]
    invs = [eye - n for n in ns]
    qs = b16(ns)
    for step in range(3):
        qs = b16([_dot(q, q) for q in qs])
        invs = [inv + _dot(inv.astype(BF16), q) for inv, q in zip(invs, qs)]
    shift = 4
    while (1 << shift) < c:
        off = ((row >> (shift + 1)) == (col >> (shift + 1))) & ((row >> shift) != (col >> shift))
        inv16 = b16(invs)
        tmps = [_dot(inv, jnp.where(off, m, 0.0).astype(BF16)) for inv, m in zip(inv16, ms)]
        invs = [inv - _dot(tmp.astype(BF16), i16) for inv, tmp, i16 in zip(invs, tmps, inv16)]
        shift += 1
    xs = [_split_bf16(inv) for inv in invs]
    res = [eye - inv - _dot_f32(m, x) for inv, m, x in zip(invs, ms, xs)]
    return [inv + _dot_f32(x, r) for inv, x, r in zip(invs, xs, res)]


def _delta_kernel(q_ref, qh_ref, k_ref, kh_ref, v_ref, vh_ref, z_ref, ba_ref,
                  wq_ref, wk_ref, wv_ref, alog_ref, dtb_ref, nw_ref, o_ref, s_ref):
    hp = pl.program_id(1)
    first = pl.program_id(2) == 0
    tt = q_ref.shape[1]
    c = DN_CHUNK
    dh = V7X_LANES
    nheads = q_ref.shape[2] // dh
    nch = tt // c

    @pl.when(first)
    def _():
        s_ref[...] = jnp.zeros_like(s_ref)

    def l2n(x):
        return x * lax.rsqrt(jnp.sum(x * x, axis=-1, keepdims=True) + 1e-6)

    ba = ba_ref[0]
    lane = lax.broadcasted_iota(jnp.int32, ba.shape, 1)
    sig_ba = jax.nn.sigmoid(ba)
    g_all = -jnp.exp(alog_ref[...]) * jax.nn.softplus(ba + dtb_ref[...])

    row = lax.broadcasted_iota(jnp.int32, (c, c), 0)
    col = lax.broadcasted_iota(jnp.int32, (c, c), 1)
    causal = row >= col
    strict = row > col
    ltri = causal.astype(BF16)

    def cumsum_block(gblk):
        g_hi, g_lo = _split_bf16(gblk)
        g_lo2 = (gblk - g_hi.astype(F32) - g_lo.astype(F32)).astype(BF16)
        return _dot(ltri, g_hi) + (_dot(ltri, g_lo) + _dot(ltri, g_lo2))

    gc_blocks = [cumsum_block(g_all[ci * c:(ci + 1) * c]) for ci in range(nch)]
    gc_blocks_t = [gcb.T for gcb in gc_blocks]

    q_all = _silu(_causal_conv(q_ref, qh_ref, wq_ref, first))
    k_all = _silu(_causal_conv(k_ref, kh_ref, wk_ref, first))
    v_all = _silu(_causal_conv(v_ref, vh_ref, wv_ref, first))
    qs, ks, vs, bs, gcs, gc_rows = [], [], [], [], [], []
    for hi in range(nheads):
        hs = slice(hi * dh, (hi + 1) * dh)
        head = nheads * hp + hi
        qh = l2n(q_all[:, hs]) * (dh ** -0.5)
        kh = l2n(k_all[:, hs])
        vh = v_all[:, hs]
        beta = jnp.sum(jnp.where(lane == head, sig_ba, 0.0), axis=-1, keepdims=True)
        for ci in range(nch):
            sl = slice(ci * c, (ci + 1) * c)
            qs.append(qh[sl]); ks.append(kh[sl]); vs.append(vh[sl]); bs.append(beta[sl])
            gcol = jnp.sum(jnp.where(col == head + DN_HEADS, gc_blocks[ci], 0.0), axis=-1, keepdims=True)
            grow = jnp.sum(jnp.where(row == head + DN_HEADS, gc_blocks_t[ci], 0.0), axis=0, keepdims=True)
            gcs.append(jnp.broadcast_to(gcol, (c, c)))
            gc_rows.append(jnp.broadcast_to(grow, (c, c)))

    decays = [jnp.where(causal, jnp.exp(jnp.where(causal, gc - gr, 0.0)), 0.0) for gc, gr in zip(gcs, gc_rows)]
    kbs = [kc * bc for kc, bc in zip(ks, bs)]
    k16 = [kc.astype(BF16) for kc in ks]
    ms = [jnp.where(strict, _dot_nt(kb.astype(BF16), kc) * dec, 0.0) for kb, kc, dec in zip(kbs, k16, decays)]
    invs = _unit_lower_inverses(ms, row, col)
    egcs = [jnp.exp(gc) for gc in gcs]
    sols = [_dot_f32(inv, jnp.concatenate([vc * bc, kb * egc], axis=1))
            for inv, vc, bc, kb, egc in zip(invs, vs, bs, kbs, egcs)]
    us = [sol[:, :dh] for sol in sols]
    ws = [sol[:, dh:].astype(BF16) for sol in sols]
    qks = [(_dot_nt(qc.astype(BF16), kc) * dec).astype(BF16) for qc, kc, dec in zip(qs, k16, decays)]
    q_decs = [(qc * egc).astype(BF16) for qc, egc in zip(qs, egcs)]
    lasts = [gc[c - 1:c, :] for gc in gcs]
    k_decs = [(kc * jnp.exp(last - gc)).astype(BF16) for kc, last, gc in zip(ks, lasts, gcs)]
    g_tots = [jnp.exp(last) for last in lasts]

    states = [s_ref[hi] for hi in range(nheads)]
    for ci in range(nch):
        sl = slice(ci * c, (ci + 1) * c)
        for hi in range(nheads):
            i = hi * nch + ci
            hs = slice(hi * dh, (hi + 1) * dh)
            sb = states[hi].astype(BF16)
            v_new = (us[i] - _dot(ws[i], sb)).astype(BF16)
            o = _dot(q_decs[i], sb) + _dot(qks[i], v_new)
            states[hi] = states[hi] * g_tots[i] + _dot_tn(k_decs[i], v_new)
            o_ref[0, sl, hs] = (_rms(o, nw_ref[...]) * _silu(z_ref[0, sl, hs])).astype(o_ref.dtype)
    for hi in range(nheads):
        s_ref[hi] = states[hi]


def _delta_mixer(proj, conv_w, a_log, dt_bias, norm_w, *, tt_cap=256):
    b, t, _ = proj.shape
    tt = _tile(t, tt_cap, DN_CHUNK)
    hh = DN_HEADS
    hps = DN_HEADS_PER_STEP
    wd = hps * V7X_LANES
    npair = hh // hps
    blk = lambda off: pl.BlockSpec((1, tt, wd), lambda bi, h, ti, off=off: (bi, ti, off + h))
    halo = lambda off: pl.BlockSpec((1, V7X_SUBLANES, wd), _halo_map(lambda h, off=off: off + h, tt))
    cw = lambda off: pl.BlockSpec((4, wd), lambda bi, h, ti, off=off: (0, off + h))
    row_spec = pl.BlockSpec((1, V7X_LANES), lambda bi, h, ti: (0, 0))
    pad = jnp.zeros((V7X_LANES - 2 * hh,), F32)
    alog_row = jnp.concatenate([jnp.zeros((hh,), F32), a_log, pad]).reshape(1, V7X_LANES)
    dtb_row = jnp.concatenate([jnp.zeros((hh,), F32), dt_bias, pad]).reshape(1, V7X_LANES)
    ba_col = 6 * hh
    return pl.pallas_call(
        _delta_kernel,
        grid=(b, npair, t // tt),
        in_specs=[
            blk(0), halo(0), blk(npair), halo(npair), blk(2 * npair), halo(2 * npair), blk(3 * npair),
            pl.BlockSpec((1, tt, V7X_LANES), lambda bi, h, ti: (bi, ti, ba_col)),
            cw(0), cw(npair), cw(2 * npair), row_spec, row_spec, row_spec,
        ],
        out_specs=pl.BlockSpec((1, tt, wd), lambda bi, h, ti: (bi, ti, h)),
        out_shape=jax.ShapeDtypeStruct((b, t, hh * V7X_LANES), BF16),
        scratch_shapes=[pltpu.VMEM((hps, V7X_LANES, V7X_LANES), F32)],
        compiler_params=_cparams("parallel", "parallel", "arbitrary"),
        name="delta_mixer",
    )(proj, proj, proj, proj, proj, proj, proj, proj, conv_w, conv_w, conv_w,
      alog_row, dtb_row, norm_w.reshape(1, V7X_LANES))


def _lru_kernel(x_ref, xh_ref, y_ref, cw_ref, cb_ref, wa_ref, ba_ref, wx_ref, bx_ref, lam_ref,
                o_ref, h_ref):
    first = pl.program_id(2) == 0
    tt = x_ref.shape[1]

    @pl.when(first)
    def _():
        h_ref[...] = jnp.zeros_like(h_ref)

    wc = x_ref.shape[2]
    xc = _causal_conv(x_ref, xh_ref, cw_ref, first) + cb_ref[...]
    xb = xc.astype(BF16)

    def block_diag(w_ref):
        return jnp.concatenate([_dot(xb[:, g * V7X_LANES:(g + 1) * V7X_LANES], w_ref[g])
                                for g in range(wc // V7X_LANES)], axis=1)

    r = jax.nn.sigmoid(block_diag(wa_ref) + ba_ref[...])
    i = jax.nn.sigmoid(block_diag(wx_ref) + bx_ref[...])
    log_a = -LRU_C * r * jax.nn.softplus(-lam_ref[...])
    a = jnp.exp(log_a)
    u = jnp.sqrt(1.0 - a * a) * (i * xc)

    hr = V7X_SUBLANES
    rowg = lax.broadcasted_iota(jnp.int32, (tt, wc), 0) & (hr - 1)
    s = 1
    while s < hr:
        a_sh = jnp.where(rowg >= s, pltpu.roll(a, s, axis=0), 1.0)
        u_sh = jnp.where(rowg >= s, pltpu.roll(u, s, axis=0), 0.0)
        u = a * u_sh + u
        a = a * a_sh
        s *= 2
    carry = h_ref[...]
    gelu_y = jax.nn.gelu(y_ref[0])
    out_rows = 2 * hr
    for g0 in range(0, tt, out_rows):
        parts = []
        for g in range(g0, g0 + out_rows, hr):
            hs = u[g:g + hr] + a[g:g + hr] * carry
            carry = hs[hr - 1:hr]
            parts.append(hs)
        rows = slice(g0, g0 + out_rows)
        o_ref[0, rows, :] = (jnp.concatenate(parts, axis=0) * gelu_y[rows]).astype(o_ref.dtype)
    h_ref[...] = carry


def _lru_mixer(proj, conv_w, conv_b, wa, ba, wx, bx, lam, *, x_col, y_col, tt_cap=512):
    b, t, _ = proj.shape
    tt = _tile(t, tt_cap, 16)
    gg = LRU_BLOCKS
    gs = LRU_BLOCKS_PER_STEP
    wc = gs * V7X_LANES
    assert gg % gs == 0 and x_col % gs == 0 and y_col % gs == 0
    vec = lambda a: a.reshape(1, gg * V7X_LANES)
    vspec = pl.BlockSpec((1, wc), lambda bi, g, ti: (0, g))
    wspec = pl.BlockSpec((gs, V7X_LANES, V7X_LANES), lambda bi, g, ti: (g, 0, 0))
    return pl.pallas_call(
        _lru_kernel,
        grid=(b, gg // gs, t // tt),
        in_specs=[
            pl.BlockSpec((1, tt, wc), lambda bi, g, ti: (bi, ti, x_col // gs + g)),
            pl.BlockSpec((1, V7X_SUBLANES, wc), _halo_map(lambda g: x_col // gs + g, tt)),
            pl.BlockSpec((1, tt, wc), lambda bi, g, ti: (bi, ti, y_col // gs + g)),
            pl.BlockSpec((4, wc), lambda bi, g, ti: (0, g)),
            vspec, wspec, vspec, wspec, vspec, vspec,
        ],
        out_specs=pl.BlockSpec((1, tt, wc), lambda bi, g, ti: (bi, ti, g)),
        out_shape=jax.ShapeDtypeStruct((b, t, gg * V7X_LANES), BF16),
        scratch_shapes=[pltpu.VMEM((1, wc), F32)],
        compiler_params=_cparams("parallel", "parallel", "arbitrary"),
        name="lru_mixer",
    )(proj, proj, proj, conv_w, vec(conv_b), wa.astype(BF16), vec(ba), wx.astype(BF16), vec(bx), vec(lam))


def _dilated_kernel(q_ref, kc_ref, kp_ref, vc_ref, vp_ref, o_ref, qa_ref, ka_ref, va_ref, sa_ref, sn_ref):
    h = pl.program_id(1)
    has_prev = pl.program_id(2) > 0
    tq = q_ref.shape[1]
    bq = SWA_BLOCK
    d1, d2, d3 = SWA_DILATIONS
    assert d1 == 1 and d3 % d2 == 0
    pw = kp_ref.shape[1]
    qq = tq // d2
    pq = pw // d2
    cs = pq + qq
    st = d3 // d2
    assert pw == bq * d3 and pq >= bq * st
    for r in range(d2):
        qa_ref[r * qq:(r + 1) * qq, :] = q_ref.at[0][pl.ds(r, qq, stride=d2), :]
        for src_p, src_c, dst in ((kp_ref, kc_ref, ka_ref), (vp_ref, vc_ref, va_ref)):
            dst[r * cs:r * cs + pq, :] = src_p.at[0][pl.ds(r, pq, stride=d2), :]
            dst[r * cs + pq:(r + 1) * cs, :] = src_c.at[0][pl.ds(r, qq, stride=d2), :]

    slope = jnp.exp2(-(jnp.full((1, 1), h, jnp.int32).astype(F32) + 1.0) * (8.0 / SWA_HEADS))
    iq = lax.broadcasted_iota(jnp.int32, (bq, 2 * bq), 0)
    ik = lax.broadcasted_iota(jnp.int32, (bq, 2 * bq), 1)
    rel = bq + iq - ik
    in_window = (rel >= 0) & (rel <= SWA_SPAN)
    relf = rel.astype(F32)
    scale = V7X_LANES ** -0.5
    ones_blk = jnp.ones((2 * bq, V7X_LANES), BF16)

    def attend(q_rows, kcat, vcat, d, prev_ok):
        qb = (q_rows * scale).astype(BF16)
        s = _dot_nt(qb, kcat.astype(BF16)) - (slope * float(d)) * relf
        valid = in_window if prev_ok is True else in_window & ((ik >= bq) | prev_ok)
        s = jnp.where(valid, s, MASK_VALUE)
        m_b = jnp.max(s, axis=-1, keepdims=True)
        p = jnp.exp(s - m_b).astype(BF16)
        pv = _dot(p, jnp.concatenate([vcat.astype(BF16), ones_blk], axis=1))
        return jnp.broadcast_to(m_b, (bq, V7X_LANES)), pv[:, V7X_LANES:], pv[:, :V7X_LANES]

    def merge(old, new):
        (m_o, l_o, acc_o), (m_b, l_b, acc_b) = old, new
        m_n = jnp.maximum(m_o, m_b)
        alpha = jnp.exp(m_o - m_n)
        beta = jnp.exp(m_b - m_n)
        return m_n, alpha * l_o + beta * l_b, alpha * acc_o + beta * acc_b

    for r in range(d2):
        for n in range(qq // bq):
            q0 = r * qq + n * bq
            k0 = r * cs + pq + (n - 1) * bq
            new = attend(qa_ref[q0:q0 + bq, :], ka_ref[k0:k0 + 2 * bq, :], va_ref[k0:k0 + 2 * bq, :],
                         d2, has_prev if n == 0 else True)
            for kk in range(3):
                sa_ref[kk, q0:q0 + bq, :] = new[kk]

    take = lambda ref, start: ref[pl.ds(start, bq, stride=st), :]
    for r3 in range(d3):
        for n in range(tq // (bq * d3)):
            r = r3 % d2
            p0 = r3 // d2 + n * (bq * st)
            q0 = r * qq + p0
            k0 = r * cs + pq + p0
            kcat = jnp.concatenate([take(ka_ref, k0 - bq * st), take(ka_ref, k0)], axis=0)
            vcat = jnp.concatenate([take(va_ref, k0 - bq * st), take(va_ref, k0)], axis=0)
            new = attend(take(qa_ref, q0), kcat, vcat, d3, has_prev if n == 0 else True)
            out = merge(tuple(take(sa_ref.at[kk], q0) for kk in range(3)), new)
            for kk in range(3):
                sa_ref.at[kk][pl.ds(q0, bq, stride=st), :] = out[kk]

    for r in range(d2):
        for kk in range(3):
            sn_ref.at[kk][pl.ds(r, qq, stride=d2), :] = sa_ref[kk, r * qq:(r + 1) * qq, :]

    for n in range(tq // bq):
        base = n * bq
        if n == 0:
            kcat = jnp.concatenate([kp_ref[0, pw - bq:pw, :], kc_ref[0, 0:bq, :]], axis=0)
            vcat = jnp.concatenate([vp_ref[0, pw - bq:pw, :], vc_ref[0, 0:bq, :]], axis=0)
        else:
            kcat, vcat = kc_ref[0, base - bq:base + bq, :], vc_ref[0, base - bq:base + bq, :]
        new = attend(q_ref[0, base:base + bq, :], kcat, vcat, d1, has_prev if n == 0 else True)
        _, l_f, acc_f = merge(tuple(sn_ref[kk, base:base + bq, :] for kk in range(3)), new)
        o_ref[0, base:base + bq, :] = (acc_f / l_f).astype(o_ref.dtype)


def _dilated_mixer(proj, *, q_col, k_col, v_col, tq=SWA_TILE):
    b, t, _ = proj.shape
    pw = SWA_BLOCK * max(SWA_DILATIONS)
    tq = min(tq, t)
    assert t % tq == 0 and tq % pw == 0
    hh = SWA_HEADS
    cur = lambda off: pl.BlockSpec((1, tq, V7X_LANES), lambda bi, h, ti, off=off: (bi, ti, off + h))
    prev = lambda off: pl.BlockSpec(
        (1, pw, V7X_LANES), lambda bi, h, ti, off=off: (bi, jnp.maximum(ti * (tq // pw) - 1, 0), off + h))
    return pl.pallas_call(
        _dilated_kernel,
        grid=(b, hh, t // tq),
        in_specs=[cur(q_col), cur(k_col), prev(k_col), cur(v_col), prev(v_col)],
        out_specs=pl.BlockSpec((1, tq, V7X_LANES), lambda bi, h, ti: (bi, ti, h)),
        out_shape=jax.ShapeDtypeStruct((b, t, hh * V7X_LANES), BF16),
        scratch_shapes=[pltpu.VMEM((tq, V7X_LANES), F32), pltpu.VMEM((pw + tq, V7X_LANES), F32),
                        pltpu.VMEM((pw + tq, V7X_LANES), F32), pltpu.VMEM((3, tq, V7X_LANES), F32),
                        pltpu.VMEM((3, tq, V7X_LANES), F32)],
        compiler_params=_cparams("parallel", "parallel", "parallel"),
        name="dilated_mixer",
    )(proj, proj, proj, proj, proj)


def _retention_kernel(q_ref, k_ref, v_ref, g_ref, o_ref, s_ref):
    h = pl.program_id(1)
    tt = q_ref.shape[1]
    c = min(RET_CHUNK, tt)
    nch = tt // c

    @pl.when(pl.program_id(2) == 0)
    def _():
        s_ref[...] = jnp.zeros_like(s_ref)

    hf = jnp.full((1, 1), h, jnp.int32).astype(F32)
    log_gamma = jnp.log1p(-jnp.exp2(-5.0 - hf))
    row = lax.broadcasted_iota(jnp.int32, (c, c), 0)
    col = lax.broadcasted_iota(jnp.int32, (c, c), 1)
    rel = (row - col).astype(F32)
    dmask = jnp.where(rel >= 0, jnp.exp(jnp.maximum(rel, 0.0) * log_gamma), 0.0)
    idx = lax.broadcasted_iota(jnp.int32, (c, 1), 0).astype(F32)
    q_scale = jnp.exp((idx + 1.0) * log_gamma)
    k_scale = jnp.exp((c - 1.0 - idx) * log_gamma)
    chunk_decay = jnp.exp(float(c) * log_gamma)

    sls = [slice(ci * c, (ci + 1) * c) for ci in range(nch)]
    qs = [q_ref[0, sl, :] for sl in sls]
    ks = [k_ref[0, sl, :] * (V7X_LANES ** -0.5) for sl in sls]
    vs = [v_ref[0, sl, :].astype(BF16) for sl in sls]
    intras = [_dot((_dot_nt(q.astype(BF16), k.astype(BF16)) * dmask).astype(BF16), v)
              for q, k, v in zip(qs, ks, vs)]
    kvs = [_dot_tn((k * k_scale).astype(BF16), v) for k, v in zip(ks, vs)]
    states = [s_ref[...]]
    for kv in kvs:
        states.append(states[-1] * chunk_decay + kv)
    s_ref[...] = states[-1]
    for sl, q, intra, state in zip(sls, qs, intras, states):
        o = intra + _dot((q * q_scale).astype(BF16), state.astype(BF16))
        mu = jnp.mean(o, axis=-1, keepdims=True)
        oc = o - mu
        o = oc * lax.rsqrt(jnp.mean(oc * oc, axis=-1, keepdims=True) + GN_EPS)
        o_ref[0, sl, :] = (o * _silu(g_ref[0, sl, :])).astype(o_ref.dtype)


def _retention_mixer(proj, *, q_col, k_col, v_col, g_col):
    b, t, _ = proj.shape
    c = _tile(t, RET_TILE, RET_CHUNK)
    hh = RET_HEADS
    wide = RET_DV // V7X_LANES
    assert v_col % wide == 0 and g_col % wide == 0
    nar = lambda off: pl.BlockSpec((1, c, V7X_LANES), lambda bi, h, ti, off=off: (bi, ti, off + h))
    wid = lambda off: pl.BlockSpec((1, c, RET_DV), lambda bi, h, ti, off=off: (bi, ti, off // wide + h))
    return pl.pallas_call(
        _retention_kernel,
        grid=(b, hh, t // c),
        in_specs=[nar(q_col), nar(k_col), wid(v_col), wid(g_col)],
        out_specs=pl.BlockSpec((1, c, RET_DV), lambda bi, h, ti: (bi, ti, h)),
        out_shape=jax.ShapeDtypeStruct((b, t, hh * RET_DV), BF16),
        scratch_shapes=[pltpu.VMEM((V7X_LANES, RET_DV), F32)],
        compiler_params=_cparams("parallel", "parallel", "arbitrary"),
        name="retention_mixer",
    )(proj, proj, proj, proj)


def _even_w_in_kernel(x_ref, o_ref):
    qkvz = 4 * DN_HEADS * V7X_LANES
    nba = 2 * DN_HEADS
    lru = 2 * LRU_BLOCKS * V7X_LANES
    rows = x_ref.shape[0]
    o_ref[:, :qkvz] = x_ref[:, :qkvz].astype(BF16)
    o_ref[:, qkvz:qkvz + lru] = x_ref[:, qkvz + nba:qkvz + nba + lru].astype(BF16)
    tail = jnp.concatenate([x_ref[:, qkvz:qkvz + nba], jnp.zeros((rows, 2 * V7X_LANES - nba), F32)], axis=1)
    o_ref[:, qkvz + lru:] = tail.astype(BF16)


def _even_w_in_bf16(w):
    l, d, c = w.shape
    n = 4 * DN_HEADS * V7X_LANES + 2 * LRU_BLOCKS * V7X_LANES + 2 * V7X_LANES
    tr = _tile(d, 256, 16)
    return pl.pallas_call(
        _even_w_in_kernel,
        grid=(l, d // tr),
        in_specs=[pl.BlockSpec((None, tr, c), lambda i, j: (i, j, 0))],
        out_specs=pl.BlockSpec((None, tr, n), lambda i, j: (i, j, 0)),
        out_shape=jax.ShapeDtypeStruct((l, d, n), BF16),
        compiler_params=_cparams("parallel", "parallel"),
        name="even_w_in_bf16",
    )(w)


def kernel(x, p, ln_mix_w, ln_mlp_w, ln_ple_w, w_up, w_down, w_ple_proj, w_ple_gate, ln_final_w,
           ev_w_in, ev_w_out, dn_conv_w, dn_a_log, dn_dt_bias, dn_norm_w,
           lru_conv_w, lru_conv_b, lru_wa, lru_ba, lru_wx, lru_bx, lru_lambda,
           od_w_in, od_w_out):
    b, t, d = x.shape
    depth = ln_mix_w.shape[0]
    m = b * t
    h = x.reshape(m, d)
    w_ple_proj, w_ple_gate, ev_w_out, od_w_in, od_w_out = (
        _to_bf16(w) for w in (w_ple_proj, w_ple_gate, ev_w_out, od_w_in, od_w_out))
    ev_w_in = _even_w_in_bf16(ev_w_in)
    w_up = _to_bf16(w_up, tn=MLP_UP_TN)
    w_down = _to_bf16(w_down)
    hn, mix_norm_w = h, ln_mix_w[0]
    for i in range(depth):
        j = i // 2
        if i % 2 == 0:
            proj = _norm_matmul(hn, mix_norm_w, ev_w_in, layer=j, act=None, out_dtype=F32)
            proj = proj.reshape(b, t, -1)
            y_a = _delta_mixer(proj, dn_conv_w[j], dn_a_log[j], dn_dt_bias[j], dn_norm_w[j])
            y_b = _lru_mixer(proj, lru_conv_w[j], lru_conv_b[j], lru_wa[j], lru_ba[j], lru_wx[j], lru_bx[j],
                             lru_lambda[j], x_col=4 * DN_HEADS, y_col=4 * DN_HEADS + LRU_BLOCKS)
            w_out = ev_w_out
        else:
            proj = _norm_matmul(hn, mix_norm_w, od_w_in, layer=j, act=None, out_dtype=F32)
            proj = proj.reshape(b, t, -1)
            y_a = _dilated_mixer(proj, q_col=0, k_col=SWA_HEADS, v_col=2 * SWA_HEADS)
            y_b = _retention_mixer(proj, q_col=3 * SWA_HEADS, k_col=3 * SWA_HEADS + RET_HEADS,
                                   v_col=3 * SWA_HEADS + 2 * RET_HEADS,
                                   g_col=3 * SWA_HEADS + 2 * RET_HEADS + RET_HEADS * RET_DV // V7X_LANES)
            w_out = od_w_out
        h, hn = _matmul_residual([y_a.reshape(m, -1), y_b.reshape(m, -1)], w_out, h, layer=j,
                                 next_norm_w=ln_mlp_w[i], tn_cap=d)
        up = _norm_matmul(hn, None, w_up, layer=i, act="relu2", out_dtype=BF16, tm_cap=MLP_UP_TM)
        h = _matmul_residual_ksplit(up, w_down, h, layer=i)
        if i == depth - 1:
            return _ple(h, ln_ple_w[i], w_ple_gate, p, w_ple_proj, ln_final_w, layer=i,
                        final_norm=True).reshape(b, t, d)
        h, hn = _ple(h, ln_ple_w[i], w_ple_gate, p, w_ple_proj, ln_mix_w[i + 1], layer=i, final_norm=False)
        mix_norm_w = None
```

```python
import functools

import jax
import jax.numpy as jnp
from jax import lax
from jax.experimental import pallas as pl
from jax.experimental.pallas import tpu as pltpu

F32 = jnp.float32
BF16 = jnp.bfloat16

V7X_LANES = 128
V7X_SUBLANES = 8
V7X_VMEM_LIMIT_BYTES = 56 * 1024 * 1024

NORM_EPS = 1e-6
GN_EPS = 1e-5
LRU_C = 8.0
DN_HEADS = 8
LRU_BLOCKS = 8
LRU_BLOCKS_PER_STEP = 4
SWA_HEADS = 8
RET_HEADS = 4
RET_DV = 256
SWA_DILATIONS = (1, 4, 16)
SWA_SPAN = 128
SWA_BLOCK = 128
SWA_TILE = 2048
DN_CHUNK = 128
DN_HEADS_PER_STEP = 8
RET_CHUNK = 256
RET_TILE = 1024
MASK_VALUE = -1e30
MLP_UP_TN = 2048
MLP_UP_TM = 1024


def _cparams(*sem):
    return pltpu.CompilerParams(dimension_semantics=sem, vmem_limit_bytes=V7X_VMEM_LIMIT_BYTES)


def _dot(a, b):
    return jnp.dot(a, b, preferred_element_type=F32)


def _dot_nt(a, b):
    return lax.dot_general(a, b, (((1,), (1,)), ((), ())), preferred_element_type=F32)


def _dot_tn(a, b):
    return lax.dot_general(a, b, (((0,), (0,)), ((), ())), preferred_element_type=F32)


def _split_bf16(x):
    hi = x.astype(BF16)
    lo = (x - hi.astype(F32)).astype(BF16)
    return hi, lo


def _dot_f32(a, b):
    ah, al = a if isinstance(a, tuple) else _split_bf16(a)
    bh, bl = b if isinstance(b, tuple) else _split_bf16(b)
    n = bh.shape[1]
    wide = _dot(ah, jnp.concatenate([bh, bl], axis=1))
    return wide[:, :n] + (wide[:, n:] + _dot(al, bh))


def _rms(x, w):
    return x * lax.rsqrt(jnp.mean(x * x, axis=-1, keepdims=True) + NORM_EPS) * w


def _silu(x):
    return x * jax.nn.sigmoid(x)


def _tile(n, cap, quantum):
    if n <= cap:
        return n
    best = None
    for c in range(quantum, cap + 1, quantum):
        if n % c == 0:
            best = c
    assert best is not None, (n, cap, quantum)
    return best


CAST_BLOCK_BYTES = 8 * 1024 * 1024


def _cast_kernel(x_ref, o_ref):
    o_ref[...] = x_ref[...].astype(o_ref.dtype)


def _to_bf16(w, tn=None):
    l, r, c = w.shape
    tn = c if tn is None else tn
    tr = _tile(r, max(16, CAST_BLOCK_BYTES // (4 * tn) // 16 * 16), 16)
    out = pl.pallas_call(
        _cast_kernel,
        grid=(l, c // tn, r // tr),
        in_specs=[pl.BlockSpec((None, tr, tn), lambda i, j, k: (i, k, j))],
        out_specs=pl.BlockSpec((None, None, tr, tn), lambda i, j, k: (i, j, k, 0)),
        out_shape=jax.ShapeDtypeStruct((l, c // tn, r, tn), BF16),
        compiler_params=_cparams("parallel", "parallel", "parallel"),
        name="to_bf16",
    )(w)
    return out if tn != c else out.reshape(l, r, c)


def _w_spec(w, layer, kk, tn, row_block=0):
    if w.ndim == 2:
        return pl.BlockSpec((kk, tn), lambda i, j: (row_block, j))
    if w.ndim == 3:
        return pl.BlockSpec((None, kk, tn), lambda i, j: (layer, row_block, j))
    assert w.shape[3] == tn, (w.shape, tn)
    return pl.BlockSpec((None, None, kk, tn), lambda i, j: (layer, j, row_block, 0))


def _w_cols(w):
    return w.shape[-1] if w.ndim < 4 else w.shape[1] * w.shape[3]


def _norm_mm_kernel(x_ref, g_ref, w_ref, o_ref, hn_ref, *, act):
    @pl.when(pl.program_id(1) == 0)
    def _():
        hn_ref[...] = _rms(x_ref[...], g_ref[...]).astype(BF16)

    _mm_act_kernel(hn_ref, w_ref, o_ref, act=act)


def _mm_act_kernel(x_ref, w_ref, o_ref, *, act):
    a = _dot(x_ref[...], w_ref[...])
    if act == "relu2":
        a = jnp.square(jnp.maximum(a, 0.0))
    o_ref[...] = a.astype(o_ref.dtype)


def _norm_matmul(x, g, w, *, act, out_dtype, layer=0, tm_cap=1024, tn_cap=1536):
    m, k = x.shape
    n = _w_cols(w)
    tm = _tile(m, tm_cap, 256)
    tn = w.shape[3] if w.ndim == 4 else _tile(n, tn_cap, 256)
    x_spec = pl.BlockSpec((tm, k), lambda i, j: (i, 0))
    common = dict(
        grid=(m // tm, n // tn),
        out_specs=pl.BlockSpec((tm, tn), lambda i, j: (i, j)),
        out_shape=jax.ShapeDtypeStruct((m, n), out_dtype),
    )
    if g is None:
        assert x.dtype == BF16
        return pl.pallas_call(
            functools.partial(_mm_act_kernel, act=act),
            in_specs=[x_spec, _w_spec(w, layer, k, tn)],
            compiler_params=_cparams("parallel", "parallel"),
            name="matmul_act", **common,
        )(x, w)
    return pl.pallas_call(
        functools.partial(_norm_mm_kernel, act=act),
        in_specs=[x_spec, pl.BlockSpec((1, k), lambda i, j: (0, 0)), _w_spec(w, layer, k, tn)],
        scratch_shapes=[pltpu.VMEM((tm, k), BF16)],
        compiler_params=_cparams("parallel", "arbitrary"),
        name="norm_matmul", **common,
    )(x, g.reshape(1, k), w)


def _mm_res_kernel(*refs, n_x, with_norm):
    xs, ws = refs[:n_x], refs[n_x:2 * n_x]
    res_ref = refs[2 * n_x]
    outs = refs[2 * n_x + 1:]
    acc = res_ref[...]
    for x_ref, w_ref in zip(xs, ws):
        acc = acc + _dot(x_ref[...], w_ref[...])
    if with_norm:
        g_ref, o_ref, hn_ref = outs
        hn_ref[...] = _rms(acc, g_ref[...]).astype(BF16)
    else:
        o_ref, = outs
    o_ref[...] = acc


def _matmul_residual(xs, w, res, *, layer=0, next_norm_w=None, tm_cap=512, tn_cap=512):
    m, n = res.shape
    assert n == _w_cols(w)
    tm = _tile(m, tm_cap, 256)
    tn = w.shape[3] if w.ndim == 4 else _tile(n, tn_cap, 256)
    with_norm = next_norm_w is not None
    assert not with_norm or tn == n
    in_specs, row = [], 0
    for x in xs:
        in_specs.append(pl.BlockSpec((tm, x.shape[1]), lambda i, j: (i, 0)))
    w_specs = []
    for x in xs:
        kk = x.shape[1]
        assert row % kk == 0
        w_specs.append(_w_spec(w, layer, kk, tn, row_block=row // kk))
        row += kk
    assert row == w.shape[-2]
    tile = pl.BlockSpec((tm, tn), lambda i, j: (i, j))
    in_specs = in_specs + w_specs + [tile]
    args = [*xs, *([w] * len(xs)), res]
    out_specs, out_shape = tile, jax.ShapeDtypeStruct((m, n), F32)
    if with_norm:
        in_specs.append(pl.BlockSpec((1, n), lambda i, j: (0, 0)))
        args.append(next_norm_w.reshape(1, n))
        out_specs, out_shape = [tile, tile], [out_shape, jax.ShapeDtypeStruct((m, n), BF16)]
    return pl.pallas_call(
        functools.partial(_mm_res_kernel, n_x=len(xs), with_norm=with_norm),
        grid=(m // tm, n // tn),
        in_specs=in_specs,
        out_specs=out_specs,
        out_shape=out_shape,
        compiler_params=_cparams("parallel", "parallel"),
        name="matmul_residual",
    )(*args)


def _mm_res_ksplit_kernel(x_ref, w_ref, res_ref, o_ref):
    @pl.when(pl.program_id(1) == 0)
    def _():
        o_ref[...] = res_ref[...]

    o_ref[...] += _dot(x_ref[...], w_ref[...])


def _matmul_residual_ksplit(x, w, res, *, layer, tm_cap=1024, tk_cap=1024):
    m, n = res.shape
    k = x.shape[1]
    tm = _tile(m, tm_cap, 256)
    tk = _tile(k, tk_cap, 256)
    return pl.pallas_call(
        _mm_res_ksplit_kernel,
        grid=(m // tm, k // tk),
        in_specs=[
            pl.BlockSpec((tm, tk), lambda i, kk: (i, kk)),
            pl.BlockSpec((None, tk, n), lambda i, kk: (layer, kk, 0)),
            pl.BlockSpec((tm, n), lambda i, kk: (i, 0)),
        ],
        out_specs=pl.BlockSpec((tm, n), lambda i, kk: (i, 0)),
        out_shape=jax.ShapeDtypeStruct((m, n), F32),
        compiler_params=_cparams("parallel", "arbitrary"),
        name="matmul_residual_ksplit",
    )(x, w, res)


def _ple_kernel(x_ref, g_ref, wg_ref, p_ref, wp_ref, gt_ref, o_ref, *hn_refs, final_norm):
    d = x_ref.shape[1]
    hn = _rms(x_ref[...], g_ref[...]).astype(BF16)
    pb = p_ref[...].astype(BF16)
    nh = d // 2 if d % (2 * V7X_LANES) == 0 else d
    for c0 in range(0, d, nh):
        gate = jax.nn.sigmoid(_dot(hn, wg_ref[:, c0:c0 + nh]))
        pp = _dot(pb, wp_ref[:, c0:c0 + nh])
        o_ref[:, c0:c0 + nh] = x_ref[:, c0:c0 + nh] + gate * pp
    tail = _rms(o_ref[...], gt_ref[...])
    if final_norm:
        o_ref[...] = tail
    else:
        hn_refs[0][...] = tail.astype(BF16)


def _ple(x, g, wg, p, wp, g_tail, *, layer, final_norm, tm_cap=512):
    m, d = x.shape
    t, pd = p.shape[2], p.shape[3]
    tm = _tile(t, tm_cap, 256)
    tiles_per_seq = t // tm
    row_tile = pl.BlockSpec((tm, d), lambda i: (i, 0))
    out_specs, out_shape = row_tile, jax.ShapeDtypeStruct((m, d), F32)
    if not final_norm:
        out_specs, out_shape = [row_tile, row_tile], [out_shape, jax.ShapeDtypeStruct((m, d), BF16)]
    return pl.pallas_call(
        functools.partial(_ple_kernel, final_norm=final_norm),
        grid=(m // tm,),
        in_specs=[
            row_tile,
            pl.BlockSpec((1, d), lambda i: (0, 0)),
            pl.BlockSpec((None, d, d), lambda i: (layer, 0, 0)),
            pl.BlockSpec((None, None, tm, pd), lambda i: (layer, i // tiles_per_seq, i % tiles_per_seq, 0)),
            pl.BlockSpec((None, pd, d), lambda i: (layer, 0, 0)),
            pl.BlockSpec((1, d), lambda i: (0, 0)),
        ],
        out_specs=out_specs,
        out_shape=out_shape,
        compiler_params=_cparams("parallel"),
        name="ple",
    )(x, g.reshape(1, d), wg, p, wp, g_tail.reshape(1, d))


def _causal_conv(x_ref, halo_ref, w_ref, first):
    x = x_ref[0]
    halo = jnp.where(first, 0.0, halo_ref[0])
    xp = jnp.concatenate([halo, x], axis=0)
    w = w_ref[...]
    y = pltpu.roll(xp, 3, axis=0)[V7X_SUBLANES:] * w[0:1]
    y = y + pltpu.roll(xp, 2, axis=0)[V7X_SUBLANES:] * w[1:2]
    y = y + pltpu.roll(xp, 1, axis=0)[V7X_SUBLANES:] * w[2:3]
    return y + x * w[3:4]


def _halo_map(col, rows_per_tile):
    nb = rows_per_tile // V7X_SUBLANES
    return lambda b, h, t: (b, jnp.maximum(t * nb - 1, 0), col(h))


def _unit_lower_inverses(ms, row, col):
    c = ms[0].shape[0]
    eye = (row == col).astype(F32)
    diag16 = (row >> 4) == (col >> 4)
    b16 = lambda xs: [x.astype(BF16) for x in xs]
    ns = [jnp.where(diag16, m, 0.0) for m in ms]
    invs = [eye - n for n in ns]
    qs = b16(ns)
    for step in range(3):
        qs = b16([_dot(q, q) for q in qs])
        invs = [inv + _dot(inv.astype(BF16), q) for inv, q in zip(invs, qs)]
    shift = 4
    while (1 << shift) < c:
        off = ((row >> (shift + 1)) == (col >> (shift + 1))) & ((row >> shift) != (col >> shift))
        inv16 = b16(invs)
        tmps = [_dot(inv, jnp.where(off, m, 0.0).astype(BF16)) for inv, m in zip(inv16, ms)]
        invs = [inv - _dot(tmp.astype(BF16), i16) for inv, tmp, i16 in zip(invs, tmps, inv16)]
        shift += 1
    xs = [_split_bf16(inv) for inv in invs]
    res = [eye - inv - _dot_f32(m, x) for inv, m, x in zip(invs, ms, xs)]
    return [inv + _dot_f32(x, r) for inv, x, r in zip(invs, xs, res)]


def _delta_kernel(q_ref, qh_ref, k_ref, kh_ref, v_ref, vh_ref, z_ref, ba_ref,
                  wq_ref, wk_ref, wv_ref, alog_ref, dtb_ref, nw_ref, o_ref, s_ref):
    hp = pl.program_id(1)
    first = pl.program_id(2) == 0
    tt = q_ref.shape[1]
    c = DN_CHUNK
    dh = V7X_LANES
    nheads = q_ref.shape[2] // dh
    nch = tt // c

    @pl.when(first)
    def _():
        s_ref[...] = jnp.zeros_like(s_ref)

    def l2n(x):
        return x * lax.rsqrt(jnp.sum(x * x, axis=-1, keepdims=True) + 1e-6)

    ba = ba_ref[0]
    lane = lax.broadcasted_iota(jnp.int32, ba.shape, 1)
    sig_ba = jax.nn.sigmoid(ba)
    g_all = -jnp.exp(alog_ref[...]) * jax.nn.softplus(ba + dtb_ref[...])

    row = lax.broadcasted_iota(jnp.int32, (c, c), 0)
    col = lax.broadcasted_iota(jnp.int32, (c, c), 1)
    causal = row >= col
    strict = row > col
    ltri = causal.astype(BF16)

    def cumsum_block(gblk):
        g_hi, g_lo = _split_bf16(gblk)
        g_lo2 = (gblk - g_hi.astype(F32) - g_lo.astype(F32)).astype(BF16)
        return _dot(ltri, g_hi) + (_dot(ltri, g_lo) + _dot(ltri, g_lo2))

    gc_blocks = [cumsum_block(g_all[ci * c:(ci + 1) * c]) for ci in range(nch)]
    gc_blocks_t = [gcb.T for gcb in gc_blocks]

    q_all = _silu(_causal_conv(q_ref, qh_ref, wq_ref, first))
    k_all = _silu(_causal_conv(k_ref, kh_ref, wk_ref, first))
    v_all = _silu(_causal_conv(v_ref, vh_ref, wv_ref, first))
    qs, ks, vs, bs, gcs, gc_rows = [], [], [], [], [], []
    for hi in range(nheads):
        hs = slice(hi * dh, (hi + 1) * dh)
        head = nheads * hp + hi
        qh = l2n(q_all[:, hs]) * (dh ** -0.5)
        kh = l2n(k_all[:, hs])
        vh = v_all[:, hs]
        beta = jnp.sum(jnp.where(lane == head, sig_ba, 0.0), axis=-1, keepdims=True)
        for ci in range(nch):
            sl = slice(ci * c, (ci + 1) * c)
            qs.append(qh[sl]); ks.append(kh[sl]); vs.append(vh[sl]); bs.append(beta[sl])
            gcol = jnp.sum(jnp.where(col == head + DN_HEADS, gc_blocks[ci], 0.0), axis=-1, keepdims=True)
            grow = jnp.sum(jnp.where(row == head + DN_HEADS, gc_blocks_t[ci], 0.0), axis=0, keepdims=True)
            gcs.append(jnp.broadcast_to(gcol, (c, c)))
            gc_rows.append(jnp.broadcast_to(grow, (c, c)))

    decays = [jnp.where(causal, jnp.exp(jnp.where(causal, gc - gr, 0.0)), 0.0) for gc, gr in zip(gcs, gc_rows)]
    kbs = [kc * bc for kc, bc in zip(ks, bs)]
    k16 = [kc.astype(BF16) for kc in ks]
    ms = [jnp.where(strict, _dot_nt(kb.astype(BF16), kc) * dec, 0.0) for kb, kc, dec in zip(kbs, k16, decays)]
    invs = _unit_lower_inverses(ms, row, col)
    egcs = [jnp.exp(gc) for gc in gcs]
    sols = [_dot_f32(inv, jnp.concatenate([vc * bc, kb * egc], axis=1))
            for inv, vc, bc, kb, egc in zip(invs, vs, bs, kbs, egcs)]
    us = [sol[:, :dh] for sol in sols]
    ws = [sol[:, dh:].astype(BF16) for sol in sols]
    qks = [(_dot_nt(qc.astype(BF16), kc) * dec).astype(BF16) for qc, kc, dec in zip(qs, k16, decays)]
    q_decs = [(qc * egc).astype(BF16) for qc, egc in zip(qs, egcs)]
    lasts = [gc[c - 1:c, :] for gc in gcs]
    k_decs = [(kc * jnp.exp(last - gc)).astype(BF16) for kc, last, gc in zip(ks, lasts, gcs)]
    g_tots = [jnp.exp(last) for last in lasts]

    states = [s_ref[hi] for hi in range(nheads)]
    for ci in range(nch):
        sl = slice(ci * c, (ci + 1) * c)
        for hi in range(nheads):
            i = hi * nch + ci
            hs = slice(hi * dh, (hi + 1) * dh)
            sb = states[hi].astype(BF16)
            v_new = (us[i] - _dot(ws[i], sb)).astype(BF16)
            o = _dot(q_decs[i], sb) + _dot(qks[i], v_new)
            states[hi] = states[hi] * g_tots[i] + _dot_tn(k_decs[i], v_new)
            o_ref[0, sl, hs] = (_rms(o, nw_ref[...]) * _silu(z_ref[0, sl, hs])).astype(o_ref.dtype)
    for hi in range(nheads):
        s_ref[hi] = states[hi]


def _delta_mixer(proj, conv_w, a_log, dt_bias, norm_w, *, tt_cap=256):
    b, t, _ = proj.shape
    tt = _tile(t, tt_cap, DN_CHUNK)
    hh = DN_HEADS
    hps = DN_HEADS_PER_STEP
    wd = hps * V7X_LANES
    npair = hh // hps
    blk = lambda off: pl.BlockSpec((1, tt, wd), lambda bi, h, ti, off=off: (bi, ti, off + h))
    halo = lambda off: pl.BlockSpec((1, V7X_SUBLANES, wd), _halo_map(lambda h, off=off: off + h, tt))
    cw = lambda off: pl.BlockSpec((4, wd), lambda bi, h, ti, off=off: (0, off + h))
    row_spec = pl.BlockSpec((1, V7X_LANES), lambda bi, h, ti: (0, 0))
    pad = jnp.zeros((V7X_LANES - 2 * hh,), F32)
    alog_row = jnp.concatenate([jnp.zeros((hh,), F32), a_log, pad]).reshape(1, V7X_LANES)
    dtb_row = jnp.concatenate([jnp.zeros((hh,), F32), dt_bias, pad]).reshape(1, V7X_LANES)
    ba_col = 6 * hh
    return pl.pallas_call(
        _delta_kernel,
        grid=(b, npair, t // tt),
        in_specs=[
            blk(0), halo(0), blk(npair), halo(npair), blk(2 * npair), halo(2 * npair), blk(3 * npair),
            pl.BlockSpec((1, tt, V7X_LANES), lambda bi, h, ti: (bi, ti, ba_col)),
            cw(0), cw(npair), cw(2 * npair), row_spec, row_spec, row_spec,
        ],
        out_specs=pl.BlockSpec((1, tt, wd), lambda bi, h, ti: (bi, ti, h)),
        out_shape=jax.ShapeDtypeStruct((b, t, hh * V7X_LANES), BF16),
        scratch_shapes=[pltpu.VMEM((hps, V7X_LANES, V7X_LANES), F32)],
        compiler_params=_cparams("parallel", "parallel", "arbitrary"),
        name="delta_mixer",
    )(proj, proj, proj, proj, proj, proj, proj, proj, conv_w, conv_w, conv_w,
      alog_row, dtb_row, norm_w.reshape(1, V7X_LANES))


def _lru_kernel(x_ref, xh_ref, y_ref, cw_ref, cb_ref, wa_ref, ba_ref, wx_ref, bx_ref, lam_ref,
                o_ref, h_ref):
    first = pl.program_id(2) == 0
    tt = x_ref.shape[1]

    @pl.when(first)
    def _():
        h_ref[...] = jnp.zeros_like(h_ref)

    wc = x_ref.shape[2]
    xc = _causal_conv(x_ref, xh_ref, cw_ref, first) + cb_ref[...]
    xb = xc.astype(BF16)

    def block_diag(w_ref):
        return jnp.concatenate([_dot(xb[:, g * V7X_LANES:(g + 1) * V7X_LANES], w_ref[g])
                                for g in range(wc // V7X_LANES)], axis=1)

    r = jax.nn.sigmoid(block_diag(wa_ref) + ba_ref[...])
    i = jax.nn.sigmoid(block_diag(wx_ref) + bx_ref[...])
    log_a = -LRU_C * r * jax.nn.softplus(-lam_ref[...])
    a = jnp.exp(log_a)
    u = jnp.sqrt(1.0 - a * a) * (i * xc)

    hr = V7X_SUBLANES
    rowg = lax.broadcasted_iota(jnp.int32, (tt, wc), 0) & (hr - 1)
    s = 1
    while s < hr:
        a_sh = jnp.where(rowg >= s, pltpu.roll(a, s, axis=0), 1.0)
        u_sh = jnp.where(rowg >= s, pltpu.roll(u, s, axis=0), 0.0)
        u = a * u_sh + u
        a = a * a_sh
        s *= 2
    carry = h_ref[...]
    gelu_y = jax.nn.gelu(y_ref[0])
    out_rows = 2 * hr
    for g0 in range(0, tt, out_rows):
        parts = []
        for g in range(g0, g0 + out_rows, hr):
            hs = u[g:g + hr] + a[g:g + hr] * carry
            carry = hs[hr - 1:hr]
            parts.append(hs)
        rows = slice(g0, g0 + out_rows)
        o_ref[0, rows, :] = (jnp.concatenate(parts, axis=0) * gelu_y[rows]).astype(o_ref.dtype)
    h_ref[...] = carry


def _lru_mixer(proj, conv_w, conv_b, wa, ba, wx, bx, lam, *, x_col, y_col, tt_cap=512):
    b, t, _ = proj.shape
    tt = _tile(t, tt_cap, 16)
    gg = LRU_BLOCKS
    gs = LRU_BLOCKS_PER_STEP
    wc = gs * V7X_LANES
    assert gg % gs == 0 and x_col % gs == 0 and y_col % gs == 0
    vec = lambda a: a.reshape(1, gg * V7X_LANES)
    vspec = pl.BlockSpec((1, wc), lambda bi, g, ti: (0, g))
    wspec = pl.BlockSpec((gs, V7X_LANES, V7X_LANES), lambda bi, g, ti: (g, 0, 0))
    return pl.pallas_call(
        _lru_kernel,
        grid=(b, gg // gs, t // tt),
        in_specs=[
            pl.BlockSpec((1, tt, wc), lambda bi, g, ti: (bi, ti, x_col // gs + g)),
            pl.BlockSpec((1, V7X_SUBLANES, wc), _halo_map(lambda g: x_col // gs + g, tt)),
            pl.BlockSpec((1, tt, wc), lambda bi, g, ti: (bi, ti, y_col // gs + g)),
            pl.BlockSpec((4, wc), lambda bi, g, ti: (0, g)),
            vspec, wspec, vspec, wspec, vspec, vspec,
        ],
        out_specs=pl.BlockSpec((1, tt, wc), lambda bi, g, ti: (bi, ti, g)),
        out_shape=jax.ShapeDtypeStruct((b, t, gg * V7X_LANES), BF16),
        scratch_shapes=[pltpu.VMEM((1, wc), F32)],
        compiler_params=_cparams("parallel", "parallel", "arbitrary"),
        name="lru_mixer",
    )(proj, proj, proj, conv_w, vec(conv_b), wa.astype(BF16), vec(ba), wx.astype(BF16), vec(bx), vec(lam))


def _dilated_kernel(q_ref, kc_ref, kp_ref, vc_ref, vp_ref, o_ref, qa_ref, ka_ref, va_ref, sa_ref, sn_ref):
    h = pl.program_id(1)
    has_prev = pl.program_id(2) > 0
    tq = q_ref.shape[1]
    bq = SWA_BLOCK
    d1, d2, d3 = SWA_DILATIONS
    assert d1 == 1 and d3 % d2 == 0
    pw = kp_ref.shape[1]
    qq = tq // d2
    pq = pw // d2
    cs = pq + qq
    st = d3 // d2
    assert pw == bq * d3 and pq >= bq * st
    scale = V7X_LANES ** -0.5
    for r in range(d2):
        qa_ref[r * qq:(r + 1) * qq, :] = q_ref.at[0][pl.ds(r, qq, stride=d2), :] * scale
        for src_p, src_c, dst in ((kp_ref, kc_ref, ka_ref), (vp_ref, vc_ref, va_ref)):
            dst[r * cs:r * cs + pq, :] = src_p.at[0][pl.ds(r, pq, stride=d2), :]
            dst[r * cs + pq:(r + 1) * cs, :] = src_c.at[0][pl.ds(r, qq, stride=d2), :]

    slope = jnp.exp2(-(jnp.full((1, 1), h, jnp.int32).astype(F32) + 1.0) * (8.0 / SWA_HEADS))
    iq = lax.broadcasted_iota(jnp.int32, (bq, 2 * bq), 0)
    ik = lax.broadcasted_iota(jnp.int32, (bq, 2 * bq), 1)
    rel = bq + iq - ik
    in_window = (rel >= 0) & (rel <= SWA_SPAN)
    relf = rel.astype(F32)
    ones_blk = jnp.ones((2 * bq, V7X_LANES), BF16)

    penalty = {d: jnp.where(in_window, (slope * float(d)) * relf, -MASK_VALUE) for d in SWA_DILATIONS}

    def attend(q_scaled, kcat, vcat, d, prev_ok):
        s = _dot_nt(q_scaled.astype(BF16), kcat.astype(BF16)) - penalty[d]
        if prev_ok is not True:
            s = jnp.where((ik >= bq) | prev_ok, s, MASK_VALUE)
        m_b = jnp.max(s, axis=-1, keepdims=True)
        p = jnp.exp(s - m_b).astype(BF16)
        pv = _dot(p, jnp.concatenate([vcat.astype(BF16), ones_blk], axis=1))
        return jnp.broadcast_to(m_b, (bq, V7X_LANES)), pv[:, V7X_LANES:], pv[:, :V7X_LANES]

    def merge(old, new):
        (m_o, l_o, acc_o), (m_b, l_b, acc_b) = old, new
        m_n = jnp.maximum(m_o, m_b)
        alpha = jnp.exp(m_o - m_n)
        beta = jnp.exp(m_b - m_n)
        return m_n, alpha * l_o + beta * l_b, alpha * acc_o + beta * acc_b

    for r in range(d2):
        for n in range(qq // bq):
            q0 = r * qq + n * bq
            k0 = r * cs + pq + (n - 1) * bq
            new = attend(qa_ref[q0:q0 + bq, :], ka_ref[k0:k0 + 2 * bq, :], va_ref[k0:k0 + 2 * bq, :],
                         d2, has_prev if n == 0 else True)
            for kk in range(3):
                sa_ref[kk, q0:q0 + bq, :] = new[kk]

    take = lambda ref, start: ref[pl.ds(start, bq, stride=st), :]
    for r3 in range(d3):
        for n in range(tq // (bq * d3)):
            r = r3 % d2
            p0 = r3 // d2 + n * (bq * st)
            q0 = r * qq + p0
            k0 = r * cs + pq + p0
            kcat = jnp.concatenate([take(ka_ref, k0 - bq * st), take(ka_ref, k0)], axis=0)
            vcat = jnp.concatenate([take(va_ref, k0 - bq * st), take(va_ref, k0)], axis=0)
            new = attend(take(qa_ref, q0), kcat, vcat, d3, has_prev if n == 0 else True)
            out = merge(tuple(take(sa_ref.at[kk], q0) for kk in range(3)), new)
            for kk in range(3):
                sa_ref.at[kk][pl.ds(q0, bq, stride=st), :] = out[kk]

    for r in range(d2):
        for kk in range(3):
            sn_ref.at[kk][pl.ds(r, qq, stride=d2), :] = sa_ref[kk, r * qq:(r + 1) * qq, :]

    for n in range(tq // bq):
        base = n * bq
        if n == 0:
            kcat = jnp.concatenate([kp_ref[0, pw - bq:pw, :], kc_ref[0, 0:bq, :]], axis=0)
            vcat = jnp.concatenate([vp_ref[0, pw - bq:pw, :], vc_ref[0, 0:bq, :]], axis=0)
        else:
            kcat, vcat = kc_ref[0, base - bq:base + bq, :], vc_ref[0, base - bq:base + bq, :]
        new = attend(q_ref[0, base:base + bq, :] * scale, kcat, vcat, d1, has_prev if n == 0 else True)
        _, l_f, acc_f = merge(tuple(sn_ref[kk, base:base + bq, :] for kk in range(3)), new)
        o_ref[0, base:base + bq, :] = (acc_f / l_f).astype(o_ref.dtype)


def _dilated_mixer(proj, *, q_col, k_col, v_col, tq=SWA_TILE):
    b, t, _ = proj.shape
    pw = SWA_BLOCK * max(SWA_DILATIONS)
    tq = min(tq, t)
    assert t % tq == 0 and tq % pw == 0
    hh = SWA_HEADS
    cur = lambda off: pl.BlockSpec((1, tq, V7X_LANES), lambda bi, h, ti, off=off: (bi, ti, off + h))
    prev = lambda off: pl.BlockSpec(
        (1, pw, V7X_LANES), lambda bi, h, ti, off=off: (bi, jnp.maximum(ti * (tq // pw) - 1, 0), off + h))
    return pl.pallas_call(
        _dilated_kernel,
        grid=(b, hh, t // tq),
        in_specs=[cur(q_col), cur(k_col), prev(k_col), cur(v_col), prev(v_col)],
        out_specs=pl.BlockSpec((1, tq, V7X_LANES), lambda bi, h, ti: (bi, ti, h)),
        out_shape=jax.ShapeDtypeStruct((b, t, hh * V7X_LANES), BF16),
        scratch_shapes=[pltpu.VMEM((tq, V7X_LANES), F32), pltpu.VMEM((pw + tq, V7X_LANES), F32),
                        pltpu.VMEM((pw + tq, V7X_LANES), F32), pltpu.VMEM((3, tq, V7X_LANES), F32),
                        pltpu.VMEM((3, tq, V7X_LANES), F32)],
        compiler_params=_cparams("parallel", "parallel", "parallel"),
        name="dilated_mixer",
    )(proj, proj, proj, proj, proj)


def _retention_kernel(q_ref, k_ref, v_ref, g_ref, o_ref, s_ref):
    h = pl.program_id(1)
    tt = q_ref.shape[1]
    c = min(RET_CHUNK, tt)
    nch = tt // c

    @pl.when(pl.program_id(2) == 0)
    def _():
        s_ref[...] = jnp.zeros_like(s_ref)

    hf = jnp.full((1, 1), h, jnp.int32).astype(F32)
    log_gamma = jnp.log1p(-jnp.exp2(-5.0 - hf))
    row = lax.broadcasted_iota(jnp.int32, (c, c), 0)
    col = lax.broadcasted_iota(jnp.int32, (c, c), 1)
    rel = (row - col).astype(F32)
    dmask = jnp.where(rel >= 0, jnp.exp(jnp.maximum(rel, 0.0) * log_gamma), 0.0)
    idx = lax.broadcasted_iota(jnp.int32, (c, 1), 0).astype(F32)
    q_scale = jnp.exp((idx + 1.0) * log_gamma)
    k_scale = jnp.exp((c - 1.0 - idx) * log_gamma)
    chunk_decay = jnp.exp(float(c) * log_gamma)

    sls = [slice(ci * c, (ci + 1) * c) for ci in range(nch)]
    qs = [q_ref[0, sl, :] for sl in sls]
    ks = [k_ref[0, sl, :] * (V7X_LANES ** -0.5) for sl in sls]
    vs = [v_ref[0, sl, :].astype(BF16) for sl in sls]
    intras = [_dot((_dot_nt(q.astype(BF16), k.astype(BF16)) * dmask).astype(BF16), v)
              for q, k, v in zip(qs, ks, vs)]
    kvs = [_dot_tn((k * k_scale).astype(BF16), v) for k, v in zip(ks, vs)]
    states = [s_ref[...]]
    for kv in kvs:
        states.append(states[-1] * chunk_decay + kv)
    s_ref[...] = states[-1]
    for sl, q, intra, state in zip(sls, qs, intras, states):
        o = intra + _dot((q * q_scale).astype(BF16), state.astype(BF16))
        mu = jnp.mean(o, axis=-1, keepdims=True)
        oc = o - mu
        o = oc * lax.rsqrt(jnp.mean(oc * oc, axis=-1, keepdims=True) + GN_EPS)
        o_ref[0, sl, :] = (o * _silu(g_ref[0, sl, :])).astype(o_ref.dtype)


def _retention_mixer(proj, *, q_col, k_col, v_col, g_col):
    b, t, _ = proj.shape
    c = _tile(t, RET_TILE, RET_CHUNK)
    hh = RET_HEADS
    wide = RET_DV // V7X_LANES
    assert v_col % wide == 0 and g_col % wide == 0
    nar = lambda off: pl.BlockSpec((1, c, V7X_LANES), lambda bi, h, ti, off=off: (bi, ti, off + h))
    wid = lambda off: pl.BlockSpec((1, c, RET_DV), lambda bi, h, ti, off=off: (bi, ti, off // wide + h))
    return pl.pallas_call(
        _retention_kernel,
        grid=(b, hh, t // c),
        in_specs=[nar(q_col), nar(k_col), wid(v_col), wid(g_col)],
        out_specs=pl.BlockSpec((1, c, RET_DV), lambda bi, h, ti: (bi, ti, h)),
        out_shape=jax.ShapeDtypeStruct((b, t, hh * RET_DV), BF16),
        scratch_shapes=[pltpu.VMEM((V7X_LANES, RET_DV), F32)],
        compiler_params=_cparams("parallel", "parallel", "arbitrary"),
        name="retention_mixer",
    )(proj, proj, proj, proj)


def _even_w_in_kernel(x_ref, o_ref):
    qkvz = 4 * DN_HEADS * V7X_LANES
    nba = 2 * DN_HEADS
    lru = 2 * LRU_BLOCKS * V7X_LANES
    rows = x_ref.shape[0]
    o_ref[:, :qkvz] = x_ref[:, :qkvz].astype(BF16)
    o_ref[:, qkvz:qkvz + lru] = x_ref[:, qkvz + nba:qkvz + nba + lru].astype(BF16)
    tail = jnp.concatenate([x_ref[:, qkvz:qkvz + nba], jnp.zeros((rows, 2 * V7X_LANES - nba), F32)], axis=1)
    o_ref[:, qkvz + lru:] = tail.astype(BF16)


def _even_w_in_bf16(w):
    l, d, c = w.shape
    n = 4 * DN_HEADS * V7X_LANES + 2 * LRU_BLOCKS * V7X_LANES + 2 * V7X_LANES
    tr = _tile(d, 256, 16)
    return pl.pallas_call(
        _even_w_in_kernel,
        grid=(l, d // tr),
        in_specs=[pl.BlockSpec((None, tr, c), lambda i, j: (i, j, 0))],
        out_specs=pl.BlockSpec((None, tr, n), lambda i, j: (i, j, 0)),
        out_shape=jax.ShapeDtypeStruct((l, d, n), BF16),
        compiler_params=_cparams("parallel", "parallel"),
        name="even_w_in_bf16",
    )(w)


def kernel(x, p, ln_mix_w, ln_mlp_w, ln_ple_w, w_up, w_down, w_ple_proj, w_ple_gate, ln_final_w,
           ev_w_in, ev_w_out, dn_conv_w, dn_a_log, dn_dt_bias, dn_norm_w,
           lru_conv_w, lru_conv_b, lru_wa, lru_ba, lru_wx, lru_bx, lru_lambda,
           od_w_in, od_w_out):
    b, t, d = x.shape
    depth = ln_mix_w.shape[0]
    m = b * t
    h = x.reshape(m, d)
    w_ple_proj, w_ple_gate, ev_w_out, od_w_in, od_w_out = (
        _to_bf16(w) for w in (w_ple_proj, w_ple_gate, ev_w_out, od_w_in, od_w_out))
    ev_w_in = _even_w_in_bf16(ev_w_in)
    w_up = _to_bf16(w_up, tn=MLP_UP_TN)
    w_down = _to_bf16(w_down)
    hn, mix_norm_w = h, ln_mix_w[0]
    for i in range(depth):
        j = i // 2
        if i % 2 == 0:
            proj = _norm_matmul(hn, mix_norm_w, ev_w_in, layer=j, act=None, out_dtype=F32)
            proj = proj.reshape(b, t, -1)
            y_a = _delta_mixer(proj, dn_conv_w[j], dn_a_log[j], dn_dt_bias[j], dn_norm_w[j])
            y_b = _lru_mixer(proj, lru_conv_w[j], lru_conv_b[j], lru_wa[j], lru_ba[j], lru_wx[j], lru_bx[j],
                             lru_lambda[j], x_col=4 * DN_HEADS, y_col=4 * DN_HEADS + LRU_BLOCKS)
            w_out = ev_w_out
        else:
            proj = _norm_matmul(hn, mix_norm_w, od_w_in, layer=j, act=None, out_dtype=F32)
            proj = proj.reshape(b, t, -1)
            y_a = _dilated_mixer(proj, q_col=0, k_col=SWA_HEADS, v_col=2 * SWA_HEADS)
            y_b = _retention_mixer(proj, q_col=3 * SWA_HEADS, k_col=3 * SWA_HEADS + RET_HEADS,
                                   v_col=3 * SWA_HEADS + 2 * RET_HEADS,
                                   g_col=3 * SWA_HEADS + 2 * RET_HEADS + RET_HEADS * RET_DV // V7X_LANES)
            w_out = od_w_out
        h, hn = _matmul_residual([y_a.reshape(m, -1), y_b.reshape(m, -1)], w_out, h, layer=j,
                                 next_norm_w=ln_mlp_w[i], tn_cap=d)
        up = _norm_matmul(hn, None, w_up, layer=i, act="relu2", out_dtype=BF16, tm_cap=MLP_UP_TM)
        h = _matmul_residual_ksplit(up, w_down, h, layer=i)
        if i == depth - 1:
            return _ple(h, ln_ple_w[i], w_ple_gate, p, w_ple_proj, ln_final_w, layer=i,
                        final_norm=True).reshape(b, t, d)
        h, hn = _ple(h, ln_ple_w[i], w_ple_gate, p, w_ple_proj, ln_mix_w[i + 1], layer=i, final_norm=False)
        mix_norm_w = None
```
